```python
import math
import jax
import jax.numpy as jnp
from jax import lax
import numpy as np

D_MODEL = 1024
BATCH = 4
SEQ = 8192
DEPTH = 1

CHUNK = 64
Q_BLOCK = 128
ROPE_THETA = 10000.0
NORM_EPS = 1e-6

DA_HEADS = 8
DA_HEAD_DIM = 64
DA_V_DIM = 2 * DA_HEAD_DIM
DA_QK_WIDTH = DA_HEADS * 2 * DA_HEAD_DIM
DA_V_WIDTH = DA_HEADS * DA_V_DIM

ML_HEADS = 4
ML_INNER = 2 * D_MODEL
ML_HEAD_DIM = ML_INNER // ML_HEADS
ML_CONV = 4
ML_QKV_BLOCK = 4

MEM_LEN = 256
CA_HEADS = 4
CA_HEAD_DIM = D_MODEL // CA_HEADS
CA_WIDTH = CA_HEADS * CA_HEAD_DIM

N_BRANCH = 3
IN_SIZES = (DA_QK_WIDTH, DA_QK_WIDTH, DA_V_WIDTH, ML_INNER, ML_INNER, CA_WIDTH, N_BRANCH * D_MODEL)
IN_WIDTH = DA_QK_WIDTH * 2 + DA_V_WIDTH + ML_INNER * 2 + CA_WIDTH + N_BRANCH * D_MODEL

N_EXPERTS = 32
TOP_K = 4
D_FF = D_MODEL
SWIGLU_LIMIT = 7.0
SWIGLU_ALPHA = 1.702

kernel_name = 'hybrid_diffattn_mlstm_memxattn_moe_block'


def rms_norm(x, g):
    xf = x.astype(jnp.float32)
    y = xf * lax.rsqrt(jnp.mean(xf * xf, axis=-1, keepdims=True) + NORM_EPS)
    return (y * g.astype(jnp.float32)).astype(x.dtype)


def rope(x, pos):
    d = x.shape[-1]
    half = d // 2
    inv = ROPE_THETA ** (-(jnp.arange(half, dtype=jnp.float32) * 2.0 / d))
    ang = pos.astype(jnp.float32)[:, None] * inv[None, :]
    cos, sin = jnp.cos(ang), jnp.sin(ang)
    xf = x.astype(jnp.float32)
    x1, x2 = xf[..., :half], xf[..., half:]
    out = jnp.concatenate([x1 * cos - x2 * sin, x2 * cos + x1 * sin], axis=-1)
    return out.astype(x.dtype)


def split_cols(t, sizes):
    outs, off = [], 0
    for s in sizes:
        outs.append(t[..., off:off + s])
        off += s
    return outs


def diff_attention(q, k, v, q_g, k_g, lq1, lk1, lq2, lk2, sub_g, lambda_init):
    B, S = q.shape[0], q.shape[1]
    pos = jnp.arange(S)
    q = rope(jnp.transpose(rms_norm(q, q_g), (0, 2, 3, 1, 4)), pos)
    k = rope(jnp.transpose(rms_norm(k, k_g), (0, 2, 3, 1, 4)), pos)
    v = jnp.transpose(v, (0, 2, 1, 3))
    f32 = jnp.float32
    lam = (jnp.exp(jnp.sum(lq1.astype(f32) * lk1.astype(f32)))
           - jnp.exp(jnp.sum(lq2.astype(f32) * lk2.astype(f32))) + lambda_init)
    n_blocks = S // Q_BLOCK
    qb = jnp.moveaxis(q.reshape(B, DA_HEADS, 2, n_blocks, Q_BLOCK, DA_HEAD_DIM), 3, 0)
    key_chunk = jnp.arange(S) // CHUNK
    scale = DA_HEAD_DIM ** -0.5

    def one_block(args):
        q_blk, blk = args
        q_chunk = (blk * Q_BLOCK + jnp.arange(Q_BLOCK)) // CHUNK
        allowed = key_chunk[None, :] <= q_chunk[:, None]
        s = jnp.einsum('bhcqd,bhckd->bhcqk', q_blk, k, preferred_element_type=f32) * scale
        p = jax.nn.softmax(jnp.where(allowed, s, -jnp.inf), axis=-1)
        a = p[:, :, 0] - lam * p[:, :, 1]
        return jnp.einsum('bhqk,bhkv->bhqv', a.astype(v.dtype), v)

    o = lax.map(one_block, (qb, jnp.arange(n_blocks)))
    o = jnp.moveaxis(o, 0, 2).reshape(B, DA_HEADS, S, DA_V_DIM).transpose(0, 2, 1, 3)
    o = rms_norm(o, sub_g) * (1.0 - lambda_init)
    return o.reshape(B, S, DA_V_WIDTH)


def causal_conv(x, w, b):
    K, C = w.shape
    y = lax.conv_general_dilated(x, w[:, None, :].astype(x.dtype), window_strides=(1,),
                                 padding=[(K - 1, 0)], dimension_numbers=('NWC', 'WIO', 'NWC'),
                                 feature_group_count=C)
    return y + b


def blockdiag(x, w):
    B, S, C = x.shape
    nb, bs, _ = w.shape
    return jnp.einsum('bsni,nio->bsno', x.reshape(B, S, nb, bs), w).reshape(B, S, C)


def mlstm_chunkwise(q, k, v, i_pre, f_pre):
    B, H, S, d = q.shape
    L = CHUNK
    NC = S // L

    def to_chunks(t):
        return jnp.moveaxis(t.reshape((B, H, NC, L) + t.shape[3:]), 2, 0)

    qc, kc, vc = to_chunks(q), to_chunks(k), to_chunks(v)
    ic = to_chunks(i_pre)
    bc = jnp.cumsum(to_chunks(jax.nn.log_sigmoid(f_pre)), axis=-1)
    causal = jnp.tril(jnp.ones((L, L), dtype=bool))

    def step(carry, inp):
        C, n, m = carry
        qt, kt, vt, it, bt = inp
        dlog = jnp.where(causal, bt[..., :, None] - bt[..., None, :] + it[..., None, :], -jnp.inf)
        inter_log = bt + m[..., None]
        m_row = jnp.maximum(inter_log, jnp.max(dlog, axis=-1))
        dw = jnp.exp(dlog - m_row[..., None])
        inter_w = jnp.exp(inter_log - m_row)
        s = jnp.einsum('bhld,bhsd->bhls', qt, kt) * dw
        num = jnp.einsum('bhls,bhsd->bhld', s, vt) + inter_w[..., None] * jnp.einsum('bhld,bhde->bhle', qt, C)
        den = jnp.sum(s, axis=-1) + inter_w * jnp.einsum('bhld,bhd->bhl', qt, n)
        h = num / jnp.maximum(jnp.abs(den), jnp.exp(-m_row))[..., None]
        b_last = bt[..., -1]
        state_log = b_last[..., None] - bt + it
        m_new = jnp.maximum(b_last + m, jnp.max(state_log, axis=-1))
        decay = jnp.exp(b_last + m - m_new)
        ws = jnp.exp(state_log - m_new[..., None])
        C_new = decay[..., None, None] * C + jnp.einsum('bhs,bhsd,bhse->bhde', ws, kt, vt)
        n_new = decay[..., None] * n + jnp.einsum('bhs,bhsd->bhd', ws, kt)
        return (C_new, n_new, m_new), h

    init = (jnp.zeros((B, H, d, d), jnp.float32), jnp.zeros((B, H, d), jnp.float32),
            jnp.zeros((B, H), jnp.float32))
    _, hs = lax.scan(step, init, (qc, kc, vc, ic, bc))
    return jnp.moveaxis(hs, 0, 2).reshape(B, H, S, d)


def mlstm_branch(x_m, z, conv_w, conv_b, wq, wk, wv, w_if, b_if, norm_g, skip):
    B, S, _ = x_m.shape
    x_c = jax.nn.silu(causal_conv(x_m, conv_w, conv_b))
    q = blockdiag(x_c, wq)
    k = blockdiag(x_c, wk)
    v = blockdiag(x_m, wv)
    gates = (jnp.concatenate([q, k, v], axis=-1) @ w_if + b_if).astype(jnp.float32)
    i_pre = jnp.transpose(gates[..., :ML_HEADS], (0, 2, 1))
    f_pre = jnp.transpose(gates[..., ML_HEADS:], (0, 2, 1))

    def heads(t):
        return t.reshape(B, S, ML_HEADS, ML_HEAD_DIM).transpose(0, 2, 1, 3).astype(jnp.float32)

    hh = mlstm_chunkwise(heads(q), heads(k) * (ML_HEAD_DIM ** -0.5), heads(v), i_pre, f_pre)
    mu = jnp.mean(hh, axis=-1, keepdims=True)
    var = jnp.mean(jnp.square(hh - mu), axis=-1, keepdims=True)
    hh = (hh - mu) * lax.rsqrt(var + NORM_EPS)
    hh = hh.transpose(0, 2, 1, 3).reshape(B, S, ML_INNER) * norm_g.astype(jnp.float32)
    return (hh.astype(x_m.dtype) + skip * x_c) * jax.nn.silu(z)


def memory_cross_attention(q, mem, mem_g, w_kv, q_g, k_g):
    B, S = q.shape[0], q.shape[1]
    M = mem.shape[1]
    q = rms_norm(q.reshape(B, S, CA_HEADS, CA_HEAD_DIM), q_g)
    kv = rms_norm(mem, mem_g) @ w_kv
    k = rms_norm(kv[..., :CA_WIDTH].reshape(B, M, CA_HEADS, CA_HEAD_DIM), k_g)
    v = kv[..., CA_WIDTH:].reshape(B, M, CA_HEADS, CA_HEAD_DIM)
    s = jnp.einsum('bshd,bmhd->bhsm', q, k, preferred_element_type=jnp.float32) * (CA_HEAD_DIM ** -0.5)
    p = jax.nn.softmax(s, axis=-1)
    o = jnp.einsum('bhsm,bmhd->bshd', p.astype(v.dtype), v)
    return o.reshape(B, S, CA_WIDTH)


def moe_ffn(xn, router_w, router_b, w_gate_up, b_gate_up, w_down, b_down):
    B, S, D = xn.shape
    T = B * S
    xt = xn.reshape(T, D)
    logits = (xt @ router_w).astype(jnp.float32) + router_b.astype(jnp.float32)
    top_v, top_i = lax.top_k(logits, TOP_K)
    top_w = jax.nn.softmax(top_v, axis=-1)
    e_flat = top_i.reshape(-1)
    order = jnp.argsort(e_flat)
    e_sorted = e_flat[order]
    tok = order // TOP_K
    group_sizes = jnp.bincount(e_flat, length=N_EXPERTS).astype(jnp.int32)
    xs = xt[tok]
    h = lax.ragged_dot(xs, w_gate_up, group_sizes) + b_gate_up[e_sorted]
    gate, up = h[:, :D_FF], h[:, D_FF:]
    gate = jnp.minimum(gate, SWIGLU_LIMIT)
    up = jnp.clip(up, -SWIGLU_LIMIT, SWIGLU_LIMIT)
    a = (up + 1.0) * (gate * jax.nn.sigmoid(SWIGLU_ALPHA * gate))
    y = lax.ragged_dot(a, w_down, group_sizes) + b_down[e_sorted]
    y = y * top_w.reshape(-1)[order][:, None].astype(y.dtype)
    out = jax.ops.segment_sum(y, tok, num_segments=T)
    return out.reshape(B, S, D)


def setup_inputs(seed: int = 0) -> dict:
    key = jax.random.key(seed)
    ks = iter(jax.random.split(key, 40))
    Ld = DEPTH

    def nrm(shape, scale):
        return jax.random.normal(next(ks), shape, jnp.float32) * scale

    def gain(shape):
        return 1.0 + nrm(shape, 0.02)

    inp = {}
    inp['x'] = nrm((BATCH, SEQ, D_MODEL), 1.0)
    inp['mem'] = nrm((BATCH, MEM_LEN, D_MODEL), 1.0)
    inp['attn_norm_g'] = gain((Ld, D_MODEL))
    inp['w_in'] = nrm((Ld, D_MODEL, IN_WIDTH), D_MODEL ** -0.5)
    inp['b_gate'] = nrm((Ld, N_BRANCH * D_MODEL), 0.02)
    inp['da_q_norm_g'] = gain((Ld, DA_HEAD_DIM))
    inp['da_k_norm_g'] = gain((Ld, DA_HEAD_DIM))
    inp['da_lambda_q1'] = nrm((Ld, DA_HEAD_DIM), 0.1)
    inp['da_lambda_k1'] = nrm((Ld, DA_HEAD_DIM), 0.1)
    inp['da_lambda_q2'] = nrm((Ld, DA_HEAD_DIM), 0.1)
    inp['da_lambda_k2'] = nrm((Ld, DA_HEAD_DIM), 0.1)
    inp['da_subln_g'] = gain((Ld, DA_V_DIM))
    inp['ml_conv_w'] = nrm((Ld, ML_CONV, ML_INNER), ML_CONV ** -0.5)
    inp['ml_conv_b'] = nrm((Ld, ML_INNER), 0.02)
    nb = ML_INNER // ML_QKV_BLOCK
    inp['ml_wq'] = nrm((Ld, nb, ML_QKV_BLOCK, ML_QKV_BLOCK), ML_QKV_BLOCK ** -0.5)
    inp['ml_wk'] = nrm((Ld, nb, ML_QKV_BLOCK, ML_QKV_BLOCK), ML_QKV_BLOCK ** -0.5)
    inp['ml_wv'] = nrm((Ld, nb, ML_QKV_BLOCK, ML_QKV_BLOCK), ML_QKV_BLOCK ** -0.5)
    inp['ml_w_if'] = nrm((Ld, 3 * ML_INNER, 2 * ML_HEADS), (3 * ML_INNER) ** -0.5)
    inp['ml_b_if'] = jnp.concatenate(
        [nrm((Ld, ML_HEADS), 0.01),
         jnp.linspace(3.0, 6.0, ML_HEADS, dtype=jnp.float32)[None, :] + nrm((Ld, ML_HEADS), 0.01)], axis=-1)
    inp['ml_out_norm_g'] = gain((Ld, ML_INNER))
    inp['ml_skip'] = gain((Ld, ML_INNER))
    inp['mem_norm_g'] = gain((Ld, D_MODEL))
    inp['ca_w_kv'] = nrm((Ld, D_MODEL, 2 * CA_WIDTH), D_MODEL ** -0.5)
    inp['ca_q_norm_g'] = gain((Ld, CA_HEAD_DIM))
    inp['ca_k_norm_g'] = gain((Ld, CA_HEAD_DIM))
    inp['w_branch_da'] = nrm((Ld, DA_V_WIDTH, D_MODEL), DA_V_WIDTH ** -0.5)
    inp['w_branch_ml'] = nrm((Ld, ML_INNER, D_MODEL), ML_INNER ** -0.5)
    inp['w_branch_ca'] = nrm((Ld, CA_WIDTH, D_MODEL), CA_WIDTH ** -0.5)
    inp['w_out'] = nrm((Ld, D_MODEL, D_MODEL), D_MODEL ** -0.5)
    inp['ffn_norm_g'] = gain((Ld, D_MODEL))
    inp['router_w'] = nrm((Ld, D_MODEL, N_EXPERTS), D_MODEL ** -0.5)
    inp['router_b'] = nrm((Ld, N_EXPERTS), 0.01)
    inp['w_gate_up'] = nrm((Ld, N_EXPERTS, D_MODEL, 2 * D_FF), D_MODEL ** -0.5)
    inp['b_gate_up'] = nrm((Ld, N_EXPERTS, 2 * D_FF), 0.02)
    inp['w_down'] = nrm((Ld, N_EXPERTS, D_FF, D_MODEL), D_FF ** -0.5)
    inp['b_down'] = nrm((Ld, N_EXPERTS, D_MODEL), 0.02)
    return inp


def reference(x, mem, attn_norm_g, w_in, b_gate, da_q_norm_g, da_k_norm_g, da_lambda_q1, da_lambda_k1,
              da_lambda_q2, da_lambda_k2, da_subln_g, ml_conv_w, ml_conv_b, ml_wq, ml_wk, ml_wv, ml_w_if,
              ml_b_if, ml_out_norm_g, ml_skip, mem_norm_g, ca_w_kv, ca_q_norm_g, ca_k_norm_g, w_branch_da,
              w_branch_ml, w_branch_ca, w_out, ffn_norm_g, router_w, router_b, w_gate_up, b_gate_up, w_down,
              b_down):
    B, S, D = x.shape
    h = x
    for l in range(DEPTH):
        lambda_init = 0.8 - 0.6 * math.exp(-0.3 * l)
        xn = rms_norm(h, attn_norm_g[l])
        proj = xn @ w_in[l]
        da_q, da_k, da_v, ml_x, ml_z, ca_q, gate_pre = split_cols(proj, IN_SIZES)
        y_da = diff_attention(da_q.reshape(B, S, DA_HEADS, 2, DA_HEAD_DIM),
                              da_k.reshape(B, S, DA_HEADS, 2, DA_HEAD_DIM),
                              da_v.reshape(B, S, DA_HEADS, DA_V_DIM),
                              da_q_norm_g[l], da_k_norm_g[l], da_lambda_q1[l], da_lambda_k1[l],
                              da_lambda_q2[l], da_lambda_k2[l], da_subln_g[l], lambda_init)
        y_ml = mlstm_branch(ml_x, ml_z, ml_conv_w[l], ml_conv_b[l], ml_wq[l], ml_wk[l], ml_wv[l],
                            ml_w_if[l], ml_b_if[l], ml_out_norm_g[l], ml_skip[l])
        y_ca = memory_cross_attention(ca_q, mem, mem_norm_g[l], ca_w_kv[l], ca_q_norm_g[l], ca_k_norm_g[l])
        g = jax.nn.sigmoid((gate_pre + b_gate[l]).astype(jnp.float32)).astype(h.dtype)
        g = g.reshape(B, S, N_BRANCH, D)
        mix = (g[:, :, 0] * (y_da @ w_branch_da[l])
               + g[:, :, 1] * (y_ml @ w_branch_ml[l])
               + g[:, :, 2] * (y_ca @ w_branch_ca[l]))
        h = h + mix @ w_out[l]
        h = h + moe_ffn(rms_norm(h, ffn_norm_g[l]), router_w[l], router_b[l], w_gate_up[l], b_gate_up[l],
                        w_down[l], b_down[l])
    return h
```

```python
import functools
import math

import jax
import jax.numpy as jnp
from jax import lax
from jax.experimental import pallas as pl
from jax.experimental.pallas import tpu as pltpu

F32 = jnp.float32
BF16 = jnp.bfloat16
I32 = jnp.int32
U32 = jnp.uint32

NORM_EPS = 1e-6
ROPE_THETA = 10000.0
CHUNK = 64

DA_HEADS = 8
DA_HEAD_DIM = 64
DA_V_DIM = 128
ML_HEADS = 4
ML_CONV = 4
ML_QKV_BLOCK = 4
CA_HEADS = 4
N_BRANCH = 3
TOP_K = 4
SWIGLU_LIMIT = 7.0
SWIGLU_ALPHA = 1.702

LANES = 128
SUBLANES = 8
MXU_DIM = 256
VMEM_LIMIT = 56 * 1024 * 1024
NEG_BIG = -1e30


def _cparams(sem):
    return pltpu.CompilerParams(dimension_semantics=sem, vmem_limit_bytes=VMEM_LIMIT)


def _silu(x):
    return x * jax.nn.sigmoid(x)


def _inproj_body(x_ref, g_ref, w_ref, b_ref, o_ref, xn_ref, *, n_plain):
    j = pl.program_id(1)

    @pl.when(j == 0)
    def _():
        x = x_ref[...]
        ms = jnp.mean(x * x, axis=-1, keepdims=True)
        xn_ref[...] = (x * lax.rsqrt(ms + NORM_EPS) * g_ref[...]).astype(BF16)

    acc = jnp.dot(xn_ref[...], w_ref[...], preferred_element_type=F32)

    @pl.when(j < n_plain)
    def _():
        o_ref[...] = acc.astype(o_ref.dtype)

    @pl.when(j >= n_plain)
    def _():
        o_ref[...] = jax.nn.sigmoid(acc + b_ref[...]).astype(o_ref.dtype)


def _inproj(x2, g, w_bf, b_gate, *, tm, tn, n_plain):
    T, D = x2.shape
    N = w_bf.shape[1]
    return pl.pallas_call(
        functools.partial(_inproj_body, n_plain=n_plain),
        grid=(T // tm, N // tn),
        in_specs=[
            pl.BlockSpec((tm, D), lambda i, j: (i, 0)),
            pl.BlockSpec((1, D), lambda i, j: (0, 0)),
            pl.BlockSpec((D, tn), lambda i, j: (0, j)),
            pl.BlockSpec((1, tn), lambda i, j: (0, jnp.maximum(j - n_plain, 0))),
        ],
        out_specs=pl.BlockSpec((tm, tn), lambda i, j: (i, j)),
        out_shape=jax.ShapeDtypeStruct((T, N), BF16),
        scratch_shapes=[pltpu.VMEM((tm, D), BF16)],
        compiler_params=_cparams(("arbitrary", "arbitrary")),
        name="inproj",
    )(x2, g, w_bf, b_gate)


def _group_sumsq(xb, gm):
    sq = xb * xb
    hi = sq.astype(BF16)
    lo = (sq - hi.astype(F32)).astype(BF16)
    return (jnp.dot(hi, gm, preferred_element_type=F32)
            + jnp.dot(lo, gm, preferred_element_type=F32))


def _qkrope_body(x_ref, g_ref, cos_ref, sin_ref, gm_ref, o_ref, *, ncb, scale):
    c = pl.program_id(1)
    g = g_ref[...] * jnp.where(c == 0, scale, 1.0).astype(F32)
    cos = cos_ref[...]
    sin = sin_ref[...]
    gm = gm_ref[...]
    lane = lax.broadcasted_iota(I32, (1, LANES), 1)
    first_half = (lane & (DA_HEAD_DIM - 1)) < (DA_HEAD_DIM // 2)
    for cb in range(ncb):
        sl = slice(cb * LANES, (cb + 1) * LANES)
        xb = x_ref[:, sl].astype(F32)
        ss = _group_sumsq(xb, gm)
        y = xb * lax.rsqrt(ss * (1.0 / DA_HEAD_DIM) + NORM_EPS) * g
        sw = jnp.where(first_half, pltpu.roll(y, LANES - DA_HEAD_DIM // 2, 1),
                       pltpu.roll(y, DA_HEAD_DIM // 2, 1))
        o_ref[:, sl] = (y * cos + sw * sin).astype(o_ref.dtype)


def _qkrope(proj, g2, cos_t, sin_t, gm, *, T, S, width, tm):
    n_s = S // tm
    return pl.pallas_call(
        functools.partial(_qkrope_body, ncb=width // LANES, scale=DA_HEAD_DIM ** -0.5),
        grid=(T // tm, 2),
        in_specs=[
            pl.BlockSpec((tm, width), lambda i, c: (i, c)),
            pl.BlockSpec((None, 1, LANES), lambda i, c: (c, 0, 0)),
            pl.BlockSpec((tm, LANES), lambda i, c: (i % n_s, 0)),
            pl.BlockSpec((tm, LANES), lambda i, c: (i % n_s, 0)),
            pl.BlockSpec((LANES, LANES), lambda i, c: (0, 0)),
        ],
        out_specs=pl.BlockSpec((tm, width), lambda i, c: (i, c)),
        out_shape=jax.ShapeDtypeStruct((T, 2 * width), BF16),
        compiler_params=_cparams(("arbitrary", "arbitrary")),
        name="qkrope",
    )(proj, g2, cos_t, sin_t, gm)


def _da_body(q_ref, k_ref, v_ref, lq1_ref, lk1_ref, lq2_ref, lk2_ref, subg_ref, o_ref,
             m_ref, l_ref, acc_ref, *, tq, lambda_init):
    i = pl.program_id(2)
    q = q_ref[...]
    lane = lax.broadcasted_iota(I32, (1, LANES), 1)
    lo = lane < DA_HEAD_DIM
    zero = jnp.zeros_like(q)
    q2 = jnp.concatenate([jnp.where(lo, q, zero), jnp.where(lo, zero, q)], axis=0)

    m_ref[...] = jnp.full(m_ref.shape, NEG_BIG, F32)
    l_ref[...] = jnp.zeros(l_ref.shape, F32)
    acc_ref[...] = jnp.zeros(acc_ref.shape, F32)

    def step(j, masked):
        start = pl.multiple_of(j * tq, tq)
        kj = k_ref[pl.ds(start, tq), :]
        vj = v_ref[pl.ds(start, tq), :]
        s = lax.dot_general(q2, kj, (((1,), (1,)), ((), ())), preferred_element_type=F32)
        if masked:
            r = lax.broadcasted_iota(I32, (2 * tq, tq), 0) & (tq - 1)
            cidx = lax.broadcasted_iota(I32, (2 * tq, tq), 1)
            s = jnp.where((cidx | (CHUNK - 1)) <= (r | (CHUNK - 1)), s, NEG_BIG)
        m_old = m_ref[...]
        m_new = jnp.maximum(m_old, jnp.max(s, axis=-1, keepdims=True))
        p = jnp.exp(s - m_new)
        alpha = jnp.exp(m_old - m_new)
        l_ref[...] = alpha * l_ref[...] + jnp.sum(p, axis=-1, keepdims=True)
        acc_ref[...] = alpha * acc_ref[...] + jnp.dot(p.astype(BF16), vj, preferred_element_type=F32)
        m_ref[...] = m_new

    def body(j, carry):
        step(j, False)
        return carry

    lax.fori_loop(0, i, body, 0)
    step(i, True)

    f32 = F32
    lam = (jnp.exp(jnp.sum(lq1_ref[...].astype(f32) * lk1_ref[...].astype(f32), keepdims=True))
           - jnp.exp(jnp.sum(lq2_ref[...].astype(f32) * lk2_ref[...].astype(f32), keepdims=True))
           + lambda_init)
    o_all = acc_ref[...] / l_ref[...]
    o = o_all[:tq] - lam * o_all[tq:]
    ms = jnp.mean(o * o, axis=-1, keepdims=True)
    o = o * lax.rsqrt(ms + NORM_EPS) * subg_ref[...]
    o_ref[...] = (o * (1.0 - lambda_init)).astype(o_ref.dtype)


def _diff_attention(qk, proj, lq1, lk1, lq2, lk2, subg, *, B, S, tq, v_col0, lambda_init):
    T = B * S
    H = DA_HEADS
    n_q = S // tq
    vec = lambda n: pl.BlockSpec((1, n), lambda b, h, i: (0, 0))
    return pl.pallas_call(
        functools.partial(_da_body, tq=tq, lambda_init=lambda_init),
        grid=(B, H, n_q),
        in_specs=[
            pl.BlockSpec((tq, LANES), lambda b, h, i: (b * n_q + i, h)),
            pl.BlockSpec((S, LANES), lambda b, h, i: (b, H + h)),
            pl.BlockSpec((S, LANES), lambda b, h, i: (b, v_col0 + h)),
            vec(DA_HEAD_DIM), vec(DA_HEAD_DIM), vec(DA_HEAD_DIM), vec(DA_HEAD_DIM),
            vec(DA_V_DIM),
        ],
        out_specs=pl.BlockSpec((tq, LANES), lambda b, h, i: (b * n_q + i, h)),
        out_shape=jax.ShapeDtypeStruct((T, H * DA_V_DIM), BF16),
        scratch_shapes=[pltpu.VMEM((2 * tq, 1), F32), pltpu.VMEM((2 * tq, 1), F32),
                        pltpu.VMEM((2 * tq, DA_V_DIM), F32)],
        compiler_params=_cparams(("arbitrary", "arbitrary", "arbitrary")),
        name="diff_attn",
    )(qk, qk, proj, lq1, lk1, lq2, lk2, subg)


def _mlpre_body(x_ref, cw_ref, cb_ref, bd_ref, wif_ref, bif_ref,
                xc_ref, q_ref, k_ref, v_ref, g_ref, prev_ref, *, tm, ncb, kscale):
    s_idx = pl.program_id(1)

    @pl.when(s_idx == 0)
    def _():
        prev_ref[...] = jnp.zeros(prev_ref.shape, F32)

    row8 = lax.broadcasted_iota(I32, (SUBLANES, MXU_DIM), 0)
    gacc = jnp.zeros((tm, LANES), F32) + bif_ref[...]
    for cb in range(ncb):
        sl = slice(cb * MXU_DIM, (cb + 1) * MXU_DIM)
        xb16 = x_ref[:, sl]
        x = xb16.astype(F32)
        prev = prev_ref[:, sl]
        conv = x * cw_ref[ML_CONV - 1:ML_CONV, sl] + cb_ref[:, sl]
        for d in range(1, ML_CONV):
            xs = pltpu.roll(x, d, 0)
            ps = pltpu.roll(prev, d, 0)
            head = jnp.where(row8 < d, ps, xs[:SUBLANES])
            shifted = jnp.concatenate([head, xs[SUBLANES:]], axis=0)
            conv = conv + shifted * cw_ref[ML_CONV - 1 - d:ML_CONV - d, sl]
        prev_ref[:, sl] = x[tm - SUBLANES:]
        xc = _silu(conv)
        xc16 = xc.astype(BF16)
        q = jnp.dot(xc16, bd_ref[0, cb], preferred_element_type=F32)
        k = jnp.dot(xc16, bd_ref[1, cb], preferred_element_type=F32)
        v = jnp.dot(xb16, bd_ref[2, cb], preferred_element_type=F32)
        q16, k16, v16 = q.astype(BF16), k.astype(BF16), v.astype(BF16)
        gacc = gacc + jnp.dot(q16, wif_ref[0, sl, :], preferred_element_type=F32)
        gacc = gacc + jnp.dot(k16, wif_ref[1, sl, :], preferred_element_type=F32)
        gacc = gacc + jnp.dot(v16, wif_ref[2, sl, :], preferred_element_type=F32)
        xc_ref[:, sl] = xc16
        q_ref[:, sl] = q16
        k_ref[:, sl] = (k * kscale).astype(BF16)
        v_ref[:, sl] = v16
    g_ref[...] = gacc


def _mlpre(proj, conv_w, conv_b, bd, wif, bif, *, B, S, C, tm, x_col):
    T = B * S
    n_s = S // tm
    row = lambda i: pl.BlockSpec((tm, C), lambda b, s: (b * n_s + s, i))
    outs = pl.pallas_call(
        functools.partial(_mlpre_body, tm=tm, ncb=C // MXU_DIM, kscale=(C // ML_HEADS) ** -0.5),
        grid=(B, n_s),
        in_specs=[
            row(x_col),
            pl.BlockSpec((ML_CONV, C), lambda b, s: (0, 0)),
            pl.BlockSpec((1, C), lambda b, s: (0, 0)),
            pl.BlockSpec(bd.shape, lambda b, s: (0, 0, 0, 0)),
            pl.BlockSpec(wif.shape, lambda b, s: (0, 0, 0)),
            pl.BlockSpec((1, LANES), lambda b, s: (0, 0)),
        ],
        out_specs=[row(0), row(0), row(0), row(0),
                   pl.BlockSpec((tm, LANES), lambda b, s: (b * n_s + s, 0))],
        out_shape=[jax.ShapeDtypeStruct((T, C), BF16)] * 4 + [jax.ShapeDtypeStruct((T, LANES), F32)],
        scratch_shapes=[pltpu.VMEM((SUBLANES, C), F32)],
        compiler_params=_cparams(("arbitrary", "arbitrary")),
        name="mlstm_pre",
    )(proj, conv_w, conv_b, bd, wif, bif)
    return outs


def _mlstm_body(q_ref, k_ref, v_ref, g_ref, xc_ref, z_ref, ng_ref, skip_ref, o_ref,
                c_ref, m_ref, *, L, d):
    h = pl.program_id(1)
    c_idx = pl.program_id(2)

    @pl.when(c_idx == 0)
    def _():
        c_ref[...] = jnp.zeros(c_ref.shape, F32)
        m_ref[...] = jnp.zeros(m_ref.shape, F32)

    g = g_ref[...]
    gt = g.T
    lane = lax.broadcasted_iota(I32, (1, LANES), 1)
    sub = lax.broadcasted_iota(I32, (LANES, 1), 0)
    i_col = jnp.sum(jnp.where(lane == h, g, 0.0), axis=1, keepdims=True)
    f_col = jnp.sum(jnp.where(lane == ML_HEADS + h, g, 0.0), axis=1, keepdims=True)
    i_row = jnp.sum(jnp.where(sub == h, gt, 0.0), axis=0, keepdims=True)
    f_row = jnp.sum(jnp.where(sub == ML_HEADS + h, gt, 0.0), axis=0, keepdims=True)
    lf_col = jax.nn.log_sigmoid(f_col)
    lf_row = jax.nn.log_sigmoid(f_row)

    t_idx = lax.broadcasted_iota(I32, (L, L), 0)
    s_idx = lax.broadcasted_iota(I32, (L, L), 1)
    tri = s_idx <= t_idx
    b_col = jnp.sum(jnp.where(tri, lf_row, 0.0), axis=1, keepdims=True)
    b_row = jnp.sum(jnp.where(t_idx <= s_idx, lf_col, 0.0), axis=0, keepdims=True)
    b_last = jnp.sum(lf_row, axis=1, keepdims=True)

    m_prev = m_ref[...]
    dlog = jnp.where(tri, b_col - b_row + i_row, NEG_BIG)
    inter_log = b_col + m_prev
    m_rowv = jnp.maximum(inter_log, jnp.max(dlog, axis=1, keepdims=True))
    dw = jnp.exp(dlog - m_rowv)
    inter_w = jnp.exp(inter_log - m_rowv)

    q = q_ref[...]
    k = k_ref[...]
    ones_col = (lax.broadcasted_iota(I32, (L, LANES), 1) == 0).astype(BF16)
    v_aug = jnp.concatenate([v_ref[...], ones_col], axis=1)
    s = lax.dot_general(q, k, (((1,), (1,)), ((), ())), preferred_element_type=F32) * dw
    num_aug = (jnp.dot(s.astype(BF16), v_aug, preferred_element_type=F32)
               + inter_w * jnp.dot(q, c_ref[...].astype(BF16), preferred_element_type=F32))
    num = num_aug[:, :d]
    den = jnp.sum(jnp.where(lane == 0, num_aug[:, d:], 0.0), axis=1, keepdims=True)
    hh = num / jnp.maximum(jnp.abs(den), jnp.exp(-m_rowv))

    state_row = b_last - b_row + i_row
    state_col = b_last - b_col + i_col
    m_new = jnp.maximum(b_last + m_prev, jnp.max(state_row, axis=1, keepdims=True))
    decay = jnp.exp(b_last + m_prev - m_new)
    ws_col = jnp.exp(state_col - m_new)
    kw = (k.astype(F32) * ws_col).astype(BF16)
    c_ref[...] = decay * c_ref[...] + lax.dot_general(
        kw, v_aug, (((0,), (0,)), ((), ())), preferred_element_type=F32)
    m_ref[...] = m_new

    mu = jnp.mean(hh, axis=-1, keepdims=True)
    cen = hh - mu
    var = jnp.mean(cen * cen, axis=-1, keepdims=True)
    hn = cen * lax.rsqrt(var + NORM_EPS) * ng_ref[...]
    y = (hn + skip_ref[...] * xc_ref[...].astype(F32)) * _silu(z_ref[...].astype(F32))
    o_ref[...] = y.astype(o_ref.dtype)


def _mlstm(q, k, v, gates, xc, proj, norm_g, skip, *, B, S, C, L, z_col0):
    T = B * S
    H = ML_HEADS
    d = C // H
    nc = S // L
    blk = lambda off: pl.BlockSpec((L, d), lambda b, h, c: (b * nc + c, off + h))
    par = pl.BlockSpec((1, d), lambda b, h, c: (0, h))
    return pl.pallas_call(
        functools.partial(_mlstm_body, L=L, d=d),
        grid=(B, H, nc),
        in_specs=[blk(0), blk(0), blk(0),
                  pl.BlockSpec((L, LANES), lambda b, h, c: (b * nc + c, 0)),
                  blk(0), blk(z_col0), par, par],
        out_specs=blk(0),
        out_shape=jax.ShapeDtypeStruct((T, C), BF16),
        scratch_shapes=[pltpu.VMEM((d, d + LANES), F32), pltpu.VMEM((1, 1), F32)],
        compiler_params=_cparams(("arbitrary", "arbitrary", "arbitrary")),
        name="mlstm_scan",
    )(q, k, v, gates, xc, proj, norm_g, skip)


def _memkv_body(mem_ref, mg_ref, w_ref, kg_ref, k_ref, v_ref, *, W, dh):
    x = mem_ref[...]
    ms = jnp.mean(x * x, axis=-1, keepdims=True)
    xn = (x * lax.rsqrt(ms + NORM_EPS) * mg_ref[...]).astype(BF16)
    kv = jnp.dot(xn, w_ref[...], preferred_element_type=F32)
    for hd in range(W // dh):
        sl = slice(hd * dh, (hd + 1) * dh)
        kh = kv[:, sl]
        msk = jnp.mean(kh * kh, axis=-1, keepdims=True)
        k_ref[:, sl] = (kh * lax.rsqrt(msk + NORM_EPS) * kg_ref[...]).astype(BF16)
    v_ref[...] = kv[:, W:].astype(BF16)


def _memkv(mem2, mem_g, w_kv_bf, k_g, *, B, M, W):
    D = mem2.shape[1]
    dh = W // CA_HEADS
    return pl.pallas_call(
        functools.partial(_memkv_body, W=W, dh=dh),
        grid=(B,),
        in_specs=[pl.BlockSpec((M, D), lambda b: (b, 0)),
                  pl.BlockSpec((1, D), lambda b: (0, 0)),
                  pl.BlockSpec((D, 2 * W), lambda b: (0, 0)),
                  pl.BlockSpec((1, dh), lambda b: (0, 0))],
        out_specs=[pl.BlockSpec((M, W), lambda b: (b, 0))] * 2,
        out_shape=[jax.ShapeDtypeStruct((B * M, W), BF16)] * 2,
        compiler_params=_cparams(("arbitrary",)),
        name="mem_kv",
    )(mem2, mem_g, w_kv_bf, k_g)


def _xattn_body(q_ref, k_ref, v_ref, qg_ref, o_ref, *, W, dh):
    scale = dh ** -0.5
    for hd in range(W // dh):
        sl = slice(hd * dh, (hd + 1) * dh)
        qh = q_ref[:, sl].astype(F32)
        ms = jnp.mean(qh * qh, axis=-1, keepdims=True)
        qn = (qh * lax.rsqrt(ms + NORM_EPS) * (qg_ref[...] * scale)).astype(BF16)
        s = lax.dot_general(qn, k_ref[:, sl], (((1,), (1,)), ((), ())), preferred_element_type=F32)
        mx = jnp.max(s, axis=-1, keepdims=True)
        p = jnp.exp(s - mx)
        p = p / jnp.sum(p, axis=-1, keepdims=True)
        o_ref[:, sl] = jnp.dot(p.astype(BF16), v_ref[:, sl],
                               preferred_element_type=F32).astype(o_ref.dtype)


def _xattn(proj, kmem, vmem, q_g, *, B, S, M, W, tm, q_col):
    T = B * S
    n_s = S // tm
    dh = W // CA_HEADS
    return pl.pallas_call(
        functools.partial(_xattn_body, W=W, dh=dh),
        grid=(T // tm,),
        in_specs=[pl.BlockSpec((tm, W), lambda i: (i, q_col)),
                  pl.BlockSpec((M, W), lambda i: (i // n_s, 0)),
                  pl.BlockSpec((M, W), lambda i: (i // n_s, 0)),
                  pl.BlockSpec((1, dh), lambda i: (0, 0))],
        out_specs=pl.BlockSpec((tm, W), lambda i: (i, 0)),
        out_shape=jax.ShapeDtypeStruct((T, W), BF16),
        compiler_params=_cparams(("arbitrary",)),
        name="mem_xattn",
    )(proj, kmem, vmem, q_g)


def _pack_bf16_pairs(x):
    w = x.shape[1] // 2
    xr = x.astype(BF16).astype(F32)
    lo = pltpu.bitcast(xr[:, :w], U32) >> 16
    hi = pltpu.bitcast(xr[:, w:], U32) & jnp.uint32(0xFFFF0000)
    return lo | hi


def _unpack_bf16_pairs(wd):
    lo = pltpu.bitcast(wd << 16, F32)
    hi = pltpu.bitcast(wd & jnp.uint32(0xFFFF0000), F32)
    return lo, hi


def _mix_body(x_ref, yda_ref, yml_ref, yca_ref, gda_ref, gml_ref, gca_ref, wda_ref, wml_ref, wca_ref,
              wout_ref, fg_ref, rw_ref, rb_ref, tri_ref,
              h_ref, xp_ref, route_ref, cnt_ref, carry_ref, *, n_exp):
    i = pl.program_id(0)

    @pl.when(i == 0)
    def _():
        carry_ref[...] = jnp.zeros(carry_ref.shape, F32)

    mix = (gda_ref[...].astype(F32) * jnp.dot(yda_ref[...], wda_ref[...], preferred_element_type=F32)
           + gml_ref[...].astype(F32) * jnp.dot(yml_ref[...], wml_ref[...], preferred_element_type=F32)
           + gca_ref[...].astype(F32) * jnp.dot(yca_ref[...], wca_ref[...], preferred_element_type=F32))
    h1 = x_ref[...] + jnp.dot(mix.astype(BF16), wout_ref[...], preferred_element_type=F32)
    h_ref[...] = h1
    ms = jnp.mean(h1 * h1, axis=-1, keepdims=True)
    xn = h1 * lax.rsqrt(ms + NORM_EPS) * fg_ref[...]
    xp_ref[...] = _pack_bf16_pairs(xn)

    logits = jnp.dot(xn.astype(BF16), rw_ref[...], preferred_element_type=F32) + rb_ref[...]
    tm = logits.shape[0]
    lane = lax.broadcasted_iota(I32, (tm, LANES), 1)
    work = jnp.where(lane < n_exp, logits, NEG_BIG)
    sel = jnp.zeros((tm, LANES), F32)
    vals, idxs = [], []
    for _ in range(TOP_K):
        mx = jnp.max(work, axis=-1, keepdims=True)
        idx = jnp.min(jnp.where(work == mx, lane, LANES), axis=-1, keepdims=True)
        hit = lane == idx
        sel = jnp.where(hit, 1.0, sel)
        work = jnp.where(hit, NEG_BIG, work)
        vals.append(mx)
        idxs.append(idx)
    exps = [jnp.exp(v - vals[0]) for v in vals]
    tot = exps[0] + exps[1] + exps[2] + exps[3]

    cum = jnp.dot(tri_ref[...], sel.astype(BF16), preferred_element_type=F32) + carry_ref[...]
    route = jnp.zeros((tm, LANES), F32)
    for kk in range(TOP_K):
        rank = jnp.sum(jnp.where(lane == idxs[kk], cum, 0.0), axis=-1, keepdims=True)
        route = jnp.where(lane == kk, idxs[kk].astype(F32), route)
        route = jnp.where(lane == TOP_K + kk, exps[kk] / tot, route)
        route = jnp.where(lane == 2 * TOP_K + kk, rank, route)
    route_ref[...] = route
    carry_ref[...] = carry_ref[...] + jnp.sum(sel, axis=0, keepdims=True)
    cnt_ref[...] = carry_ref[...]


def _mix(x2, yda, yml, yca, proj, wda, wml, wca, wout, fg, rw, rb, tri, *, tm, g_col, n_exp):
    T, D = x2.shape
    C = yml.shape[1]
    const = lambda shape: pl.BlockSpec(shape, lambda i: (0,) * len(shape))
    return pl.pallas_call(
        functools.partial(_mix_body, n_exp=n_exp),
        grid=(T // tm,),
        in_specs=[pl.BlockSpec((tm, D), lambda i: (i, 0)),
                  pl.BlockSpec((tm, D), lambda i: (i, 0)),
                  pl.BlockSpec((tm, C), lambda i: (i, 0)),
                  pl.BlockSpec((tm, D), lambda i: (i, 0)),
                  pl.BlockSpec((tm, D), lambda i: (i, g_col)),
                  pl.BlockSpec((tm, D), lambda i: (i, g_col + 1)),
                  pl.BlockSpec((tm, D), lambda i: (i, g_col + 2)),
                  const((D, D)), const((C, D)), const((D, D)), const((D, D)),
                  const((1, D)), const((D, LANES)), const((1, LANES)), const((tm, tm))],
        out_specs=[pl.BlockSpec((tm, D), lambda i: (i, 0)),
                   pl.BlockSpec((tm, D // 2), lambda i: (i, 0)),
                   pl.BlockSpec((tm, LANES), lambda i: (i, 0)),
                   pl.BlockSpec((1, LANES), lambda i: (0, 0))],
        out_shape=[jax.ShapeDtypeStruct((T, D), F32),
                   jax.ShapeDtypeStruct((T, D // 2), U32),
                   jax.ShapeDtypeStruct((T, LANES), F32),
                   jax.ShapeDtypeStruct((1, LANES), F32)],
        scratch_shapes=[pltpu.VMEM((1, LANES), F32)],
        compiler_params=_cparams(("arbitrary",)),
        name="mix_route",
    )(x2, yda, yml, yca, proj, proj, proj, wda, wml, wca, wout, fg, rw, rb, tri)


def _group_starts(cnt, tg, n_exp):
    lane_r = lax.broadcasted_iota(I32, (LANES, LANES), 0)
    lane_c = lax.broadcasted_iota(I32, (LANES, LANES), 1)
    padded = jnp.ceil(cnt * (1.0 / tg)) * tg
    padded_col = jnp.sum(jnp.where(lane_r == lane_c, padded, 0.0), axis=1, keepdims=True)
    start = jnp.sum(jnp.where(lane_r < lane_c, padded_col, 0.0), axis=0, keepdims=True)
    return start, start + padded


def _scatter_body(route_ref, cnt_ref, xp_hbm, xs_in_hbm, pos_ref, te_ref, nu_ref, xs_hbm,
                  posv_ref, pos_smem, sem_p, sem_d, *, tm, tg, n_exp, n_tiles):
    del xs_in_hbm
    i = pl.program_id(0)
    route = route_ref[...]
    lane = lax.broadcasted_iota(I32, (tm, LANES), 1)
    start, end = _group_starts(cnt_ref[...], tg, n_exp)

    posm = jnp.zeros((tm, LANES), F32)
    for kk in range(TOP_K):
        e = jnp.sum(jnp.where(lane == kk, route, 0.0), axis=-1, keepdims=True)
        rank = jnp.sum(jnp.where(lane == 2 * TOP_K + kk, route, 0.0), axis=-1, keepdims=True)
        st = jnp.sum(jnp.where(lane == e.astype(I32), start, 0.0), axis=-1, keepdims=True)
        posm = jnp.where(lane == kk, st + rank, posm)
    post = posm.astype(I32).T
    posv_ref[...] = post[:SUBLANES]
    pos_ref[...] = post[:SUBLANES]

    tstart = (lax.broadcasted_iota(I32, (n_tiles, LANES), 0) * tg).astype(F32)
    lane_t = lax.broadcasted_iota(I32, (n_tiles, LANES), 1)
    done = jnp.where((lane_t < n_exp) & (end <= tstart), 1.0, 0.0)
    te = jnp.sum(done, axis=-1, keepdims=True)
    te_ref[...] = jnp.broadcast_to(te, (n_tiles, LANES)).astype(I32)
    n_used = jnp.max(end, axis=-1, keepdims=True) * (1.0 / tg)
    nu_ref[...] = jnp.broadcast_to(n_used, (1, LANES)).astype(I32)

    cp = pltpu.make_async_copy(posv_ref, pos_smem, sem_p)
    cp.start()
    cp.wait()

    base = i * tm

    def row_copy(t, kk):
        return pltpu.make_async_copy(xp_hbm.at[pl.ds(base + t, 1)],
                                     xs_hbm.at[pl.ds(pos_smem[kk, t], 1)], sem_d)

    def issue(t, carry):
        for kk in range(TOP_K):
            row_copy(t, kk).start()
        return carry

    lax.fori_loop(0, tm, issue, 0)

    def drain(t, carry):
        for kk in range(TOP_K):
            row_copy(t, kk).wait()
        return carry

    lax.fori_loop(0, tm, drain, 0)


def _scatter(route, cnt, xp, xs_init, *, tm, tg, n_exp, n_tiles):
    T = route.shape[0]
    n_rows, W = xs_init.shape
    return pl.pallas_call(
        functools.partial(_scatter_body, tm=tm, tg=tg, n_exp=n_exp, n_tiles=n_tiles),
        grid=(T // tm,),
        in_specs=[pl.BlockSpec((tm, LANES), lambda i: (i, 0)),
                  pl.BlockSpec((1, LANES), lambda i: (0, 0)),
                  pl.BlockSpec(memory_space=pl.ANY),
                  pl.BlockSpec(memory_space=pl.ANY)],
        out_specs=[pl.BlockSpec((SUBLANES, tm), lambda i: (0, i)),
                   pl.BlockSpec((n_tiles, LANES), lambda i: (0, 0)),
                   pl.BlockSpec((1, LANES), lambda i: (0, 0)),
                   pl.BlockSpec(memory_space=pl.ANY)],
        out_shape=[jax.ShapeDtypeStruct((SUBLANES, T), I32),
                   jax.ShapeDtypeStruct((n_tiles, LANES), I32),
                   jax.ShapeDtypeStruct((1, LANES), I32),
                   jax.ShapeDtypeStruct((n_rows, W), U32)],
        scratch_shapes=[pltpu.VMEM((SUBLANES, tm), I32), pltpu.SMEM((SUBLANES, tm), I32),
                        pltpu.SemaphoreType.DMA, pltpu.SemaphoreType.DMA],
        input_output_aliases={3: 3},
        compiler_params=_cparams(("arbitrary",)),
        name="moe_scatter",
    )(route, cnt, xp, xs_init)


def _experts_body(te_ref, nu_ref, xs_ref, wgu_ref, bgu_ref, wd_ref, bd_ref, y_ref, *, F):
    i = pl.program_id(0)

    @pl.when(i < nu_ref[0])
    def _():
        lo, hi = _unpack_bf16_pairs(xs_ref[...])
        half = lo.shape[1]
        h = (jnp.dot(lo.astype(BF16), wgu_ref[:half, :], preferred_element_type=F32)
             + jnp.dot(hi.astype(BF16), wgu_ref[half:, :], preferred_element_type=F32)
             + bgu_ref[...])
        gate = jnp.minimum(h[:, :F], SWIGLU_LIMIT)
        up = jnp.clip(h[:, F:], -SWIGLU_LIMIT, SWIGLU_LIMIT)
        a = (up + 1.0) * (gate * jax.nn.sigmoid(SWIGLU_ALPHA * gate))
        y = jnp.dot(a.astype(BF16), wd_ref[...], preferred_element_type=F32) + bd_ref[...]
        y_ref[...] = _pack_bf16_pairs(y)


def _experts(te, n_used, xs, wgu, bgu, wd, bd, *, tg, n_tiles):
    n_rows, W = xs.shape
    E, D, F2 = wgu.shape
    F = F2 // 2
    row = lambda i, te, nu: (jnp.minimum(i, nu[0] - 1), 0)
    exp3 = lambda i, te, nu: (te[jnp.minimum(i, nu[0] - 1)], 0, 0)
    grid_spec = pltpu.PrefetchScalarGridSpec(
        num_scalar_prefetch=2,
        grid=(n_tiles,),
        in_specs=[pl.BlockSpec((tg, W), row),
                  pl.BlockSpec((None, D, F2), exp3),
                  pl.BlockSpec((None, 1, F2), exp3),
                  pl.BlockSpec((None, F, D), exp3),
                  pl.BlockSpec((None, 1, D), exp3)],
        out_specs=pl.BlockSpec((tg, W), row),
    )
    return pl.pallas_call(
        functools.partial(_experts_body, F=F),
        grid_spec=grid_spec,
        out_shape=jax.ShapeDtypeStruct((n_rows, W), U32),
        compiler_params=_cparams(("arbitrary",)),
        name="moe_experts",
    )(te, n_used, xs, wgu, bgu, wd, bd)


def _combine_body(pos_ref, route_ref, h_ref, y_hbm, o_ref, pos_smem, buf_ref, sem_p, sem_d, *, tm):
    cp = pltpu.make_async_copy(pos_ref, pos_smem, sem_p)
    cp.start()
    cp.wait()

    def row_copy(t, kk):
        return pltpu.make_async_copy(y_hbm.at[pl.ds(pos_smem[kk, t], 1)],
                                     buf_ref.at[kk, pl.ds(t, 1)], sem_d)

    def issue(t, carry):
        for kk in range(TOP_K):
            row_copy(t, kk).start()
        return carry

    lax.fori_loop(0, tm, issue, 0)

    def drain(t, carry):
        for kk in range(TOP_K):
            row_copy(t, kk).wait()
        return carry

    lax.fori_loop(0, tm, drain, 0)

    route = route_ref[...]
    lane = lax.broadcasted_iota(I32, (tm, LANES), 1)
    half = buf_ref.shape[2]
    acc_lo = h_ref[:, :half]
    acc_hi = h_ref[:, half:]
    for kk in range(TOP_K):
        w = jnp.sum(jnp.where(lane == TOP_K + kk, route, 0.0), axis=-1, keepdims=True)
        lo, hi = _unpack_bf16_pairs(buf_ref[kk])
        acc_lo = acc_lo + w * lo
        acc_hi = acc_hi + w * hi
    o_ref[:, :half] = acc_lo
    o_ref[:, half:] = acc_hi


def _combine(pos, route, h1, y, *, tm):
    T, D = h1.shape
    W = y.shape[1]
    return pl.pallas_call(
        functools.partial(_combine_body, tm=tm),
        grid=(T // tm,),
        in_specs=[pl.BlockSpec((SUBLANES, tm), lambda i: (0, i)),
                  pl.BlockSpec((tm, LANES), lambda i: (i, 0)),
                  pl.BlockSpec((tm, D), lambda i: (i, 0)),
                  pl.BlockSpec(memory_space=pl.ANY)],
        out_specs=pl.BlockSpec((tm, D), lambda i: (i, 0)),
        out_shape=jax.ShapeDtypeStruct((T, D), F32),
        scratch_shapes=[pltpu.SMEM((SUBLANES, tm), I32), pltpu.VMEM((TOP_K, tm, W), U32),
                        pltpu.SemaphoreType.DMA, pltpu.SemaphoreType.DMA],
        compiler_params=_cparams(("arbitrary",)),
        name="moe_combine",
    )(pos, route, h1, y)


def _blockdiag_dense(w, width):
    nb, bs, _ = w.shape
    per = width // bs
    wt = w.reshape(nb // per, per, bs, bs)
    eye = jnp.eye(per, dtype=w.dtype)
    dense = jnp.einsum('gpio,pq->gpiqo', wt, eye)
    return dense.reshape(nb // per, width, width)


def _rope_tables(S):
    half = DA_HEAD_DIM // 2
    inv = ROPE_THETA ** (-(jnp.arange(half, dtype=F32) * 2.0 / DA_HEAD_DIM))
    ang = jnp.arange(S, dtype=F32)[:, None] * inv[None, :]
    cos = jnp.tile(jnp.cos(ang), (1, LANES // half))
    sign = jnp.where((jnp.arange(LANES) % DA_HEAD_DIM) < half, -1.0, 1.0).astype(F32)
    sin = jnp.tile(jnp.sin(ang), (1, LANES // half)) * sign[None, :]
    return cos, sin


def _tile(n, pref):
    return pref if n % pref == 0 else n


def _layer(h2, mem2, B, S, lambda_init, attn_norm_g, w_in, b_gate, da_q_norm_g, da_k_norm_g,
           da_lambda_q1, da_lambda_k1, da_lambda_q2, da_lambda_k2, da_subln_g, ml_conv_w, ml_conv_b,
           ml_wq, ml_wk, ml_wv, ml_w_if, ml_b_if, ml_out_norm_g, ml_skip, mem_norm_g, ca_w_kv,
           ca_q_norm_g, ca_k_norm_g, w_branch_da, w_branch_ml, w_branch_ca, w_out, ffn_norm_g,
           router_w, router_b, w_gate_up, b_gate_up, w_down, b_down):
    T, D = h2.shape
    M = mem2.shape[0] // B
    QK = DA_HEADS * 2 * DA_HEAD_DIM
    VW = DA_HEADS * DA_V_DIM
    C = ml_conv_w.shape[1]
    CAW = ca_w_kv.shape[1] // 2
    E = router_w.shape[1]

    o = [0, QK, 2 * QK, 2 * QK + VW, 2 * QK + VW + C, 2 * QK + VW + 2 * C, 2 * QK + VW + 2 * C + CAW]
    w_re = jnp.concatenate([w_in[:, o[0]:o[3]], w_in[:, o[5]:o[6]], w_in[:, o[3]:o[5]], w_in[:, o[6]:]],
                           axis=1).astype(BF16)
    tn = 1024
    col_v, col_caq, col_mlx, col_mlz, col_gate = 2 * QK, 2 * QK + VW, 2 * QK + VW + CAW, \
        2 * QK + VW + CAW + C, 2 * QK + VW + CAW + 2 * C
    proj = _inproj(h2, attn_norm_g[None, :], w_re, b_gate[None, :], tm=_tile(T, 1024), tn=tn,
                   n_plain=col_gate // tn)

    g2 = jnp.stack([jnp.tile(da_q_norm_g, LANES // DA_HEAD_DIM),
                    jnp.tile(da_k_norm_g, LANES // DA_HEAD_DIM)])[:, None, :]
    cos_t, sin_t = _rope_tables(S)
    lane_grp = jnp.arange(LANES) // DA_HEAD_DIM
    gm = (lane_grp[:, None] == lane_grp[None, :]).astype(BF16)
    qk = _qkrope(proj, g2, cos_t, sin_t, gm, T=T, S=S, width=QK, tm=_tile(S, 1024))
    y_da = _diff_attention(qk, proj, da_lambda_q1[None, :], da_lambda_k1[None, :], da_lambda_q2[None, :],
                           da_lambda_k2[None, :], da_subln_g[None, :], B=B, S=S, tq=_tile(S, 256),
                           v_col0=col_v // LANES, lambda_init=lambda_init)

    bd = jnp.stack([_blockdiag_dense(ml_wq, MXU_DIM), _blockdiag_dense(ml_wk, MXU_DIM),
                    _blockdiag_dense(ml_wv, MXU_DIM)]).astype(BF16)
    wif = jnp.pad(ml_w_if.reshape(3, C, 2 * ML_HEADS), ((0, 0), (0, 0), (0, LANES - 2 * ML_HEADS))).astype(BF16)
    bif = jnp.pad(ml_b_if, (0, LANES - 2 * ML_HEADS))[None, :]
    xc, mq, mk, mv, gates = _mlpre(proj, ml_conv_w, ml_conv_b[None, :], bd, wif, bif, B=B, S=S, C=C,
                                   tm=_tile(S, 512), x_col=col_mlx // C)
    dml = C // ML_HEADS
    y_ml = _mlstm(mq, mk, mv, gates, xc, proj, ml_out_norm_g[None, :], ml_skip[None, :], B=B, S=S, C=C,
                  L=_tile(S, 256), z_col0=col_mlz // dml)

    kmem, vmem = _memkv(mem2, mem_norm_g[None, :], ca_w_kv.astype(BF16), ca_k_norm_g[None, :], B=B, M=M, W=CAW)
    y_ca = _xattn(proj, kmem, vmem, ca_q_norm_g[None, :], B=B, S=S, M=M, W=CAW, tm=_tile(S, 512),
                  q_col=col_caq // CAW)

    tmx = _tile(T, 512)
    tri = (jnp.arange(tmx)[None, :] < jnp.arange(tmx)[:, None]).astype(BF16)
    rw = jnp.pad(router_w, ((0, 0), (0, LANES - E))).astype(BF16)
    rb = jnp.pad(router_b, (0, LANES - E))[None, :]
    h1, xp, route, cnt = _mix(h2, y_da, y_ml, y_ca, proj, w_branch_da.astype(BF16), w_branch_ml.astype(BF16),
                              w_branch_ca.astype(BF16), w_out.astype(BF16), ffn_norm_g[None, :], rw, rb, tri,
                              tm=tmx, g_col=col_gate // D, n_exp=E)

    tg = _tile(T, 512)
    n_tiles = (T * TOP_K) // tg + E
    n_rows = n_tiles * tg
    xs0 = jnp.zeros((n_rows, D // 2), U32)
    pos, te, n_used, xs = _scatter(route, cnt, xp, xs0, tm=_tile(T, 512), tg=tg, n_exp=E, n_tiles=n_tiles)
    y = _experts(te[:, 0], n_used[0, :1], xs, w_gate_up.astype(BF16), b_gate_up[:, None, :],
                 w_down.astype(BF16), b_down[:, None, :], tg=tg, n_tiles=n_tiles)
    return _combine(pos, route, h1, y, tm=_tile(T, 256))


def kernel(x, mem, attn_norm_g, w_in, b_gate, da_q_norm_g, da_k_norm_g, da_lambda_q1, da_lambda_k1, da_lambda_q2, da_lambda_k2, da_subln_g, ml_conv_w, ml_conv_b, ml_wq, ml_wk, ml_wv, ml_w_if, ml_b_if, ml_out_norm_g, ml_skip, mem_norm_g, ca_w_kv, ca_q_norm_g, ca_k_norm_g, w_branch_da, w_branch_ml, w_branch_ca, w_out, ffn_norm_g, router_w, router_b, w_gate_up, b_gate_up, w_down, b_down):
    B, S, D = x.shape
    depth = w_in.shape[0]
    h2 = x.reshape(B * S, D)
    mem2 = mem.reshape(B * mem.shape[1], D)
    params = (attn_norm_g, w_in, b_gate, da_q_norm_g, da_k_norm_g, da_lambda_q1, da_lambda_k1,
              da_lambda_q2, da_lambda_k2, da_subln_g, ml_conv_w, ml_conv_b, ml_wq, ml_wk, ml_wv, ml_w_if,
              ml_b_if, ml_out_norm_g, ml_skip, mem_norm_g, ca_w_kv, ca_q_norm_g, ca_k_norm_g, w_branch_da,
              w_branch_ml, w_branch_ca, w_out, ffn_norm_g, router_w, router_b, w_gate_up, b_gate_up,
              w_down, b_down)
    for l in range(depth):
        lambda_init = 0.8 - 0.6 * math.exp(-0.3 * l)
        h2 = _layer(h2, mem2, B, S, lambda_init, *[p[l] for p in params])
    return h2.reshape(B, S, D)
```

```python
import functools
import math

import jax
import jax.numpy as jnp
from jax import lax
from jax.experimental import pallas as pl
from jax.experimental.pallas import tpu as pltpu

F32 = jnp.float32
BF16 = jnp.bfloat16
I32 = jnp.int32
U32 = jnp.uint32

NORM_EPS = 1e-6
ROPE_THETA = 10000.0
CHUNK = 64

DA_HEADS = 8
DA_HEAD_DIM = 64
DA_V_DIM = 128
ML_HEADS = 4
ML_CONV = 4
ML_QKV_BLOCK = 4
CA_HEADS = 4
N_BRANCH = 3
TOP_K = 4
SWIGLU_LIMIT = 7.0
SWIGLU_ALPHA = 1.702

LANES = 128
SUBLANES = 8
MXU_DIM = 256
VMEM_LIMIT = 56 * 1024 * 1024
NEG_BIG = -1e30


def _cparams(sem):
    return pltpu.CompilerParams(dimension_semantics=sem, vmem_limit_bytes=VMEM_LIMIT)


def _silu(x):
    return x * jax.nn.sigmoid(x)


def _inproj_body(x_ref, g_ref, w_ref, b_ref, o_ref, xn_ref, *, n_plain):
    j = pl.program_id(1)

    @pl.when(j == 0)
    def _():
        x = x_ref[...]
        ms = jnp.mean(x * x, axis=-1, keepdims=True)
        xn_ref[...] = (x * lax.rsqrt(ms + NORM_EPS) * g_ref[...]).astype(BF16)

    acc = jnp.dot(xn_ref[...], w_ref[...], preferred_element_type=F32)

    @pl.when(j < n_plain)
    def _():
        o_ref[...] = acc.astype(o_ref.dtype)

    @pl.when(j >= n_plain)
    def _():
        o_ref[...] = jax.nn.sigmoid(acc + b_ref[...]).astype(o_ref.dtype)


def _inproj(x2, g, w_bf, b_gate, *, tm, tn, n_plain):
    T, D = x2.shape
    N = w_bf.shape[1]
    return pl.pallas_call(
        functools.partial(_inproj_body, n_plain=n_plain),
        grid=(T // tm, N // tn),
        in_specs=[
            pl.BlockSpec((tm, D), lambda i, j: (i, 0)),
            pl.BlockSpec((1, D), lambda i, j: (0, 0)),
            pl.BlockSpec((D, tn), lambda i, j: (0, j)),
            pl.BlockSpec((1, tn), lambda i, j: (0, jnp.maximum(j - n_plain, 0))),
        ],
        out_specs=pl.BlockSpec((tm, tn), lambda i, j: (i, j)),
        out_shape=jax.ShapeDtypeStruct((T, N), BF16),
        scratch_shapes=[pltpu.VMEM((tm, D), BF16)],
        compiler_params=_cparams(("arbitrary", "arbitrary")),
        name="inproj",
    )(x2, g, w_bf, b_gate)


def _group_sumsq(xb, gm):
    sq = xb * xb
    hi = sq.astype(BF16)
    lo = (sq - hi.astype(F32)).astype(BF16)
    return (jnp.dot(hi, gm, preferred_element_type=F32)
            + jnp.dot(lo, gm, preferred_element_type=F32))


def _qkrope_body(x_ref, g_ref, cos_ref, sin_ref, gm_ref, o_ref, *, ncb, scale):
    c = pl.program_id(1)
    g = g_ref[...] * jnp.where(c == 0, scale, 1.0).astype(F32)
    cos = cos_ref[...]
    sin = sin_ref[...]
    gm = gm_ref[...]
    lane = lax.broadcasted_iota(I32, (1, LANES), 1)
    first_half = (lane & (DA_HEAD_DIM - 1)) < (DA_HEAD_DIM // 2)
    for cb in range(ncb):
        sl = slice(cb * LANES, (cb + 1) * LANES)
        xb = x_ref[:, sl].astype(F32)
        ss = _group_sumsq(xb, gm)
        y = xb * lax.rsqrt(ss * (1.0 / DA_HEAD_DIM) + NORM_EPS) * g
        sw = jnp.where(first_half, pltpu.roll(y, LANES - DA_HEAD_DIM // 2, 1),
                       pltpu.roll(y, DA_HEAD_DIM // 2, 1))
        o_ref[:, sl] = (y * cos + sw * sin).astype(o_ref.dtype)


def _qkrope(proj, g2, cos_t, sin_t, gm, *, T, S, width, tm):
    n_s = S // tm
    return pl.pallas_call(
        functools.partial(_qkrope_body, ncb=width // LANES, scale=DA_HEAD_DIM ** -0.5 * math.log2(math.e)),
        grid=(T // tm, 2),
        in_specs=[
            pl.BlockSpec((tm, width), lambda i, c: (i, c)),
            pl.BlockSpec((None, 1, LANES), lambda i, c: (c, 0, 0)),
            pl.BlockSpec((tm, LANES), lambda i, c: (i % n_s, 0)),
            pl.BlockSpec((tm, LANES), lambda i, c: (i % n_s, 0)),
            pl.BlockSpec((LANES, LANES), lambda i, c: (0, 0)),
        ],
        out_specs=pl.BlockSpec((tm, width), lambda i, c: (i, c)),
        out_shape=jax.ShapeDtypeStruct((T, 2 * width), BF16),
        compiler_params=_cparams(("arbitrary", "arbitrary")),
        name="qkrope",
    )(proj, g2, cos_t, sin_t, gm)


def _da_body(q_ref, k_ref, v_ref, lq1_ref, lk1_ref, lq2_ref, lk2_ref, subg_ref, o_ref,
             m_ref, acc_ref, *, tq, lambda_init):
    i = pl.program_id(2)
    q = q_ref[...]
    lane = lax.broadcasted_iota(I32, (1, LANES), 1)
    lo = lane < DA_HEAD_DIM
    zero = jnp.zeros_like(q)
    qc = (jnp.where(lo, q, zero), jnp.where(lo, zero, q))

    m_ref[...] = jnp.full(m_ref.shape, NEG_BIG, F32)
    acc_ref[...] = jnp.zeros(acc_ref.shape, F32)
    ones = jnp.ones((tq, LANES), BF16)

    def scores(j):
        kj = k_ref[pl.ds(pl.multiple_of(j * tq, tq), tq), :]
        return [lax.dot_general(qc[c], kj, (((1,), (1,)), ((), ())), preferred_element_type=F32)
                for c in range(2)]

    def consume(j, ss, masked):
        vj = jnp.concatenate([v_ref[pl.ds(pl.multiple_of(j * tq, tq), tq), :], ones], axis=1)
        if masked:
            r = lax.broadcasted_iota(I32, (tq, tq), 0)
            cidx = lax.broadcasted_iota(I32, (tq, tq), 1)
            allowed = (cidx | (CHUNK - 1)) <= (r | (CHUNK - 1))
        ps, alphas = [], []
        for c in range(2):
            s = jnp.where(allowed, ss[c], NEG_BIG) if masked else ss[c]
            m_old = m_ref[c]
            m_new = jnp.maximum(m_old, jnp.max(s, axis=-1, keepdims=True))
            ps.append(jnp.exp2(s - jnp.concatenate([m_new] * (tq // LANES), axis=1)).astype(BF16))
            alphas.append(jnp.exp2(m_old - m_new))
            m_ref[c] = m_new
        pvs = [jnp.dot(ps[c], vj, preferred_element_type=F32) for c in range(2)]
        for c in range(2):
            acc_ref[c] = jnp.concatenate([alphas[c], alphas[c]], axis=1) * acc_ref[c] + pvs[c]

    def pair(jj, carry):
        sa = scores(2 * jj)
        sb = scores(2 * jj + 1)
        consume(2 * jj, sa, False)
        consume(2 * jj + 1, sb, False)
        return carry

    lax.fori_loop(0, lax.shift_right_logical(i, 1), pair, 0)

    @pl.when((i & 1) == 1)
    def _():
        consume(i - 1, scores(i - 1), False)

    consume(i, scores(i), True)

    f32 = F32
    lam = (jnp.exp(jnp.sum(lq1_ref[...].astype(f32) * lk1_ref[...].astype(f32), keepdims=True))
           - jnp.exp(jnp.sum(lq2_ref[...].astype(f32) * lk2_ref[...].astype(f32), keepdims=True))
           + lambda_init)
    a1 = acc_ref[0]
    a2 = acc_ref[1]
    o = a1[:, :DA_V_DIM] / a1[:, DA_V_DIM:] - lam * (a2[:, :DA_V_DIM] / a2[:, DA_V_DIM:])
    ms = jnp.mean(o * o, axis=-1, keepdims=True)
    o = o * lax.rsqrt(ms + NORM_EPS) * subg_ref[...]
    o_ref[...] = (o * (1.0 - lambda_init)).astype(o_ref.dtype)


def _diff_attention(qk, proj, lq1, lk1, lq2, lk2, subg, *, B, S, tq, v_col0, lambda_init):
    T = B * S
    H = DA_HEADS
    n_q = S // tq
    vec = lambda n: pl.BlockSpec((1, n), lambda b, h, i: (0, 0))
    return pl.pallas_call(
        functools.partial(_da_body, tq=tq, lambda_init=lambda_init),
        grid=(B, H, n_q),
        in_specs=[
            pl.BlockSpec((tq, LANES), lambda b, h, i: (b * n_q + i, h)),
            pl.BlockSpec((S, LANES), lambda b, h, i: (b, H + h)),
            pl.BlockSpec((S, LANES), lambda b, h, i: (b, v_col0 + h)),
            vec(DA_HEAD_DIM), vec(DA_HEAD_DIM), vec(DA_HEAD_DIM), vec(DA_HEAD_DIM),
            vec(DA_V_DIM),
        ],
        out_specs=pl.BlockSpec((tq, LANES), lambda b, h, i: (b * n_q + i, h)),
        out_shape=jax.ShapeDtypeStruct((T, H * DA_V_DIM), BF16),
        scratch_shapes=[pltpu.VMEM((2, tq, LANES), F32), pltpu.VMEM((2, tq, 2 * DA_V_DIM), F32)],
        compiler_params=_cparams(("arbitrary", "arbitrary", "arbitrary")),
        name="diff_attn",
    )(qk, qk, proj, lq1, lk1, lq2, lk2, subg)


def _mlpre_body(x_ref, cw_ref, cb_ref, bd_ref, wif_ref, bif_ref,
                xc_ref, q_ref, k_ref, v_ref, g_ref, prev_ref, *, tm, ncb, kscale):
    s_idx = pl.program_id(1)

    @pl.when(s_idx == 0)
    def _():
        prev_ref[...] = jnp.zeros(prev_ref.shape, F32)

    row8 = lax.broadcasted_iota(I32, (SUBLANES, MXU_DIM), 0)
    gacc = jnp.zeros((tm, LANES), F32) + bif_ref[...]
    for cb in range(ncb):
        sl = slice(cb * MXU_DIM, (cb + 1) * MXU_DIM)
        xb16 = x_ref[:, sl]
        x = xb16.astype(F32)
        prev = prev_ref[:, sl]
        conv = x * cw_ref[ML_CONV - 1:ML_CONV, sl] + cb_ref[:, sl]
        for d in range(1, ML_CONV):
            xs = pltpu.roll(x, d, 0)
            ps = pltpu.roll(prev, d, 0)
            head = jnp.where(row8 < d, ps, xs[:SUBLANES])
            shifted = jnp.concatenate([head, xs[SUBLANES:]], axis=0)
            conv = conv + shifted * cw_ref[ML_CONV - 1 - d:ML_CONV - d, sl]
        prev_ref[:, sl] = x[tm - SUBLANES:]
        xc = _silu(conv)
        xc16 = xc.astype(BF16)
        q = jnp.dot(xc16, bd_ref[0, cb], preferred_element_type=F32)
        k = jnp.dot(xc16, bd_ref[1, cb], preferred_element_type=F32)
        v = jnp.dot(xb16, bd_ref[2, cb], preferred_element_type=F32)
        q16, k16, v16 = q.astype(BF16), k.astype(BF16), v.astype(BF16)
        gacc = gacc + jnp.dot(q16, wif_ref[0, sl, :], preferred_element_type=F32)
        gacc = gacc + jnp.dot(k16, wif_ref[1, sl, :], preferred_element_type=F32)
        gacc = gacc + jnp.dot(v16, wif_ref[2, sl, :], preferred_element_type=F32)
        xc_ref[:, sl] = xc16
        q_ref[:, sl] = q16
        k_ref[:, sl] = (k * kscale).astype(BF16)
        v_ref[:, sl] = v16
    g_ref[...] = gacc


def _mlpre(proj, conv_w, conv_b, bd, wif, bif, *, B, S, C, tm, x_col):
    T = B * S
    n_s = S // tm
    row = lambda i: pl.BlockSpec((tm, C), lambda b, s: (b * n_s + s, i))
    outs = pl.pallas_call(
        functools.partial(_mlpre_body, tm=tm, ncb=C // MXU_DIM, kscale=(C // ML_HEADS) ** -0.5),
        grid=(B, n_s),
        in_specs=[
            row(x_col),
            pl.BlockSpec((ML_CONV, C), lambda b, s: (0, 0)),
            pl.BlockSpec((1, C), lambda b, s: (0, 0)),
            pl.BlockSpec(bd.shape, lambda b, s: (0, 0, 0, 0)),
            pl.BlockSpec(wif.shape, lambda b, s: (0, 0, 0)),
            pl.BlockSpec((1, LANES), lambda b, s: (0, 0)),
        ],
        out_specs=[row(0), row(0), row(0), row(0),
                   pl.BlockSpec((tm, LANES), lambda b, s: (b * n_s + s, 0))],
        out_shape=[jax.ShapeDtypeStruct((T, C), BF16)] * 4 + [jax.ShapeDtypeStruct((T, LANES), F32)],
        scratch_shapes=[pltpu.VMEM((SUBLANES, C), F32)],
        compiler_params=_cparams(("arbitrary", "arbitrary")),
        name="mlstm_pre",
    )(proj, conv_w, conv_b, bd, wif, bif)
    return outs


def _mlstm_body(q_ref, k_ref, v_ref, g_ref, xc_ref, z_ref, ng_ref, skip_ref, o_ref,
                c_ref, m_ref, *, L, d):
    h = pl.program_id(1)
    c_idx = pl.program_id(2)

    @pl.when(c_idx == 0)
    def _():
        c_ref[...] = jnp.zeros(c_ref.shape, F32)
        m_ref[...] = jnp.zeros(m_ref.shape, F32)

    g = g_ref[...]
    gt = g.T
    lane = lax.broadcasted_iota(I32, (1, LANES), 1)
    sub = lax.broadcasted_iota(I32, (LANES, 1), 0)
    i_col = jnp.sum(jnp.where(lane == h, g, 0.0), axis=1, keepdims=True)
    f_col = jnp.sum(jnp.where(lane == ML_HEADS + h, g, 0.0), axis=1, keepdims=True)
    i_row = jnp.sum(jnp.where(sub == h, gt, 0.0), axis=0, keepdims=True)
    f_row = jnp.sum(jnp.where(sub == ML_HEADS + h, gt, 0.0), axis=0, keepdims=True)
    lf_col = jax.nn.log_sigmoid(f_col)
    lf_row = jax.nn.log_sigmoid(f_row)

    t_idx = lax.broadcasted_iota(I32, (L, L), 0)
    s_idx = lax.broadcasted_iota(I32, (L, L), 1)
    tri = s_idx <= t_idx
    b_col = jnp.sum(jnp.where(tri, lf_row, 0.0), axis=1, keepdims=True)
    b_row = jnp.sum(jnp.where(t_idx <= s_idx, lf_col, 0.0), axis=0, keepdims=True)
    b_last = jnp.sum(lf_row, axis=1, keepdims=True)

    m_prev = m_ref[...]
    dlog = jnp.where(tri, b_col - b_row + i_row, NEG_BIG)
    inter_log = b_col + m_prev
    m_rowv = jnp.maximum(inter_log, jnp.max(dlog, axis=1, keepdims=True))
    dw = jnp.exp(dlog - m_rowv)
    inter_w = jnp.exp(inter_log - m_rowv)

    q = q_ref[...]
    k = k_ref[...]
    ones_col = (lax.broadcasted_iota(I32, (L, LANES), 1) == 0).astype(BF16)
    v_aug = jnp.concatenate([v_ref[...], ones_col], axis=1)
    s = lax.dot_general(q, k, (((1,), (1,)), ((), ())), preferred_element_type=F32) * dw
    num_aug = (jnp.dot(s.astype(BF16), v_aug, preferred_element_type=F32)
               + inter_w * jnp.dot(q, c_ref[...].astype(BF16), preferred_element_type=F32))
    num = num_aug[:, :d]
    den = jnp.sum(jnp.where(lane == 0, num_aug[:, d:], 0.0), axis=1, keepdims=True)
    hh = num / jnp.maximum(jnp.abs(den), jnp.exp(-m_rowv))

    state_row = b_last - b_row + i_row
    state_col = b_last - b_col + i_col
    m_new = jnp.maximum(b_last + m_prev, jnp.max(state_row, axis=1, keepdims=True))
    decay = jnp.exp(b_last + m_prev - m_new)
    ws_col = jnp.exp(state_col - m_new)
    kw = (k.astype(F32) * ws_col).astype(BF16)
    c_ref[...] = decay * c_ref[...] + lax.dot_general(
        kw, v_aug, (((0,), (0,)), ((), ())), preferred_element_type=F32)
    m_ref[...] = m_new

    mu = jnp.mean(hh, axis=-1, keepdims=True)
    cen = hh - mu
    var = jnp.mean(cen * cen, axis=-1, keepdims=True)
    hn = cen * lax.rsqrt(var + NORM_EPS) * ng_ref[...]
    y = (hn + skip_ref[...] * xc_ref[...].astype(F32)) * _silu(z_ref[...].astype(F32))
    o_ref[...] = y.astype(o_ref.dtype)


def _mlstm(q, k, v, gates, xc, proj, norm_g, skip, *, B, S, C, L, z_col0):
    T = B * S
    H = ML_HEADS
    d = C // H
    nc = S // L
    blk = lambda off: pl.BlockSpec((L, d), lambda b, h, c: (b * nc + c, off + h))
    par = pl.BlockSpec((1, d), lambda b, h, c: (0, h))
    return pl.pallas_call(
        functools.partial(_mlstm_body, L=L, d=d),
        grid=(B, H, nc),
        in_specs=[blk(0), blk(0), blk(0),
                  pl.BlockSpec((L, LANES), lambda b, h, c: (b * nc + c, 0)),
                  blk(0), blk(z_col0), par, par],
        out_specs=blk(0),
        out_shape=jax.ShapeDtypeStruct((T, C), BF16),
        scratch_shapes=[pltpu.VMEM((d, d + LANES), F32), pltpu.VMEM((1, 1), F32)],
        compiler_params=_cparams(("arbitrary", "arbitrary", "arbitrary")),
        name="mlstm_scan",
    )(q, k, v, gates, xc, proj, norm_g, skip)


def _memkv_body(mem_ref, mg_ref, w_ref, kg_ref, k_ref, v_ref, *, W, dh):
    x = mem_ref[...]
    ms = jnp.mean(x * x, axis=-1, keepdims=True)
    xn = (x * lax.rsqrt(ms + NORM_EPS) * mg_ref[...]).astype(BF16)
    kv = jnp.dot(xn, w_ref[...], preferred_element_type=F32)
    for hd in range(W // dh):
        sl = slice(hd * dh, (hd + 1) * dh)
        kh = kv[:, sl]
        msk = jnp.mean(kh * kh, axis=-1, keepdims=True)
        k_ref[:, sl] = (kh * lax.rsqrt(msk + NORM_EPS) * kg_ref[...]).astype(BF16)
    v_ref[...] = kv[:, W:].astype(BF16)


def _memkv(mem2, mem_g, w_kv_bf, k_g, *, B, M, W):
    D = mem2.shape[1]
    dh = W // CA_HEADS
    return pl.pallas_call(
        functools.partial(_memkv_body, W=W, dh=dh),
        grid=(B,),
        in_specs=[pl.BlockSpec((M, D), lambda b: (b, 0)),
                  pl.BlockSpec((1, D), lambda b: (0, 0)),
                  pl.BlockSpec((D, 2 * W), lambda b: (0, 0)),
                  pl.BlockSpec((1, dh), lambda b: (0, 0))],
        out_specs=[pl.BlockSpec((M, W), lambda b: (b, 0))] * 2,
        out_shape=[jax.ShapeDtypeStruct((B * M, W), BF16)] * 2,
        compiler_params=_cparams(("arbitrary",)),
        name="mem_kv",
    )(mem2, mem_g, w_kv_bf, k_g)


def _xattn_body(q_ref, k_ref, v_ref, qg_ref, o_ref, *, W, dh):
    scale = dh ** -0.5
    for hd in range(W // dh):
        sl = slice(hd * dh, (hd + 1) * dh)
        qh = q_ref[:, sl].astype(F32)
        ms = jnp.mean(qh * qh, axis=-1, keepdims=True)
        qn = (qh * lax.rsqrt(ms + NORM_EPS) * (qg_ref[...] * scale)).astype(BF16)
        s = lax.dot_general(qn, k_ref[:, sl], (((1,), (1,)), ((), ())), preferred_element_type=F32)
        mx = jnp.max(s, axis=-1, keepdims=True)
        p = jnp.exp(s - mx)
        p = p / jnp.sum(p, axis=-1, keepdims=True)
        o_ref[:, sl] = jnp.dot(p.astype(BF16), v_ref[:, sl],
                               preferred_element_type=F32).astype(o_ref.dtype)


def _xattn(proj, kmem, vmem, q_g, *, B, S, M, W, tm, q_col):
    T = B * S
    n_s = S // tm
    dh = W // CA_HEADS
    return pl.pallas_call(
        functools.partial(_xattn_body, W=W, dh=dh),
        grid=(T // tm,),
        in_specs=[pl.BlockSpec((tm, W), lambda i: (i, q_col)),
                  pl.BlockSpec((M, W), lambda i: (i // n_s, 0)),
                  pl.BlockSpec((M, W), lambda i: (i // n_s, 0)),
                  pl.BlockSpec((1, dh), lambda i: (0, 0))],
        out_specs=pl.BlockSpec((tm, W), lambda i: (i, 0)),
        out_shape=jax.ShapeDtypeStruct((T, W), BF16),
        compiler_params=_cparams(("arbitrary",)),
        name="mem_xattn",
    )(proj, kmem, vmem, q_g)


def _pack_bf16_pairs(x):
    w = x.shape[1] // 2
    xr = x.astype(BF16).astype(F32)
    lo = pltpu.bitcast(xr[:, :w], U32) >> 16
    hi = pltpu.bitcast(xr[:, w:], U32) & jnp.uint32(0xFFFF0000)
    return lo | hi


def _unpack_bf16_pairs(wd):
    lo = pltpu.bitcast(wd << 16, F32)
    hi = pltpu.bitcast(wd & jnp.uint32(0xFFFF0000), F32)
    return lo, hi


def _mix_body(x_ref, yda_ref, yml_ref, yca_ref, gda_ref, gml_ref, gca_ref, wda_ref, wml_ref, wca_ref,
              wout_ref, fg_ref, rw_ref, rb_ref, tri_ref,
              h_ref, xp_ref, route_ref, cnt_ref, carry_ref, *, n_exp):
    i = pl.program_id(0)

    @pl.when(i == 0)
    def _():
        carry_ref[...] = jnp.zeros(carry_ref.shape, F32)

    mix = (gda_ref[...].astype(F32) * jnp.dot(yda_ref[...], wda_ref[...], preferred_element_type=F32)
           + gml_ref[...].astype(F32) * jnp.dot(yml_ref[...], wml_ref[...], preferred_element_type=F32)
           + gca_ref[...].astype(F32) * jnp.dot(yca_ref[...], wca_ref[...], preferred_element_type=F32))
    h1 = x_ref[...] + jnp.dot(mix.astype(BF16), wout_ref[...], preferred_element_type=F32)
    h_ref[...] = h1
    ms = jnp.mean(h1 * h1, axis=-1, keepdims=True)
    xn = h1 * lax.rsqrt(ms + NORM_EPS) * fg_ref[...]
    xp_ref[...] = _pack_bf16_pairs(xn)

    logits = jnp.dot(xn.astype(BF16), rw_ref[...], preferred_element_type=F32) + rb_ref[...]
    tm = logits.shape[0]
    lane = lax.broadcasted_iota(I32, (tm, LANES), 1)
    work = jnp.where(lane < n_exp, logits, NEG_BIG)
    sel = jnp.zeros((tm, LANES), F32)
    vals, idxs = [], []
    for _ in range(TOP_K):
        mx = jnp.max(work, axis=-1, keepdims=True)
        idx = jnp.min(jnp.where(work == mx, lane, LANES), axis=-1, keepdims=True)
        hit = lane == idx
        sel = jnp.where(hit, 1.0, sel)
        work = jnp.where(hit, NEG_BIG, work)
        vals.append(mx)
        idxs.append(idx)
    exps = [jnp.exp(v - vals[0]) for v in vals]
    tot = exps[0] + exps[1] + exps[2] + exps[3]

    cum = jnp.dot(tri_ref[...], sel.astype(BF16), preferred_element_type=F32) + carry_ref[...]
    route = jnp.zeros((tm, LANES), F32)
    for kk in range(TOP_K):
        rank = jnp.sum(jnp.where(lane == idxs[kk], cum, 0.0), axis=-1, keepdims=True)
        route = jnp.where(lane == kk, idxs[kk].astype(F32), route)
        route = jnp.where(lane == TOP_K + kk, exps[kk] / tot, route)
        route = jnp.where(lane == 2 * TOP_K + kk, rank, route)
    route_ref[...] = route
    carry_ref[...] = carry_ref[...] + jnp.sum(sel, axis=0, keepdims=True)
    cnt_ref[...] = carry_ref[...]


def _mix(x2, yda, yml, yca, proj, wda, wml, wca, wout, fg, rw, rb, tri, *, tm, g_col, n_exp):
    T, D = x2.shape
    C = yml.shape[1]
    const = lambda shape: pl.BlockSpec(shape, lambda i: (0,) * len(shape))
    return pl.pallas_call(
        functools.partial(_mix_body, n_exp=n_exp),
        grid=(T // tm,),
        in_specs=[pl.BlockSpec((tm, D), lambda i: (i, 0)),
                  pl.BlockSpec((tm, D), lambda i: (i, 0)),
                  pl.BlockSpec((tm, C), lambda i: (i, 0)),
                  pl.BlockSpec((tm, D), lambda i: (i, 0)),
                  pl.BlockSpec((tm, D), lambda i: (i, g_col)),
                  pl.BlockSpec((tm, D), lambda i: (i, g_col + 1)),
                  pl.BlockSpec((tm, D), lambda i: (i, g_col + 2)),
                  const((D, D)), const((C, D)), const((D, D)), const((D, D)),
                  const((1, D)), const((D, LANES)), const((1, LANES)), const((tm, tm))],
        out_specs=[pl.BlockSpec((tm, D), lambda i: (i, 0)),
                   pl.BlockSpec((tm, D // 2), lambda i: (i, 0)),
                   pl.BlockSpec((tm, LANES), lambda i: (i, 0)),
                   pl.BlockSpec((1, LANES), lambda i: (0, 0))],
        out_shape=[jax.ShapeDtypeStruct((T, D), F32),
                   jax.ShapeDtypeStruct((T, D // 2), U32),
                   jax.ShapeDtypeStruct((T, LANES), F32),
                   jax.ShapeDtypeStruct((1, LANES), F32)],
        scratch_shapes=[pltpu.VMEM((1, LANES), F32)],
        compiler_params=_cparams(("arbitrary",)),
        name="mix_route",
    )(x2, yda, yml, yca, proj, proj, proj, wda, wml, wca, wout, fg, rw, rb, tri)


ROW_DMA_UNROLL = 8


def _issue_and_drain_rows(row_copy, tm):
    def issue(t, carry):
        for kk in range(TOP_K):
            row_copy(t, kk).start(priority=kk % 2)
        return carry

    lax.fori_loop(0, tm, issue, 0, unroll=ROW_DMA_UNROLL)

    def drain(t, carry):
        for kk in range(TOP_K):
            row_copy(t, kk).wait()
        return carry

    lax.fori_loop(0, tm, drain, 0, unroll=ROW_DMA_UNROLL)


def _group_starts(cnt, tg, n_exp):
    lane_r = lax.broadcasted_iota(I32, (LANES, LANES), 0)
    lane_c = lax.broadcasted_iota(I32, (LANES, LANES), 1)
    padded = jnp.ceil(cnt * (1.0 / tg)) * tg
    padded_col = jnp.sum(jnp.where(lane_r == lane_c, padded, 0.0), axis=1, keepdims=True)
    start = jnp.sum(jnp.where(lane_r < lane_c, padded_col, 0.0), axis=0, keepdims=True)
    return start, start + padded


def _scatter_body(route_ref, cnt_ref, xp_ref, xs_in_hbm, pos_ref, te_ref, nu_ref, xs_hbm,
                  posv_ref, pos_smem, sem_p, sem_d, *, tm, tg, n_exp, n_tiles):
    del xs_in_hbm
    route = route_ref[...]
    lane = lax.broadcasted_iota(I32, (tm, LANES), 1)
    start, end = _group_starts(cnt_ref[...], tg, n_exp)

    posm = jnp.zeros((tm, LANES), F32)
    for kk in range(TOP_K):
        e = jnp.sum(jnp.where(lane == kk, route, 0.0), axis=-1, keepdims=True)
        rank = jnp.sum(jnp.where(lane == 2 * TOP_K + kk, route, 0.0), axis=-1, keepdims=True)
        st = jnp.sum(jnp.where(lane == e.astype(I32), start, 0.0), axis=-1, keepdims=True)
        posm = jnp.where(lane == kk, st + rank, posm)
    post = posm.astype(I32).T
    posv_ref[...] = post[:SUBLANES]
    pos_ref[...] = post[:SUBLANES]

    tstart = (lax.broadcasted_iota(I32, (n_tiles, LANES), 0) * tg).astype(F32)
    lane_t = lax.broadcasted_iota(I32, (n_tiles, LANES), 1)
    done = jnp.where((lane_t < n_exp) & (end <= tstart), 1.0, 0.0)
    te = jnp.sum(done, axis=-1, keepdims=True)
    te_ref[...] = jnp.broadcast_to(te, (n_tiles, LANES)).astype(I32)
    n_used = jnp.max(end, axis=-1, keepdims=True) * (1.0 / tg)
    nu_ref[...] = jnp.broadcast_to(n_used, (1, LANES)).astype(I32)

    cp = pltpu.make_async_copy(posv_ref, pos_smem, sem_p)
    cp.start()
    cp.wait()

    def row_copy(t, kk):
        return pltpu.make_async_copy(xp_ref.at[pl.ds(t, 1)],
                                     xs_hbm.at[pl.ds(pos_smem[kk, t], 1)], sem_d)

    _issue_and_drain_rows(row_copy, tm)


def _scatter(route, cnt, xp, xs_init, *, tm, tg, n_exp, n_tiles):
    T = route.shape[0]
    n_rows, W = xs_init.shape
    return pl.pallas_call(
        functools.partial(_scatter_body, tm=tm, tg=tg, n_exp=n_exp, n_tiles=n_tiles),
        grid=(T // tm,),
        in_specs=[pl.BlockSpec((tm, LANES), lambda i: (i, 0)),
                  pl.BlockSpec((1, LANES), lambda i: (0, 0)),
                  pl.BlockSpec((tm, W), lambda i: (i, 0)),
                  pl.BlockSpec(memory_space=pl.ANY)],
        out_specs=[pl.BlockSpec((SUBLANES, tm), lambda i: (0, i)),
                   pl.BlockSpec((n_tiles, LANES), lambda i: (0, 0)),
                   pl.BlockSpec((1, LANES), lambda i: (0, 0)),
                   pl.BlockSpec(memory_space=pl.ANY)],
        out_shape=[jax.ShapeDtypeStruct((SUBLANES, T), I32),
                   jax.ShapeDtypeStruct((n_tiles, LANES), I32),
                   jax.ShapeDtypeStruct((1, LANES), I32),
                   jax.ShapeDtypeStruct((n_rows, W), U32)],
        scratch_shapes=[pltpu.VMEM((SUBLANES, tm), I32), pltpu.SMEM((SUBLANES, tm), I32),
                        pltpu.SemaphoreType.DMA, pltpu.SemaphoreType.DMA],
        input_output_aliases={3: 3},
        compiler_params=_cparams(("arbitrary",)),
        name="moe_scatter",
    )(route, cnt, xp, xs_init)


def _experts_body(te_ref, nu_ref, xs_ref, wgu_ref, bgu_ref, wd_ref, bd_ref, y_ref, *, F):
    i = pl.program_id(0)

    @pl.when(i < nu_ref[0])
    def _():
        lo, hi = _unpack_bf16_pairs(xs_ref[...])
        half = lo.shape[1]
        h = (jnp.dot(lo.astype(BF16), wgu_ref[:half, :], preferred_element_type=F32)
             + jnp.dot(hi.astype(BF16), wgu_ref[half:, :], preferred_element_type=F32)
             + bgu_ref[...])
        gate = jnp.minimum(h[:, :F], SWIGLU_LIMIT)
        up = jnp.clip(h[:, F:], -SWIGLU_LIMIT, SWIGLU_LIMIT)
        a = (up + 1.0) * (gate * jax.nn.sigmoid(SWIGLU_ALPHA * gate))
        y = jnp.dot(a.astype(BF16), wd_ref[...], preferred_element_type=F32) + bd_ref[...]
        y_ref[...] = _pack_bf16_pairs(y)


def _experts(te, n_used, xs, wgu, bgu, wd, bd, *, tg, n_tiles):
    n_rows, W = xs.shape
    E, D, F2 = wgu.shape
    F = F2 // 2
    row = lambda i, te, nu: (jnp.minimum(i, nu[0] - 1), 0)
    exp3 = lambda i, te, nu: (te[jnp.minimum(i, nu[0] - 1)], 0, 0)
    grid_spec = pltpu.PrefetchScalarGridSpec(
        num_scalar_prefetch=2,
        grid=(n_tiles,),
        in_specs=[pl.BlockSpec((tg, W), row),
                  pl.BlockSpec((None, D, F2), exp3),
                  pl.BlockSpec((None, 1, F2), exp3),
                  pl.BlockSpec((None, F, D), exp3),
                  pl.BlockSpec((None, 1, D), exp3)],
        out_specs=pl.BlockSpec((tg, W), row),
    )
    return pl.pallas_call(
        functools.partial(_experts_body, F=F),
        grid_spec=grid_spec,
        out_shape=jax.ShapeDtypeStruct((n_rows, W), U32),
        compiler_params=_cparams(("arbitrary",)),
        name="moe_experts",
    )(te, n_used, xs, wgu, bgu, wd, bd)


def _combine_body(pos_ref, route_ref, h_ref, y_hbm, o_ref, pos_smem, buf_ref, sem_p, sem_d, *, tm):
    cp = pltpu.make_async_copy(pos_ref, pos_smem, sem_p)
    cp.start()
    cp.wait()

    def row_copy(t, kk):
        return pltpu.make_async_copy(y_hbm.at[pl.ds(pos_smem[kk, t], 1)],
                                     buf_ref.at[kk, pl.ds(t, 1)], sem_d)

    _issue_and_drain_rows(row_copy, tm)

    route = route_ref[...]
    lane = lax.broadcasted_iota(I32, (tm, LANES), 1)
    half = buf_ref.shape[2]
    acc_lo = h_ref[:, :half]
    acc_hi = h_ref[:, half:]
    for kk in range(TOP_K):
        w = jnp.sum(jnp.where(lane == TOP_K + kk, route, 0.0), axis=-1, keepdims=True)
        lo, hi = _unpack_bf16_pairs(buf_ref[kk])
        acc_lo = acc_lo + w * lo
        acc_hi = acc_hi + w * hi
    o_ref[:, :half] = acc_lo
    o_ref[:, half:] = acc_hi


def _combine(pos, route, h1, y, *, tm):
    T, D = h1.shape
    W = y.shape[1]
    return pl.pallas_call(
        functools.partial(_combine_body, tm=tm),
        grid=(T // tm,),
        in_specs=[pl.BlockSpec((SUBLANES, tm), lambda i: (0, i)),
                  pl.BlockSpec((tm, LANES), lambda i: (i, 0)),
                  pl.BlockSpec((tm, D), lambda i: (i, 0)),
                  pl.BlockSpec(memory_space=pl.ANY)],
        out_specs=pl.BlockSpec((tm, D), lambda i: (i, 0)),
        out_shape=jax.ShapeDtypeStruct((T, D), F32),
        scratch_shapes=[pltpu.SMEM((SUBLANES, tm), I32), pltpu.VMEM((TOP_K, tm, W), U32),
                        pltpu.SemaphoreType.DMA, pltpu.SemaphoreType.DMA],
        compiler_params=_cparams(("arbitrary",)),
        name="moe_combine",
    )(pos, route, h1, y)


def _blockdiag_dense(w, width):
    nb, bs, _ = w.shape
    per = width // bs
    wt = w.reshape(nb // per, per, bs, bs)
    eye = jnp.eye(per, dtype=w.dtype)
    dense = jnp.einsum('gpio,pq->gpiqo', wt, eye)
    return dense.reshape(nb // per, width, width)


def _rope_tables(S):
    half = DA_HEAD_DIM // 2
    inv = ROPE_THETA ** (-(jnp.arange(half, dtype=F32) * 2.0 / DA_HEAD_DIM))
    ang = jnp.arange(S, dtype=F32)[:, None] * inv[None, :]
    cos = jnp.tile(jnp.cos(ang), (1, LANES // half))
    sign = jnp.where((jnp.arange(LANES) % DA_HEAD_DIM) < half, -1.0, 1.0).astype(F32)
    sin = jnp.tile(jnp.sin(ang), (1, LANES // half)) * sign[None, :]
    return cos, sin


def _tile(n, pref):
    return pref if n % pref == 0 else n


def _layer(h2, mem2, B, S, lambda_init, attn_norm_g, w_in, b_gate, da_q_norm_g, da_k_norm_g,
           da_lambda_q1, da_lambda_k1, da_lambda_q2, da_lambda_k2, da_subln_g, ml_conv_w, ml_conv_b,
           ml_wq, ml_wk, ml_wv, ml_w_if, ml_b_if, ml_out_norm_g, ml_skip, mem_norm_g, ca_w_kv,
           ca_q_norm_g, ca_k_norm_g, w_branch_da, w_branch_ml, w_branch_ca, w_out, ffn_norm_g,
           router_w, router_b, w_gate_up, b_gate_up, w_down, b_down):
    T, D = h2.shape
    M = mem2.shape[0] // B
    QK = DA_HEADS * 2 * DA_HEAD_DIM
    VW = DA_HEADS * DA_V_DIM
    C = ml_conv_w.shape[1]
    CAW = ca_w_kv.shape[1] // 2
    E = router_w.shape[1]

    o = [0, QK, 2 * QK, 2 * QK + VW, 2 * QK + VW + C, 2 * QK + VW + 2 * C, 2 * QK + VW + 2 * C + CAW]
    w_re = jnp.concatenate([w_in[:, o[0]:o[3]], w_in[:, o[5]:o[6]], w_in[:, o[3]:o[5]], w_in[:, o[6]:]],
                           axis=1).astype(BF16)
    tn = 1024
    col_v, col_caq, col_mlx, col_mlz, col_gate = 2 * QK, 2 * QK + VW, 2 * QK + VW + CAW, \
        2 * QK + VW + CAW + C, 2 * QK + VW + CAW + 2 * C
    proj = _inproj(h2, attn_norm_g[None, :], w_re, b_gate[None, :], tm=_tile(T, 1024), tn=tn,
                   n_plain=col_gate // tn)

    g2 = jnp.stack([jnp.tile(da_q_norm_g, LANES // DA_HEAD_DIM),
                    jnp.tile(da_k_norm_g, LANES // DA_HEAD_DIM)])[:, None, :]
    cos_t, sin_t = _rope_tables(S)
    lane_grp = jnp.arange(LANES) // DA_HEAD_DIM
    gm = (lane_grp[:, None] == lane_grp[None, :]).astype(BF16)
    qk = _qkrope(proj, g2, cos_t, sin_t, gm, T=T, S=S, width=QK, tm=_tile(S, 1024))
    y_da = _diff_attention(qk, proj, da_lambda_q1[None, :], da_lambda_k1[None, :], da_lambda_q2[None, :],
                           da_lambda_k2[None, :], da_subln_g[None, :], B=B, S=S, tq=_tile(S, 512),
                           v_col0=col_v // LANES, lambda_init=lambda_init)

    bd = jnp.stack([_blockdiag_dense(ml_wq, MXU_DIM), _blockdiag_dense(ml_wk, MXU_DIM),
                    _blockdiag_dense(ml_wv, MXU_DIM)]).astype(BF16)
    wif = jnp.pad(ml_w_if.reshape(3, C, 2 * ML_HEADS), ((0, 0), (0, 0), (0, LANES - 2 * ML_HEADS))).astype(BF16)
    bif = jnp.pad(ml_b_if, (0, LANES - 2 * ML_HEADS))[None, :]
    xc, mq, mk, mv, gates = _mlpre(proj, ml_conv_w, ml_conv_b[None, :], bd, wif, bif, B=B, S=S, C=C,
                                   tm=_tile(S, 512), x_col=col_mlx // C)
    dml = C // ML_HEADS
    y_ml = _mlstm(mq, mk, mv, gates, xc, proj, ml_out_norm_g[None, :], ml_skip[None, :], B=B, S=S, C=C,
                  L=_tile(S, 256), z_col0=col_mlz // dml)

    kmem, vmem = _memkv(mem2, mem_norm_g[None, :], ca_w_kv.astype(BF16), ca_k_norm_g[None, :], B=B, M=M, W=CAW)
    y_ca = _xattn(proj, kmem, vmem, ca_q_norm_g[None, :], B=B, S=S, M=M, W=CAW, tm=_tile(S, 512),
                  q_col=col_caq // CAW)

    tmx = _tile(T, 512)
    tri = (jnp.arange(tmx)[None, :] < jnp.arange(tmx)[:, None]).astype(BF16)
    rw = jnp.pad(router_w, ((0, 0), (0, LANES - E))).astype(BF16)
    rb = jnp.pad(router_b, (0, LANES - E))[None, :]
    h1, xp, route, cnt = _mix(h2, y_da, y_ml, y_ca, proj, w_branch_da.astype(BF16), w_branch_ml.astype(BF16),
                              w_branch_ca.astype(BF16), w_out.astype(BF16), ffn_norm_g[None, :], rw, rb, tri,
                              tm=tmx, g_col=col_gate // D, n_exp=E)

    tg = _tile(T, 512)
    n_tiles = (T * TOP_K) // tg + E
    n_rows = n_tiles * tg
    xs0 = jnp.zeros((n_rows, D // 2), U32)
    pos, te, n_used, xs = _scatter(route, cnt, xp, xs0, tm=_tile(T, 512), tg=tg, n_exp=E, n_tiles=n_tiles)
    y = _experts(te[:, 0], n_used[0, :1], xs, w_gate_up.astype(BF16), b_gate_up[:, None, :],
                 w_down.astype(BF16), b_down[:, None, :], tg=tg, n_tiles=n_tiles)
    return _combine(pos, route, h1, y, tm=_tile(T, 256))


def kernel(x, mem, attn_norm_g, w_in, b_gate, da_q_norm_g, da_k_norm_g, da_lambda_q1, da_lambda_k1, da_lambda_q2, da_lambda_k2, da_subln_g, ml_conv_w, ml_conv_b, ml_wq, ml_wk, ml_wv, ml_w_if, ml_b_if, ml_out_norm_g, ml_skip, mem_norm_g, ca_w_kv, ca_q_norm_g, ca_k_norm_g, w_branch_da, w_branch_ml, w_branch_ca, w_out, ffn_norm_g, router_w, router_b, w_gate_up, b_gate_up, w_down, b_down):
    B, S, D = x.shape
    depth = w_in.shape[0]
    h2 = x.reshape(B * S, D)
    mem2 = mem.reshape(B * mem.shape[1], D)
    params = (attn_norm_g, w_in, b_gate, da_q_norm_g, da_k_norm_g, da_lambda_q1, da_lambda_k1,
              da_lambda_q2, da_lambda_k2, da_subln_g, ml_conv_w, ml_conv_b, ml_wq, ml_wk, ml_wv, ml_w_if,
              ml_b_if, ml_out_norm_g, ml_skip, mem_norm_g, ca_w_kv, ca_q_norm_g, ca_k_norm_g, w_branch_da,
              w_branch_ml, w_branch_ca, w_out, ffn_norm_g, router_w, router_b, w_gate_up, b_gate_up,
              w_down, b_down)
    for l in range(depth):
        lambda_init = 0.8 - 0.6 * math.exp(-0.3 * l)
        h2 = _layer(h2, mem2, B, S, lambda_init, *[p[l] for p in params])
    return h2.reshape(B, S, D)
```

```python
import functools
import math

import jax
import jax.numpy as jnp
from jax import lax
from jax.experimental import pallas as pl
from jax.experimental.pallas import tpu as pltpu

F32 = jnp.float32
BF16 = jnp.bfloat16
I32 = jnp.int32
U32 = jnp.uint32

NORM_EPS = 1e-6
ROPE_THETA = 10000.0
CHUNK = 64

DA_HEADS = 8
DA_HEAD_DIM = 64
DA_V_DIM = 128
ML_HEADS = 4
ML_CONV = 4
ML_QKV_BLOCK = 4
CA_HEADS = 4
N_BRANCH = 3
TOP_K = 4
SWIGLU_LIMIT = 7.0
SWIGLU_ALPHA = 1.702

LANES = 128
SUBLANES = 8
MXU_DIM = 256
VMEM_LIMIT = 56 * 1024 * 1024
NEG_BIG = -1e30
DA_BOUND_SLACK = 1.01
DA_BOUND_LIMIT = 40.0


def _cparams(sem):
    return pltpu.CompilerParams(dimension_semantics=sem, vmem_limit_bytes=VMEM_LIMIT)


def _silu(x):
    return x * jax.nn.sigmoid(x)


def _inproj_body(x_ref, g_ref, w_ref, b_ref, o_ref, xn_ref, *, n_plain):
    j = pl.program_id(1)

    @pl.when(j == 0)
    def _():
        x = x_ref[...]
        ms = jnp.mean(x * x, axis=-1, keepdims=True)
        xn_ref[...] = (x * lax.rsqrt(ms + NORM_EPS) * g_ref[...]).astype(BF16)

    acc = jnp.dot(xn_ref[...], w_ref[...], preferred_element_type=F32)

    @pl.when(j < n_plain)
    def _():
        o_ref[...] = acc.astype(o_ref.dtype)

    @pl.when(j >= n_plain)
    def _():
        o_ref[...] = jax.nn.sigmoid(acc + b_ref[...]).astype(o_ref.dtype)


def _inproj(x2, g, w_bf, b_gate, *, tm, tn, n_plain):
    T, D = x2.shape
    N = w_bf.shape[1]
    return pl.pallas_call(
        functools.partial(_inproj_body, n_plain=n_plain),
        grid=(T // tm, N // tn),
        in_specs=[
            pl.BlockSpec((tm, D), lambda i, j: (i, 0)),
            pl.BlockSpec((1, D), lambda i, j: (0, 0)),
            pl.BlockSpec((D, tn), lambda i, j: (0, j)),
            pl.BlockSpec((1, tn), lambda i, j: (0, jnp.maximum(j - n_plain, 0))),
        ],
        out_specs=pl.BlockSpec((tm, tn), lambda i, j: (i, j)),
        out_shape=jax.ShapeDtypeStruct((T, N), BF16),
        scratch_shapes=[pltpu.VMEM((tm, D), BF16)],
        compiler_params=_cparams(("arbitrary", "arbitrary")),
        name="inproj",
    )(x2, g, w_bf, b_gate)


def _group_sumsq(xb, gm):
    sq = xb * xb
    hi = sq.astype(BF16)
    lo = (sq - hi.astype(F32)).astype(BF16)
    return (jnp.dot(hi, gm, preferred_element_type=F32)
            + jnp.dot(lo, gm, preferred_element_type=F32))


def _qkrope_body(x_ref, g_ref, cos_ref, sin_ref, gm_ref, o_ref, *, ncb, scale):
    c = pl.program_id(1)
    g = g_ref[...] * jnp.where(c == 0, scale, 1.0).astype(F32)
    cos = cos_ref[...]
    sin = sin_ref[...]
    gm = gm_ref[...]
    lane = lax.broadcasted_iota(I32, (1, LANES), 1)
    first_half = (lane & (DA_HEAD_DIM - 1)) < (DA_HEAD_DIM // 2)
    for cb in range(ncb):
        sl = slice(cb * LANES, (cb + 1) * LANES)
        xb = x_ref[:, sl].astype(F32)
        ss = _group_sumsq(xb, gm)
        y = xb * lax.rsqrt(ss * (1.0 / DA_HEAD_DIM) + NORM_EPS) * g
        sw = jnp.where(first_half, pltpu.roll(y, LANES - DA_HEAD_DIM // 2, 1),
                       pltpu.roll(y, DA_HEAD_DIM // 2, 1))
        o_ref[:, sl] = (y * cos + sw * sin).astype(o_ref.dtype)


def _qkrope(proj, g2, cos_t, sin_t, gm, *, T, S, width, tm):
    n_s = S // tm
    return pl.pallas_call(
        functools.partial(_qkrope_body, ncb=width // LANES, scale=DA_HEAD_DIM ** -0.5 * math.log2(math.e)),
        grid=(T // tm, 2),
        in_specs=[
            pl.BlockSpec((tm, width), lambda i, c: (i, c)),
            pl.BlockSpec((None, 1, LANES), lambda i, c: (c, 0, 0)),
            pl.BlockSpec((tm, LANES), lambda i, c: (i % n_s, 0)),
            pl.BlockSpec((tm, LANES), lambda i, c: (i % n_s, 0)),
            pl.BlockSpec((LANES, LANES), lambda i, c: (0, 0)),
        ],
        out_specs=pl.BlockSpec((tm, width), lambda i, c: (i, c)),
        out_shape=jax.ShapeDtypeStruct((T, 2 * width), BF16),
        compiler_params=_cparams(("arbitrary", "arbitrary")),
        name="qkrope",
    )(proj, g2, cos_t, sin_t, gm)


def _comp_sumsq(x16, gc_ref):
    x = x16.astype(F32)
    sq = x * x
    hi = sq.astype(BF16)
    lo = (sq - hi.astype(F32)).astype(BF16)
    return [jnp.dot(hi, gc_ref[c], preferred_element_type=F32)
            + jnp.dot(lo, gc_ref[c], preferred_element_type=F32) for c in range(2)]


def _da_body(q_ref, k_ref, v_ref, lq1_ref, lk1_ref, lq2_ref, lk2_ref, subg_ref, gc_ref, o_ref,
             m_ref, acc_ref, kmax_ref, *, tq, n_kv, lambda_init):
    i = pl.program_id(2)
    q = q_ref[...]
    lane = lax.broadcasted_iota(I32, (1, LANES), 1)
    lo = lane < DA_HEAD_DIM
    zero = jnp.zeros_like(q)
    qc = (jnp.where(lo, q, zero), jnp.where(lo, zero, q))

    @pl.when(i == 0)
    def _():
        def kbody(j, carry):
            kk = k_ref[pl.ds(pl.multiple_of(j * tq, tq), tq), :]
            n2 = _comp_sumsq(kk, gc_ref)
            return tuple(jnp.maximum(carry[c], jnp.max(n2[c], axis=0, keepdims=True)) for c in range(2))
        init = (jnp.zeros((1, LANES), F32), jnp.zeros((1, LANES), F32))
        kmax = lax.fori_loop(0, n_kv, kbody, init)
        kmax_ref[0] = kmax[0]
        kmax_ref[1] = kmax[1]

    qn2 = _comp_sumsq(q, gc_ref)
    bound = [jnp.sqrt(qn2[c] * kmax_ref[c]) * DA_BOUND_SLACK for c in range(2)]
    bound_max = jnp.max(jnp.maximum(bound[0], bound[1]))

    acc_ref[...] = jnp.zeros(acc_ref.shape, F32)
    ones = jnp.ones((tq, LANES), BF16)

    def scores(j):
        kj = k_ref[pl.ds(pl.multiple_of(j * tq, tq), tq), :]
        return [lax.dot_general(qc[c], kj, (((1,), (1,)), ((), ())), preferred_element_type=F32)
                for c in range(2)]

    def diag_mask():
        r = lax.broadcasted_iota(I32, (tq, tq), 0)
        cidx = lax.broadcasted_iota(I32, (tq, tq), 1)
        return (cidx | (CHUNK - 1)) <= (r | (CHUNK - 1))

    def v_aug(j):
        return jnp.concatenate([v_ref[pl.ds(pl.multiple_of(j * tq, tq), tq), :], ones], axis=1)

    def consume_bounded(j, ss, masked):
        vj = v_aug(j)
        ps = []
        for c in range(2):
            p = jnp.exp2(ss[c] - jnp.concatenate([bound[c]] * (tq // LANES), axis=1))
            if masked:
                p = jnp.where(diag_mask(), p, 0.0)
            ps.append(p.astype(BF16))
        pvs = [jnp.dot(ps[c], vj, preferred_element_type=F32) for c in range(2)]
        for c in range(2):
            acc_ref[c] = acc_ref[c] + pvs[c]

    def consume_online(j, ss, masked):
        vj = v_aug(j)
        ps, alphas = [], []
        for c in range(2):
            s = jnp.where(diag_mask(), ss[c], NEG_BIG) if masked else ss[c]
            m_old = m_ref[c]
            m_new = jnp.maximum(m_old, jnp.max(s, axis=-1, keepdims=True))
            ps.append(jnp.exp2(s - jnp.concatenate([m_new] * (tq // LANES), axis=1)).astype(BF16))
            alphas.append(jnp.exp2(m_old - m_new))
            m_ref[c] = m_new
        pvs = [jnp.dot(ps[c], vj, preferred_element_type=F32) for c in range(2)]
        for c in range(2):
            acc_ref[c] = jnp.concatenate([alphas[c], alphas[c]], axis=1) * acc_ref[c] + pvs[c]

    def sweep(consume):
        def pair(jj, carry):
            sa = scores(2 * jj)
            sb = scores(2 * jj + 1)
            consume(2 * jj, sa, False)
            consume(2 * jj + 1, sb, False)
            return carry

        lax.fori_loop(0, lax.shift_right_logical(i, 1), pair, 0)

        @pl.when((i & 1) == 1)
        def _():
            consume(i - 1, scores(i - 1), False)

        consume(i, scores(i), True)

    @pl.when(bound_max < DA_BOUND_LIMIT)
    def _():
        sweep(consume_bounded)

    @pl.when(jnp.logical_not(bound_max < DA_BOUND_LIMIT))
    def _():
        m_ref[...] = jnp.full(m_ref.shape, NEG_BIG, F32)
        sweep(consume_online)

    f32 = F32
    lam = (jnp.exp(jnp.sum(lq1_ref[...].astype(f32) * lk1_ref[...].astype(f32), keepdims=True))
           - jnp.exp(jnp.sum(lq2_ref[...].astype(f32) * lk2_ref[...].astype(f32), keepdims=True))
           + lambda_init)
    a1 = acc_ref[0]
    a2 = acc_ref[1]
    o = a1[:, :DA_V_DIM] / a1[:, DA_V_DIM:] - lam * (a2[:, :DA_V_DIM] / a2[:, DA_V_DIM:])
    ms = jnp.mean(o * o, axis=-1, keepdims=True)
    o = o * lax.rsqrt(ms + NORM_EPS) * subg_ref[...]
    o_ref[...] = (o * (1.0 - lambda_init)).astype(o_ref.dtype)


def _diff_attention(qk, proj, lq1, lk1, lq2, lk2, subg, *, B, S, tq, v_col0, lambda_init):
    T = B * S
    H = DA_HEADS
    n_q = S // tq
    vec = lambda n: pl.BlockSpec((1, n), lambda b, h, i: (0, 0))
    comp = jnp.arange(LANES) // DA_HEAD_DIM
    gc = jnp.broadcast_to((comp[None, :, None] == jnp.arange(2)[:, None, None]), (2, LANES, LANES)).astype(BF16)
    return pl.pallas_call(
        functools.partial(_da_body, tq=tq, n_kv=n_q, lambda_init=lambda_init),
        grid=(B, H, n_q),
        in_specs=[
            pl.BlockSpec((tq, LANES), lambda b, h, i: (b * n_q + i, h)),
            pl.BlockSpec((S, LANES), lambda b, h, i: (b, H + h)),
            pl.BlockSpec((S, LANES), lambda b, h, i: (b, v_col0 + h)),
            vec(DA_HEAD_DIM), vec(DA_HEAD_DIM), vec(DA_HEAD_DIM), vec(DA_HEAD_DIM),
            vec(DA_V_DIM),
            pl.BlockSpec((2, LANES, LANES), lambda b, h, i: (0, 0, 0)),
        ],
        out_specs=pl.BlockSpec((tq, LANES), lambda b, h, i: (b * n_q + i, h)),
        out_shape=jax.ShapeDtypeStruct((T, H * DA_V_DIM), BF16),
        scratch_shapes=[pltpu.VMEM((2, tq, LANES), F32), pltpu.VMEM((2, tq, 2 * DA_V_DIM), F32),
                        pltpu.VMEM((2, 1, LANES), F32)],
        compiler_params=_cparams(("arbitrary", "arbitrary", "arbitrary")),
        name="diff_attn",
    )(qk, qk, proj, lq1, lk1, lq2, lk2, subg, gc)


def _mlpre_body(x_ref, cw_ref, cb_ref, bd_ref, wif_ref, bif_ref,
                xc_ref, q_ref, k_ref, v_ref, g_ref, prev_ref, *, tm, ncb, kscale):
    s_idx = pl.program_id(1)

    @pl.when(s_idx == 0)
    def _():
        prev_ref[...] = jnp.zeros(prev_ref.shape, F32)

    row8 = lax.broadcasted_iota(I32, (SUBLANES, MXU_DIM), 0)
    gacc = jnp.zeros((tm, LANES), F32) + bif_ref[...]
    for cb in range(ncb):
        sl = slice(cb * MXU_DIM, (cb + 1) * MXU_DIM)
        xb16 = x_ref[:, sl]
        x = xb16.astype(F32)
        prev = prev_ref[:, sl]
        conv = x * cw_ref[ML_CONV - 1:ML_CONV, sl] + cb_ref[:, sl]
        for d in range(1, ML_CONV):
            xs = pltpu.roll(x, d, 0)
            ps = pltpu.roll(prev, d, 0)
            head = jnp.where(row8 < d, ps, xs[:SUBLANES])
            shifted = jnp.concatenate([head, xs[SUBLANES:]], axis=0)
            conv = conv + shifted * cw_ref[ML_CONV - 1 - d:ML_CONV - d, sl]
        prev_ref[:, sl] = x[tm - SUBLANES:]
        xc = _silu(conv)
        xc16 = xc.astype(BF16)
        q = jnp.dot(xc16, bd_ref[0, cb], preferred_element_type=F32)
        k = jnp.dot(xc16, bd_ref[1, cb], preferred_element_type=F32)
        v = jnp.dot(xb16, bd_ref[2, cb], preferred_element_type=F32)
        q16, k16, v16 = q.astype(BF16), k.astype(BF16), v.astype(BF16)
        gacc = gacc + jnp.dot(q16, wif_ref[0, sl, :], preferred_element_type=F32)
        gacc = gacc + jnp.dot(k16, wif_ref[1, sl, :], preferred_element_type=F32)
        gacc = gacc + jnp.dot(v16, wif_ref[2, sl, :], preferred_element_type=F32)
        xc_ref[:, sl] = xc16
        q_ref[:, sl] = q16
        k_ref[:, sl] = (k * kscale).astype(BF16)
        v_ref[:, sl] = v16
    g_ref[...] = gacc


def _mlpre(proj, conv_w, conv_b, bd, wif, bif, *, B, S, C, tm, x_col):
    T = B * S
    n_s = S // tm
    row = lambda i: pl.BlockSpec((tm, C), lambda b, s: (b * n_s + s, i))
    outs = pl.pallas_call(
        functools.partial(_mlpre_body, tm=tm, ncb=C // MXU_DIM, kscale=(C // ML_HEADS) ** -0.5),
        grid=(B, n_s),
        in_specs=[
            row(x_col),
            pl.BlockSpec((ML_CONV, C), lambda b, s: (0, 0)),
            pl.BlockSpec((1, C), lambda b, s: (0, 0)),
            pl.BlockSpec(bd.shape, lambda b, s: (0, 0, 0, 0)),
            pl.BlockSpec(wif.shape, lambda b, s: (0, 0, 0)),
            pl.BlockSpec((1, LANES), lambda b, s: (0, 0)),
        ],
        out_specs=[row(0), row(0), row(0), row(0),
                   pl.BlockSpec((tm, LANES), lambda b, s: (b * n_s + s, 0))],
        out_shape=[jax.ShapeDtypeStruct((T, C), BF16)] * 4 + [jax.ShapeDtypeStruct((T, LANES), F32)],
        scratch_shapes=[pltpu.VMEM((SUBLANES, C), F32)],
        compiler_params=_cparams(("arbitrary", "arbitrary")),
        name="mlstm_pre",
    )(proj, conv_w, conv_b, bd, wif, bif)
    return outs


def _mlstm_body(q_ref, k_ref, v_ref, g_ref, xc_ref, z_ref, ng_ref, skip_ref, o_ref,
                c_ref, cb_ref, n_ref, m_ref, *, L):
    h = pl.program_id(1)
    c_idx = pl.program_id(2)

    @pl.when(c_idx == 0)
    def _():
        c_ref[...] = jnp.zeros(c_ref.shape, F32)
        cb_ref[...] = jnp.zeros(cb_ref.shape, BF16)
        n_ref[...] = jnp.zeros(n_ref.shape, F32)
        m_ref[...] = jnp.zeros(m_ref.shape, F32)

    g = g_ref[...]
    gt = g.T
    lane = lax.broadcasted_iota(I32, (1, LANES), 1)
    sub = lax.broadcasted_iota(I32, (LANES, 1), 0)
    i_col = jnp.sum(jnp.where(lane == h, g, 0.0), axis=1, keepdims=True)
    f_col = jnp.sum(jnp.where(lane == ML_HEADS + h, g, 0.0), axis=1, keepdims=True)
    i_row = jnp.sum(jnp.where(sub == h, gt, 0.0), axis=0, keepdims=True)
    f_row = jnp.sum(jnp.where(sub == ML_HEADS + h, gt, 0.0), axis=0, keepdims=True)
    lf_col = jax.nn.log_sigmoid(f_col)
    lf_row = jax.nn.log_sigmoid(f_row)

    t_idx = lax.broadcasted_iota(I32, (L, L), 0)
    s_idx = lax.broadcasted_iota(I32, (L, L), 1)
    tri = s_idx <= t_idx
    b_col = jnp.sum(jnp.where(tri, lf_row, 0.0), axis=1, keepdims=True)
    b_row = jnp.sum(jnp.where(t_idx <= s_idx, lf_col, 0.0), axis=0, keepdims=True)
    b_last = jnp.sum(lf_row, axis=1, keepdims=True)

    m_prev = m_ref[...]
    dlog = jnp.where(tri, b_col - b_row + i_row, NEG_BIG)
    inter_log = b_col + m_prev
    m_rowv = jnp.maximum(inter_log, jnp.max(dlog, axis=1, keepdims=True))
    dw = jnp.exp(dlog - m_rowv)
    inter_w = jnp.exp(inter_log - m_rowv)

    q = q_ref[...]
    k = k_ref[...]
    v = v_ref[...]
    qk = lax.dot_general(q, k, (((1,), (1,)), ((), ())), preferred_element_type=F32)
    qc = jnp.dot(q, cb_ref[...], preferred_element_type=F32)
    s = qk * dw

    state_row = b_last - b_row + i_row
    state_col = b_last - b_col + i_col
    m_new = jnp.maximum(b_last + m_prev, jnp.max(state_row, axis=1, keepdims=True))
    decay = jnp.exp(b_last + m_prev - m_new)
    ws_col = jnp.exp(state_col - m_new)
    kwf = k.astype(F32) * ws_col

    num = jnp.dot(s.astype(BF16), v, preferred_element_type=F32) + inter_w * qc
    dc = lax.dot_general(kwf.astype(BF16), v, (((0,), (0,)), ((), ())), preferred_element_type=F32)
    den = (jnp.sum(s, axis=1, keepdims=True)
           + inter_w * jnp.sum(q.astype(F32) * n_ref[...], axis=1, keepdims=True))
    hh = num / jnp.maximum(jnp.abs(den), jnp.exp(-m_rowv))

    c_new = decay * c_ref[...] + dc
    c_ref[...] = c_new
    cb_ref[...] = c_new.astype(BF16)
    n_ref[...] = decay * n_ref[...] + jnp.sum(kwf, axis=0, keepdims=True)
    m_ref[...] = m_new

    mu = jnp.mean(hh, axis=-1, keepdims=True)
    cen = hh - mu
    var = jnp.mean(cen * cen, axis=-1, keepdims=True)
    hn = cen * lax.rsqrt(var + NORM_EPS) * ng_ref[...]
    y = (hn + skip_ref[...] * xc_ref[...].astype(F32)) * _silu(z_ref[...].astype(F32))
    o_ref[...] = y.astype(o_ref.dtype)


def _mlstm(q, k, v, gates, xc, proj, norm_g, skip, *, B, S, C, L, z_col0):
    T = B * S
    H = ML_HEADS
    d = C // H
    nc = S // L
    blk = lambda off: pl.BlockSpec((L, d), lambda b, h, c: (b * nc + c, off + h))
    par = pl.BlockSpec((1, d), lambda b, h, c: (0, h))
    return pl.pallas_call(
        functools.partial(_mlstm_body, L=L),
        grid=(B, H, nc),
        in_specs=[blk(0), blk(0), blk(0),
                  pl.BlockSpec((L, LANES), lambda b, h, c: (b * nc + c, 0)),
                  blk(0), blk(z_col0), par, par],
        out_specs=blk(0),
        out_shape=jax.ShapeDtypeStruct((T, C), BF16),
        scratch_shapes=[pltpu.VMEM((d, d), F32), pltpu.VMEM((d, d), BF16), pltpu.VMEM((1, d), F32),
                        pltpu.VMEM((1, 1), F32)],
        compiler_params=_cparams(("arbitrary", "arbitrary", "arbitrary")),
        name="mlstm_scan",
    )(q, k, v, gates, xc, proj, norm_g, skip)


def _memkv_body(mem_ref, mg_ref, w_ref, kg_ref, k_ref, v_ref, *, W, dh):
    x = mem_ref[...]
    ms = jnp.mean(x * x, axis=-1, keepdims=True)
    xn = (x * lax.rsqrt(ms + NORM_EPS) * mg_ref[...]).astype(BF16)
    kv = jnp.dot(xn, w_ref[...], preferred_element_type=F32)
    for hd in range(W // dh):
        sl = slice(hd * dh, (hd + 1) * dh)
        kh = kv[:, sl]
        msk = jnp.mean(kh * kh, axis=-1, keepdims=True)
        k_ref[:, sl] = (kh * lax.rsqrt(msk + NORM_EPS) * kg_ref[...]).astype(BF16)
    v_ref[...] = kv[:, W:].astype(BF16)


def _memkv(mem2, mem_g, w_kv_bf, k_g, *, B, M, W):
    D = mem2.shape[1]
    dh = W // CA_HEADS
    return pl.pallas_call(
        functools.partial(_memkv_body, W=W, dh=dh),
        grid=(B,),
        in_specs=[pl.BlockSpec((M, D), lambda b: (b, 0)),
                  pl.BlockSpec((1, D), lambda b: (0, 0)),
                  pl.BlockSpec((D, 2 * W), lambda b: (0, 0)),
                  pl.BlockSpec((1, dh), lambda b: (0, 0))],
        out_specs=[pl.BlockSpec((M, W), lambda b: (b, 0))] * 2,
        out_shape=[jax.ShapeDtypeStruct((B * M, W), BF16)] * 2,
        compiler_params=_cparams(("arbitrary",)),
        name="mem_kv",
    )(mem2, mem_g, w_kv_bf, k_g)


def _xattn_body(q_ref, k_ref, v_ref, qg_ref, o_ref, *, W, dh):
    scale = dh ** -0.5
    for hd in range(W // dh):
        sl = slice(hd * dh, (hd + 1) * dh)
        qh = q_ref[:, sl].astype(F32)
        ms = jnp.mean(qh * qh, axis=-1, keepdims=True)
        qn = (qh * lax.rsqrt(ms + NORM_EPS) * (qg_ref[...] * scale)).astype(BF16)
        s = lax.dot_general(qn, k_ref[:, sl], (((1,), (1,)), ((), ())), preferred_element_type=F32)
        mx = jnp.max(s, axis=-1, keepdims=True)
        p = jnp.exp(s - mx)
        p = p / jnp.sum(p, axis=-1, keepdims=True)
        o_ref[:, sl] = jnp.dot(p.astype(BF16), v_ref[:, sl],
                               preferred_element_type=F32).astype(o_ref.dtype)


def _xattn(proj, kmem, vmem, q_g, *, B, S, M, W, tm, q_col):
    T = B * S
    n_s = S // tm
    dh = W // CA_HEADS
    return pl.pallas_call(
        functools.partial(_xattn_body, W=W, dh=dh),
        grid=(T // tm,),
        in_specs=[pl.BlockSpec((tm, W), lambda i: (i, q_col)),
                  pl.BlockSpec((M, W), lambda i: (i // n_s, 0)),
                  pl.BlockSpec((M, W), lambda i: (i // n_s, 0)),
                  pl.BlockSpec((1, dh), lambda i: (0, 0))],
        out_specs=pl.BlockSpec((tm, W), lambda i: (i, 0)),
        out_shape=jax.ShapeDtypeStruct((T, W), BF16),
        compiler_params=_cparams(("arbitrary",)),
        name="mem_xattn",
    )(proj, kmem, vmem, q_g)


def _pack_bf16_pairs(x):
    w = x.shape[1] // 2
    xr = x.astype(BF16).astype(F32)
    lo = pltpu.bitcast(xr[:, :w], U32) >> 16
    hi = pltpu.bitcast(xr[:, w:], U32) & jnp.uint32(0xFFFF0000)
    return lo | hi


def _unpack_bf16_pairs(wd):
    lo = pltpu.bitcast(wd << 16, F32)
    hi = pltpu.bitcast(wd & jnp.uint32(0xFFFF0000), F32)
    return lo, hi


def _mix_body(x_ref, yda_ref, yml_ref, yca_ref, gda_ref, gml_ref, gca_ref, wda_ref, wml_ref, wca_ref,
              wout_ref, fg_ref, rw_ref, rb_ref, tri_ref,
              h_ref, xp_ref, route_ref, cnt_ref, carry_ref, *, n_exp):
    i = pl.program_id(0)

    @pl.when(i == 0)
    def _():
        carry_ref[...] = jnp.zeros(carry_ref.shape, F32)

    mix = (gda_ref[...].astype(F32) * jnp.dot(yda_ref[...], wda_ref[...], preferred_element_type=F32)
           + gml_ref[...].astype(F32) * jnp.dot(yml_ref[...], wml_ref[...], preferred_element_type=F32)
           + gca_ref[...].astype(F32) * jnp.dot(yca_ref[...], wca_ref[...], preferred_element_type=F32))
    h1 = x_ref[...] + jnp.dot(mix.astype(BF16), wout_ref[...], preferred_element_type=F32)
    h_ref[...] = h1
    ms = jnp.mean(h1 * h1, axis=-1, keepdims=True)
    xn = h1 * lax.rsqrt(ms + NORM_EPS) * fg_ref[...]
    xp_ref[...] = _pack_bf16_pairs(xn)

    logits = jnp.dot(xn.astype(BF16), rw_ref[...], preferred_element_type=F32) + rb_ref[...]
    tm = logits.shape[0]
    lane = lax.broadcasted_iota(I32, (tm, LANES), 1)
    work = jnp.where(lane < n_exp, logits, NEG_BIG)
    sel = jnp.zeros((tm, LANES), F32)
    vals, idxs = [], []
    for _ in range(TOP_K):
        mx = jnp.max(work, axis=-1, keepdims=True)
        idx = jnp.min(jnp.where(work == mx, lane, LANES), axis=-1, keepdims=True)
        hit = lane == idx
        sel = jnp.where(hit, 1.0, sel)
        work = jnp.where(hit, NEG_BIG, work)
        vals.append(mx)
        idxs.append(idx)
    exps = [jnp.exp(v - vals[0]) for v in vals]
    tot = exps[0] + exps[1] + exps[2] + exps[3]

    cum = jnp.dot(tri_ref[...], sel.astype(BF16), preferred_element_type=F32) + carry_ref[...]
    route = jnp.zeros((tm, LANES), F32)
    for kk in range(TOP_K):
        rank = jnp.sum(jnp.where(lane == idxs[kk], cum, 0.0), axis=-1, keepdims=True)
        route = jnp.where(lane == kk, idxs[kk].astype(F32), route)
        route = jnp.where(lane == TOP_K + kk, exps[kk] / tot, route)
        route = jnp.where(lane == 2 * TOP_K + kk, rank, route)
    route_ref[...] = route
    carry_ref[...] = carry_ref[...] + jnp.sum(sel, axis=0, keepdims=True)
    cnt_ref[...] = carry_ref[...]


def _mix(x2, yda, yml, yca, proj, wda, wml, wca, wout, fg, rw, rb, tri, *, tm, g_col, n_exp):
    T, D = x2.shape
    C = yml.shape[1]
    const = lambda shape: pl.BlockSpec(shape, lambda i: (0,) * len(shape))
    return pl.pallas_call(
        functools.partial(_mix_body, n_exp=n_exp),
        grid=(T // tm,),
        in_specs=[pl.BlockSpec((tm, D), lambda i: (i, 0)),
                  pl.BlockSpec((tm, D), lambda i: (i, 0)),
                  pl.BlockSpec((tm, C), lambda i: (i, 0)),
                  pl.BlockSpec((tm, D), lambda i: (i, 0)),
                  pl.BlockSpec((tm, D), lambda i: (i, g_col)),
                  pl.BlockSpec((tm, D), lambda i: (i, g_col + 1)),
                  pl.BlockSpec((tm, D), lambda i: (i, g_col + 2)),
                  const((D, D)), const((C, D)), const((D, D)), const((D, D)),
                  const((1, D)), const((D, LANES)), const((1, LANES)), const((tm, tm))],
        out_specs=[pl.BlockSpec((tm, D), lambda i: (i, 0)),
                   pl.BlockSpec((tm, D // 2), lambda i: (i, 0)),
                   pl.BlockSpec((tm, LANES), lambda i: (i, 0)),
                   pl.BlockSpec((1, LANES), lambda i: (0, 0))],
        out_shape=[jax.ShapeDtypeStruct((T, D), F32),
                   jax.ShapeDtypeStruct((T, D // 2), U32),
                   jax.ShapeDtypeStruct((T, LANES), F32),
                   jax.ShapeDtypeStruct((1, LANES), F32)],
        scratch_shapes=[pltpu.VMEM((1, LANES), F32)],
        compiler_params=_cparams(("arbitrary",)),
        name="mix_route",
    )(x2, yda, yml, yca, proj, proj, proj, wda, wml, wca, wout, fg, rw, rb, tri)


ROW_DMA_UNROLL = 8


def _issue_and_drain_rows(row_copy, tm):
    def issue(t, carry):
        for kk in range(TOP_K):
            row_copy(t, kk).start(priority=kk % 2)
        return carry

    lax.fori_loop(0, tm, issue, 0, unroll=ROW_DMA_UNROLL)

    def drain(t, carry):
        for kk in range(TOP_K):
            row_copy(t, kk).wait()
        return carry

    lax.fori_loop(0, tm, drain, 0, unroll=ROW_DMA_UNROLL)


def _group_starts(cnt, tg, n_exp):
    lane_r = lax.broadcasted_iota(I32, (LANES, LANES), 0)
    lane_c = lax.broadcasted_iota(I32, (LANES, LANES), 1)
    padded = jnp.ceil(cnt * (1.0 / tg)) * tg
    padded_col = jnp.sum(jnp.where(lane_r == lane_c, padded, 0.0), axis=1, keepdims=True)
    start = jnp.sum(jnp.where(lane_r < lane_c, padded_col, 0.0), axis=0, keepdims=True)
    return start, start + padded


def _scatter_body(route_ref, cnt_ref, xp_ref, xs_in_hbm, pos_ref, te_ref, nu_ref, xs_hbm,
                  posv_ref, pos_smem, sem_p, sem_d, *, tm, tg, n_exp, n_tiles):
    del xs_in_hbm
    route = route_ref[...]
    lane = lax.broadcasted_iota(I32, (tm, LANES), 1)
    start, end = _group_starts(cnt_ref[...], tg, n_exp)

    posm = jnp.zeros((tm, LANES), F32)
    for kk in range(TOP_K):
        e = jnp.sum(jnp.where(lane == kk, route, 0.0), axis=-1, keepdims=True)
        rank = jnp.sum(jnp.where(lane == 2 * TOP_K + kk, route, 0.0), axis=-1, keepdims=True)
        st = jnp.sum(jnp.where(lane == e.astype(I32), start, 0.0), axis=-1, keepdims=True)
        posm = jnp.where(lane == kk, st + rank, posm)
    post = posm.astype(I32).T
    posv_ref[...] = post[:SUBLANES]
    pos_ref[...] = post[:SUBLANES]

    tstart = (lax.broadcasted_iota(I32, (n_tiles, LANES), 0) * tg).astype(F32)
    lane_t = lax.broadcasted_iota(I32, (n_tiles, LANES), 1)
    done = jnp.where((lane_t < n_exp) & (end <= tstart), 1.0, 0.0)
    te = jnp.sum(done, axis=-1, keepdims=True)
    te_ref[...] = jnp.broadcast_to(te, (n_tiles, LANES)).astype(I32)
    n_used = jnp.max(end, axis=-1, keepdims=True) * (1.0 / tg)
    nu_ref[...] = jnp.broadcast_to(n_used, (1, LANES)).astype(I32)

    cp = pltpu.make_async_copy(posv_ref, pos_smem, sem_p)
    cp.start()
    cp.wait()

    def row_copy(t, kk):
        return pltpu.make_async_copy(xp_ref.at[pl.ds(t, 1)],
                                     xs_hbm.at[pl.ds(pos_smem[kk, t], 1)], sem_d)

    _issue_and_drain_rows(row_copy, tm)


def _scatter(route, cnt, xp, xs_init, *, tm, tg, n_exp, n_tiles):
    T = route.shape[0]
    n_rows, W = xs_init.shape
    return pl.pallas_call(
        functools.partial(_scatter_body, tm=tm, tg=tg, n_exp=n_exp, n_tiles=n_tiles),
        grid=(T // tm,),
        in_specs=[pl.BlockSpec((tm, LANES), lambda i: (i, 0)),
                  pl.BlockSpec((1, LANES), lambda i: (0, 0)),
                  pl.BlockSpec((tm, W), lambda i: (i, 0)),
                  pl.BlockSpec(memory_space=pl.ANY)],
        out_specs=[pl.BlockSpec((SUBLANES, tm), lambda i: (0, i)),
                   pl.BlockSpec((n_tiles, LANES), lambda i: (0, 0)),
                   pl.BlockSpec((1, LANES), lambda i: (0, 0)),
                   pl.BlockSpec(memory_space=pl.ANY)],
        out_shape=[jax.ShapeDtypeStruct((SUBLANES, T), I32),
                   jax.ShapeDtypeStruct((n_tiles, LANES), I32),
                   jax.ShapeDtypeStruct((1, LANES), I32),
                   jax.ShapeDtypeStruct((n_rows, W), U32)],
        scratch_shapes=[pltpu.VMEM((SUBLANES, tm), I32), pltpu.SMEM((SUBLANES, tm), I32),
                        pltpu.SemaphoreType.DMA, pltpu.SemaphoreType.DMA],
        input_output_aliases={3: 3},
        compiler_params=_cparams(("arbitrary",)),
        name="moe_scatter",
    )(route, cnt, xp, xs_init)


def _experts_body(te_ref, nu_ref, xs_ref, wgu_ref, bgu_ref, wd_ref, bd_ref, y_ref, wgu16_ref, wd16_ref,
                  *, F):
    i = pl.program_id(0)
    live = i < nu_ref[0]
    new_expert = (i == 0) | (te_ref[i] != te_ref[jnp.maximum(i - 1, 0)])

    @pl.when(live & new_expert)
    def _():
        wgu16_ref[...] = wgu_ref[...].astype(BF16)
        wd16_ref[...] = wd_ref[...].astype(BF16)

    @pl.when(live)
    def _():
        lo, hi = _unpack_bf16_pairs(xs_ref[...])
        half = lo.shape[1]
        h = (jnp.dot(lo.astype(BF16), wgu16_ref[:half, :], preferred_element_type=F32)
             + jnp.dot(hi.astype(BF16), wgu16_ref[half:, :], preferred_element_type=F32)
             + bgu_ref[...])
        gate = jnp.minimum(h[:, :F], SWIGLU_LIMIT)
        up = jnp.clip(h[:, F:], -SWIGLU_LIMIT, SWIGLU_LIMIT)
        a = (up + 1.0) * (gate * jax.nn.sigmoid(SWIGLU_ALPHA * gate))
        y = jnp.dot(a.astype(BF16), wd16_ref[...], preferred_element_type=F32) + bd_ref[...]
        y_ref[...] = _pack_bf16_pairs(y)


def _experts(te, n_used, xs, wgu, bgu, wd, bd, *, tg, n_tiles):
    n_rows, W = xs.shape
    E, D, F2 = wgu.shape
    F = F2 // 2
    row = lambda i, te, nu: (jnp.minimum(i, nu[0] - 1), 0)
    exp3 = lambda i, te, nu: (te[jnp.minimum(i, nu[0] - 1)], 0, 0)
    grid_spec = pltpu.PrefetchScalarGridSpec(
        num_scalar_prefetch=2,
        grid=(n_tiles,),
        in_specs=[pl.BlockSpec((tg, W), row),
                  pl.BlockSpec((None, D, F2), exp3),
                  pl.BlockSpec((None, 1, F2), exp3),
                  pl.BlockSpec((None, F, D), exp3),
                  pl.BlockSpec((None, 1, D), exp3)],
        out_specs=pl.BlockSpec((tg, W), row),
        scratch_shapes=[pltpu.VMEM((D, F2), BF16), pltpu.VMEM((F, D), BF16)],
    )
    return pl.pallas_call(
        functools.partial(_experts_body, F=F),
        grid_spec=grid_spec,
        out_shape=jax.ShapeDtypeStruct((n_rows, W), U32),
        compiler_params=_cparams(("arbitrary",)),
        name="moe_experts",
    )(te, n_used, xs, wgu, bgu, wd, bd)


def _combine_body(pos_ref, route_ref, h_ref, y_hbm, o_ref, pos_smem, buf_ref, sem_p, sem_d, *, tm):
    cp = pltpu.make_async_copy(pos_ref, pos_smem, sem_p)
    cp.start()
    cp.wait()

    def row_copy(t, kk):
        return pltpu.make_async_copy(y_hbm.at[pl.ds(pos_smem[kk, t], 1)],
                                     buf_ref.at[kk, pl.ds(t, 1)], sem_d)

    _issue_and_drain_rows(row_copy, tm)

    route = route_ref[...]
    lane = lax.broadcasted_iota(I32, (tm, LANES), 1)
    half = buf_ref.shape[2]
    acc_lo = h_ref[:, :half]
    acc_hi = h_ref[:, half:]
    for kk in range(TOP_K):
        w = jnp.sum(jnp.where(lane == TOP_K + kk, route, 0.0), axis=-1, keepdims=True)
        lo, hi = _unpack_bf16_pairs(buf_ref[kk])
        acc_lo = acc_lo + w * lo
        acc_hi = acc_hi + w * hi
    o_ref[:, :half] = acc_lo
    o_ref[:, half:] = acc_hi


def _combine(pos, route, h1, y, *, tm):
    T, D = h1.shape
    W = y.shape[1]
    return pl.pallas_call(
        functools.partial(_combine_body, tm=tm),
        grid=(T // tm,),
        in_specs=[pl.BlockSpec((SUBLANES, tm), lambda i: (0, i)),
                  pl.BlockSpec((tm, LANES), lambda i: (i, 0)),
                  pl.BlockSpec((tm, D), lambda i: (i, 0)),
                  pl.BlockSpec(memory_space=pl.ANY)],
        out_specs=pl.BlockSpec((tm, D), lambda i: (i, 0)),
        out_shape=jax.ShapeDtypeStruct((T, D), F32),
        scratch_shapes=[pltpu.SMEM((SUBLANES, tm), I32), pltpu.VMEM((TOP_K, tm, W), U32),
                        pltpu.SemaphoreType.DMA, pltpu.SemaphoreType.DMA],
        compiler_params=_cparams(("arbitrary",)),
        name="moe_combine",
    )(pos, route, h1, y)


def _blockdiag_dense(w, width):
    nb, bs, _ = w.shape
    per = width // bs
    wt = w.reshape(nb // per, per, bs, bs)
    eye = jnp.eye(per, dtype=w.dtype)
    dense = jnp.einsum('gpio,pq->gpiqo', wt, eye)
    return dense.reshape(nb // per, width, width)


def _rope_tables(S):
    half = DA_HEAD_DIM // 2
    inv = ROPE_THETA ** (-(jnp.arange(half, dtype=F32) * 2.0 / DA_HEAD_DIM))
    ang = jnp.arange(S, dtype=F32)[:, None] * inv[None, :]
    cos = jnp.tile(jnp.cos(ang), (1, LANES // half))
    sign = jnp.where((jnp.arange(LANES) % DA_HEAD_DIM) < half, -1.0, 1.0).astype(F32)
    sin = jnp.tile(jnp.sin(ang), (1, LANES // half)) * sign[None, :]
    return cos, sin


def _tile(n, pref):
    return pref if n % pref == 0 else n


def _layer(h2, mem2, B, S, lambda_init, attn_norm_g, w_in, b_gate, da_q_norm_g, da_k_norm_g,
           da_lambda_q1, da_lambda_k1, da_lambda_q2, da_lambda_k2, da_subln_g, ml_conv_w, ml_conv_b,
           ml_wq, ml_wk, ml_wv, ml_w_if, ml_b_if, ml_out_norm_g, ml_skip, mem_norm_g, ca_w_kv,
           ca_q_norm_g, ca_k_norm_g, w_branch_da, w_branch_ml, w_branch_ca, w_out, ffn_norm_g,
           router_w, router_b, w_gate_up, b_gate_up, w_down, b_down):
    T, D = h2.shape
    M = mem2.shape[0] // B
    QK = DA_HEADS * 2 * DA_HEAD_DIM
    VW = DA_HEADS * DA_V_DIM
    C = ml_conv_w.shape[1]
    CAW = ca_w_kv.shape[1] // 2
    E = router_w.shape[1]

    o = [0, QK, 2 * QK, 2 * QK + VW, 2 * QK + VW + C, 2 * QK + VW + 2 * C, 2 * QK + VW + 2 * C + CAW]
    w_re = jnp.concatenate([w_in[:, o[0]:o[3]], w_in[:, o[5]:o[6]], w_in[:, o[3]:o[5]], w_in[:, o[6]:]],
                           axis=1).astype(BF16)
    tn = 1024
    col_v, col_caq, col_mlx, col_mlz, col_gate = 2 * QK, 2 * QK + VW, 2 * QK + VW + CAW, \
        2 * QK + VW + CAW + C, 2 * QK + VW + CAW + 2 * C
    proj = _inproj(h2, attn_norm_g[None, :], w_re, b_gate[None, :], tm=_tile(T, 1024), tn=tn,
                   n_plain=col_gate // tn)

    g2 = jnp.stack([jnp.tile(da_q_norm_g, LANES // DA_HEAD_DIM),
                    jnp.tile(da_k_norm_g, LANES // DA_HEAD_DIM)])[:, None, :]
    cos_t, sin_t = _rope_tables(S)
    lane_grp = jnp.arange(LANES) // DA_HEAD_DIM
    gm = (lane_grp[:, None] == lane_grp[None, :]).astype(BF16)
    qk = _qkrope(proj, g2, cos_t, sin_t, gm, T=T, S=S, width=QK, tm=_tile(S, 1024))
    y_da = _diff_attention(qk, proj, da_lambda_q1[None, :], da_lambda_k1[None, :], da_lambda_q2[None, :],
                           da_lambda_k2[None, :], da_subln_g[None, :], B=B, S=S, tq=_tile(S, 512),
                           v_col0=col_v // LANES, lambda_init=lambda_init)

    bd = jnp.stack([_blockdiag_dense(ml_wq, MXU_DIM), _blockdiag_dense(ml_wk, MXU_DIM),
                    _blockdiag_dense(ml_wv, MXU_DIM)]).astype(BF16)
    wif = jnp.pad(ml_w_if.reshape(3, C, 2 * ML_HEADS), ((0, 0), (0, 0), (0, LANES - 2 * ML_HEADS))).astype(BF16)
    bif = jnp.pad(ml_b_if, (0, LANES - 2 * ML_HEADS))[None, :]
    xc, mq, mk, mv, gates = _mlpre(proj, ml_conv_w, ml_conv_b[None, :], bd, wif, bif, B=B, S=S, C=C,
                                   tm=_tile(S, 512), x_col=col_mlx // C)
    dml = C // ML_HEADS
    y_ml = _mlstm(mq, mk, mv, gates, xc, proj, ml_out_norm_g[None, :], ml_skip[None, :], B=B, S=S, C=C,
                  L=_tile(S, 256), z_col0=col_mlz // dml)

    kmem, vmem = _memkv(mem2, mem_norm_g[None, :], ca_w_kv.astype(BF16), ca_k_norm_g[None, :], B=B, M=M, W=CAW)
    y_ca = _xattn(proj, kmem, vmem, ca_q_norm_g[None, :], B=B, S=S, M=M, W=CAW, tm=_tile(S, 512),
                  q_col=col_caq // CAW)

    tmx = _tile(T, 512)
    tri = (jnp.arange(tmx)[None, :] < jnp.arange(tmx)[:, None]).astype(BF16)
    rw = jnp.pad(router_w, ((0, 0), (0, LANES - E))).astype(BF16)
    rb = jnp.pad(router_b, (0, LANES - E))[None, :]
    h1, xp, route, cnt = _mix(h2, y_da, y_ml, y_ca, proj, w_branch_da.astype(BF16), w_branch_ml.astype(BF16),
                              w_branch_ca.astype(BF16), w_out.astype(BF16), ffn_norm_g[None, :], rw, rb, tri,
                              tm=tmx, g_col=col_gate // D, n_exp=E)

    tg = _tile(T, 512)
    n_tiles = (T * TOP_K) // tg + E
    n_rows = n_tiles * tg
    xs0 = jnp.zeros((n_rows, D // 2), U32)
    pos, te, n_used, xs = _scatter(route, cnt, xp, xs0, tm=_tile(T, 512), tg=tg, n_exp=E, n_tiles=n_tiles)
    y = _experts(te[:, 0], n_used[0, :1], xs, w_gate_up, b_gate_up[:, None, :],
                 w_down, b_down[:, None, :], tg=tg, n_tiles=n_tiles)
    return _combine(pos, route, h1, y, tm=_tile(T, 256))


def kernel(x, mem, attn_norm_g, w_in, b_gate, da_q_norm_g, da_k_norm_g, da_lambda_q1, da_lambda_k1, da_lambda_q2, da_lambda_k2, da_subln_g, ml_conv_w, ml_conv_b, ml_wq, ml_wk, ml_wv, ml_w_if, ml_b_if, ml_out_norm_g, ml_skip, mem_norm_g, ca_w_kv, ca_q_norm_g, ca_k_norm_g, w_branch_da, w_branch_ml, w_branch_ca, w_out, ffn_norm_g, router_w, router_b, w_gate_up, b_gate_up, w_down, b_down):
    B, S, D = x.shape
    depth = w_in.shape[0]
    h2 = x.reshape(B * S, D)
    mem2 = mem.reshape(B * mem.shape[1], D)
    params = (attn_norm_g, w_in, b_gate, da_q_norm_g, da_k_norm_g, da_lambda_q1, da_lambda_k1,
              da_lambda_q2, da_lambda_k2, da_subln_g, ml_conv_w, ml_conv_b, ml_wq, ml_wk, ml_wv, ml_w_if,
              ml_b_if, ml_out_norm_g, ml_skip, mem_norm_g, ca_w_kv, ca_q_norm_g, ca_k_norm_g, w_branch_da,
              w_branch_ml, w_branch_ca, w_out, ffn_norm_g, router_w, router_b, w_gate_up, b_gate_up,
              w_down, b_down)
    for l in range(depth):
        lambda_init = 0.8 - 0.6 * math.exp(-0.3 * l)
        h2 = _layer(h2, mem2, B, S, lambda_init, *[p[l] for p in params])
    return h2.reshape(B, S, D)
```

```python
import functools
import math

import jax
import jax.numpy as jnp
from jax import lax
from jax.experimental import pallas as pl
from jax.experimental.pallas import tpu as pltpu

F32 = jnp.float32
BF16 = jnp.bfloat16
I32 = jnp.int32
U32 = jnp.uint32

NORM_EPS = 1e-6
ROPE_THETA = 10000.0
CHUNK = 64

DA_HEADS = 8
DA_HEAD_DIM = 64
DA_V_DIM = 128
ML_HEADS = 4
ML_CONV = 4
ML_QKV_BLOCK = 4
CA_HEADS = 4
N_BRANCH = 3
TOP_K = 4
SWIGLU_LIMIT = 7.0
SWIGLU_ALPHA = 1.702

LANES = 128
SUBLANES = 8
MXU_DIM = 256
VMEM_LIMIT = 56 * 1024 * 1024
NEG_BIG = -1e30
DA_BOUND_SLACK = 1.01
DA_BOUND_LIMIT = 40.0


def _cparams(sem):
    return pltpu.CompilerParams(dimension_semantics=sem, vmem_limit_bytes=VMEM_LIMIT)


def _sigmoid(x):
    return 0.5 * jnp.tanh(0.5 * x) + 0.5


def _silu(x):
    return x * _sigmoid(x)


def _inproj_body(x_ref, g_ref, w_ref, o_ref, xn_ref):
    j = pl.program_id(1)

    @pl.when(j == 0)
    def _():
        x = x_ref[...]
        ms = jnp.mean(x * x, axis=-1, keepdims=True)
        xn_ref[...] = (x * lax.rsqrt(ms + NORM_EPS) * g_ref[...]).astype(BF16)

    o_ref[...] = jnp.dot(xn_ref[...], w_ref[...], preferred_element_type=F32).astype(o_ref.dtype)


def _inproj(x2, g, w_bf, *, tm, tn):
    T, D = x2.shape
    N = w_bf.shape[1]
    return pl.pallas_call(
        _inproj_body,
        grid=(T // tm, N // tn),
        in_specs=[
            pl.BlockSpec((tm, D), lambda i, j: (i, 0)),
            pl.BlockSpec((1, D), lambda i, j: (0, 0)),
            pl.BlockSpec((D, tn), lambda i, j: (0, j)),
        ],
        out_specs=pl.BlockSpec((tm, tn), lambda i, j: (i, j)),
        out_shape=jax.ShapeDtypeStruct((T, N), BF16),
        scratch_shapes=[pltpu.VMEM((tm, D), BF16)],
        compiler_params=_cparams(("arbitrary", "arbitrary")),
        name="inproj",
    )(x2, g, w_bf)


def _group_sumsq(xb, gm):
    sq = xb * xb
    hi = sq.astype(BF16)
    lo = (sq - hi.astype(F32)).astype(BF16)
    return (jnp.dot(hi, gm, preferred_element_type=F32)
            + jnp.dot(lo, gm, preferred_element_type=F32))


def _qkrope_body(x_ref, g_ref, cos_ref, sin_ref, gm_ref, o_ref, *, ncb, scale):
    c = pl.program_id(1)
    g = g_ref[...] * jnp.where(c == 0, scale, 1.0).astype(F32)
    cos = cos_ref[...]
    sin = sin_ref[...]
    gm = gm_ref[...]
    lane = lax.broadcasted_iota(I32, (1, LANES), 1)
    first_half = (lane & (DA_HEAD_DIM - 1)) < (DA_HEAD_DIM // 2)
    for cb in range(ncb):
        sl = slice(cb * LANES, (cb + 1) * LANES)
        xb = x_ref[:, sl].astype(F32)
        ss = _group_sumsq(xb, gm)
        y = xb * lax.rsqrt(ss * (1.0 / DA_HEAD_DIM) + NORM_EPS) * g
        sw = jnp.where(first_half, pltpu.roll(y, LANES - DA_HEAD_DIM // 2, 1),
                       pltpu.roll(y, DA_HEAD_DIM // 2, 1))
        o_ref[:, sl] = (y * cos + sw * sin).astype(o_ref.dtype)


def _qkrope(proj, g2, cos_t, sin_t, gm, *, T, S, width, tm):
    n_s = S // tm
    return pl.pallas_call(
        functools.partial(_qkrope_body, ncb=width // LANES, scale=DA_HEAD_DIM ** -0.5 * math.log2(math.e)),
        grid=(T // tm, 2),
        in_specs=[
            pl.BlockSpec((tm, width), lambda i, c: (i, c)),
            pl.BlockSpec((None, 1, LANES), lambda i, c: (c, 0, 0)),
            pl.BlockSpec((tm, LANES), lambda i, c: (i % n_s, 0)),
            pl.BlockSpec((tm, LANES), lambda i, c: (i % n_s, 0)),
            pl.BlockSpec((LANES, LANES), lambda i, c: (0, 0)),
        ],
        out_specs=pl.BlockSpec((tm, width), lambda i, c: (i, c)),
        out_shape=jax.ShapeDtypeStruct((T, 2 * width), BF16),
        compiler_params=_cparams(("arbitrary", "arbitrary")),
        name="qkrope",
    )(proj, g2, cos_t, sin_t, gm)


def _comp_sumsq(x16, gc_ref):
    x = x16.astype(F32)
    sq = x * x
    hi = sq.astype(BF16)
    lo = (sq - hi.astype(F32)).astype(BF16)
    return [jnp.dot(hi, gc_ref[c], preferred_element_type=F32)
            + jnp.dot(lo, gc_ref[c], preferred_element_type=F32) for c in range(2)]


def _da_body(q_ref, k_ref, v_ref, lq1_ref, lk1_ref, lq2_ref, lk2_ref, subg_ref, gc_ref, o_ref,
             m_ref, acc_ref, kmax_ref, *, tq, n_kv, lambda_init):
    i = pl.program_id(2)
    q = q_ref[...]
    lane = lax.broadcasted_iota(I32, (1, LANES), 1)
    lo = lane < DA_HEAD_DIM
    zero = jnp.zeros_like(q)
    qc = (jnp.where(lo, q, zero), jnp.where(lo, zero, q))

    @pl.when(i == 0)
    def _():
        def kbody(j, carry):
            kk = k_ref[pl.ds(pl.multiple_of(j * tq, tq), tq), :]
            n2 = _comp_sumsq(kk, gc_ref)
            return tuple(jnp.maximum(carry[c], jnp.max(n2[c], axis=0, keepdims=True)) for c in range(2))
        init = (jnp.zeros((1, LANES), F32), jnp.zeros((1, LANES), F32))
        kmax = lax.fori_loop(0, n_kv, kbody, init)
        kmax_ref[0] = kmax[0]
        kmax_ref[1] = kmax[1]

    qn2 = _comp_sumsq(q, gc_ref)
    bound = [jnp.sqrt(qn2[c] * kmax_ref[c]) * DA_BOUND_SLACK for c in range(2)]
    bound_max = jnp.max(jnp.maximum(bound[0], bound[1]))

    acc_ref[...] = jnp.zeros(acc_ref.shape, F32)
    ones = jnp.ones((tq, LANES), BF16)

    def scores(j):
        kj = k_ref[pl.ds(pl.multiple_of(j * tq, tq), tq), :]
        return [lax.dot_general(qc[c], kj, (((1,), (1,)), ((), ())), preferred_element_type=F32)
                for c in range(2)]

    def diag_mask():
        r = lax.broadcasted_iota(I32, (tq, tq), 0)
        cidx = lax.broadcasted_iota(I32, (tq, tq), 1)
        return (cidx | (CHUNK - 1)) <= (r | (CHUNK - 1))

    def v_aug(j):
        return jnp.concatenate([v_ref[pl.ds(pl.multiple_of(j * tq, tq), tq), :], ones], axis=1)

    def consume_bounded(j, ss, masked):
        vj = v_aug(j)
        ps = []
        for c in range(2):
            p = jnp.exp2(ss[c] - jnp.concatenate([bound[c]] * (tq // LANES), axis=1))
            if masked:
                p = jnp.where(diag_mask(), p, 0.0)
            ps.append(p.astype(BF16))
        pvs = [jnp.dot(ps[c], vj, preferred_element_type=F32) for c in range(2)]
        for c in range(2):
            acc_ref[c] = acc_ref[c] + pvs[c]

    def consume_online(j, ss, masked):
        vj = v_aug(j)
        ps, alphas = [], []
        for c in range(2):
            s = jnp.where(diag_mask(), ss[c], NEG_BIG) if masked else ss[c]
            m_old = m_ref[c]
            m_new = jnp.maximum(m_old, jnp.max(s, axis=-1, keepdims=True))
            ps.append(jnp.exp2(s - jnp.concatenate([m_new] * (tq // LANES), axis=1)).astype(BF16))
            alphas.append(jnp.exp2(m_old - m_new))
            m_ref[c] = m_new
        pvs = [jnp.dot(ps[c], vj, preferred_element_type=F32) for c in range(2)]
        for c in range(2):
            acc_ref[c] = jnp.concatenate([alphas[c], alphas[c]], axis=1) * acc_ref[c] + pvs[c]

    def sweep(consume):
        def pair(jj, carry):
            sa = scores(2 * jj)
            sb = scores(2 * jj + 1)
            consume(2 * jj, sa, False)
            consume(2 * jj + 1, sb, False)
            return carry

        lax.fori_loop(0, lax.shift_right_logical(i, 1), pair, 0)

        @pl.when((i & 1) == 1)
        def _():
            consume(i - 1, scores(i - 1), False)

        consume(i, scores(i), True)

    @pl.when(bound_max < DA_BOUND_LIMIT)
    def _():
        sweep(consume_bounded)

    @pl.when(jnp.logical_not(bound_max < DA_BOUND_LIMIT))
    def _():
        m_ref[...] = jnp.full(m_ref.shape, NEG_BIG, F32)
        sweep(consume_online)

    f32 = F32
    lam = (jnp.exp(jnp.sum(lq1_ref[...].astype(f32) * lk1_ref[...].astype(f32), keepdims=True))
           - jnp.exp(jnp.sum(lq2_ref[...].astype(f32) * lk2_ref[...].astype(f32), keepdims=True))
           + lambda_init)
    a1 = acc_ref[0]
    a2 = acc_ref[1]
    o = a1[:, :DA_V_DIM] / a1[:, DA_V_DIM:] - lam * (a2[:, :DA_V_DIM] / a2[:, DA_V_DIM:])
    ms = jnp.mean(o * o, axis=-1, keepdims=True)
    o = o * lax.rsqrt(ms + NORM_EPS) * subg_ref[...]
    o_ref[...] = (o * (1.0 - lambda_init)).astype(o_ref.dtype)


def _diff_attention(qk, proj, lq1, lk1, lq2, lk2, subg, *, B, S, tq, v_col0, lambda_init):
    T = B * S
    H = DA_HEADS
    n_q = S // tq
    vec = lambda n: pl.BlockSpec((1, n), lambda b, h, i: (0, 0))
    comp = jnp.arange(LANES) // DA_HEAD_DIM
    gc = jnp.broadcast_to((comp[None, :, None] == jnp.arange(2)[:, None, None]), (2, LANES, LANES)).astype(BF16)
    return pl.pallas_call(
        functools.partial(_da_body, tq=tq, n_kv=n_q, lambda_init=lambda_init),
        grid=(B, H, n_q),
        in_specs=[
            pl.BlockSpec((tq, LANES), lambda b, h, i: (b * n_q + i, h)),
            pl.BlockSpec((S, LANES), lambda b, h, i: (b, H + h)),
            pl.BlockSpec((S, LANES), lambda b, h, i: (b, v_col0 + h)),
            vec(DA_HEAD_DIM), vec(DA_HEAD_DIM), vec(DA_HEAD_DIM), vec(DA_HEAD_DIM),
            vec(DA_V_DIM),
            pl.BlockSpec((2, LANES, LANES), lambda b, h, i: (0, 0, 0)),
        ],
        out_specs=pl.BlockSpec((tq, LANES), lambda b, h, i: (b * n_q + i, h)),
        out_shape=jax.ShapeDtypeStruct((T, H * DA_V_DIM), BF16),
        scratch_shapes=[pltpu.VMEM((2, tq, LANES), F32), pltpu.VMEM((2, tq, 2 * DA_V_DIM), F32),
                        pltpu.VMEM((2, 1, LANES), F32)],
        compiler_params=_cparams(("arbitrary", "arbitrary", "arbitrary")),
        name="diff_attn",
    )(qk, qk, proj, lq1, lk1, lq2, lk2, subg, gc)


def _mlpre_body(x_ref, cw_ref, cb_ref, bd_ref, wif_ref, bif_ref,
                xc_ref, q_ref, k_ref, v_ref, g_ref, prev_ref, *, tm, ncb, kscale):
    s_idx = pl.program_id(1)

    @pl.when(s_idx == 0)
    def _():
        prev_ref[...] = jnp.zeros(prev_ref.shape, F32)

    row8 = lax.broadcasted_iota(I32, (SUBLANES, MXU_DIM), 0)
    gacc = jnp.zeros((tm, LANES), F32) + bif_ref[...]
    for cb in range(ncb):
        sl = slice(cb * MXU_DIM, (cb + 1) * MXU_DIM)
        xb16 = x_ref[:, sl]
        x = xb16.astype(F32)
        prev = prev_ref[:, sl]
        conv = x * cw_ref[ML_CONV - 1:ML_CONV, sl] + cb_ref[:, sl]
        for d in range(1, ML_CONV):
            xs = pltpu.roll(x, d, 0)
            ps = pltpu.roll(prev, d, 0)
            head = jnp.where(row8 < d, ps, xs[:SUBLANES])
            shifted = jnp.concatenate([head, xs[SUBLANES:]], axis=0)
            conv = conv + shifted * cw_ref[ML_CONV - 1 - d:ML_CONV - d, sl]
        prev_ref[:, sl] = x[tm - SUBLANES:]
        xc = _silu(conv)
        xc16 = xc.astype(BF16)
        q = jnp.dot(xc16, bd_ref[0, cb], preferred_element_type=F32)
        k = jnp.dot(xc16, bd_ref[1, cb], preferred_element_type=F32)
        v = jnp.dot(xb16, bd_ref[2, cb], preferred_element_type=F32)
        q16, k16, v16 = q.astype(BF16), k.astype(BF16), v.astype(BF16)
        gacc = gacc + jnp.dot(q16, wif_ref[0, sl, :], preferred_element_type=F32)
        gacc = gacc + jnp.dot(k16, wif_ref[1, sl, :], preferred_element_type=F32)
        gacc = gacc + jnp.dot(v16, wif_ref[2, sl, :], preferred_element_type=F32)
        xc_ref[:, sl] = xc16
        q_ref[:, sl] = q16
        k_ref[:, sl] = (k * kscale).astype(BF16)
        v_ref[:, sl] = v16
    g_ref[...] = gacc


def _mlpre(proj, conv_w, conv_b, bd, wif, bif, *, B, S, C, tm, x_col):
    T = B * S
    n_s = S // tm
    row = lambda i: pl.BlockSpec((tm, C), lambda b, s: (b * n_s + s, i))
    outs = pl.pallas_call(
        functools.partial(_mlpre_body, tm=tm, ncb=C // MXU_DIM, kscale=(C // ML_HEADS) ** -0.5),
        grid=(B, n_s),
        in_specs=[
            row(x_col),
            pl.BlockSpec((ML_CONV, C), lambda b, s: (0, 0)),
            pl.BlockSpec((1, C), lambda b, s: (0, 0)),
            pl.BlockSpec(bd.shape, lambda b, s: (0, 0, 0, 0)),
            pl.BlockSpec(wif.shape, lambda b, s: (0, 0, 0)),
            pl.BlockSpec((1, LANES), lambda b, s: (0, 0)),
        ],
        out_specs=[row(0), row(0), row(0), row(0),
                   pl.BlockSpec((tm, LANES), lambda b, s: (b * n_s + s, 0))],
        out_shape=[jax.ShapeDtypeStruct((T, C), BF16)] * 4 + [jax.ShapeDtypeStruct((T, LANES), F32)],
        scratch_shapes=[pltpu.VMEM((SUBLANES, C), F32)],
        compiler_params=_cparams(("arbitrary", "arbitrary")),
        name="mlstm_pre",
    )(proj, conv_w, conv_b, bd, wif, bif)
    return outs


def _mlstm_body(q_ref, k_ref, v_ref, g_ref, xc_ref, z_ref, ng_ref, skip_ref, o_ref,
                c_ref, cb_ref, n_ref, m_ref, *, L):
    h = pl.program_id(1)
    c_idx = pl.program_id(2)

    @pl.when(c_idx == 0)
    def _():
        c_ref[...] = jnp.zeros(c_ref.shape, F32)
        cb_ref[...] = jnp.zeros(cb_ref.shape, BF16)
        n_ref[...] = jnp.zeros(n_ref.shape, F32)
        m_ref[...] = jnp.zeros(m_ref.shape, F32)

    g = g_ref[...]
    gt = g.T
    lane = lax.broadcasted_iota(I32, (1, LANES), 1)
    sub = lax.broadcasted_iota(I32, (LANES, 1), 0)
    i_col = jnp.sum(jnp.where(lane == h, g, 0.0), axis=1, keepdims=True)
    f_col = jnp.sum(jnp.where(lane == ML_HEADS + h, g, 0.0), axis=1, keepdims=True)
    i_row = jnp.sum(jnp.where(sub == h, gt, 0.0), axis=0, keepdims=True)
    f_row = jnp.sum(jnp.where(sub == ML_HEADS + h, gt, 0.0), axis=0, keepdims=True)
    lf_col = jax.nn.log_sigmoid(f_col)
    lf_row = jax.nn.log_sigmoid(f_row)

    t_idx = lax.broadcasted_iota(I32, (L, L), 0)
    s_idx = lax.broadcasted_iota(I32, (L, L), 1)
    tri = s_idx <= t_idx
    b_col = jnp.sum(jnp.where(tri, lf_row, 0.0), axis=1, keepdims=True)
    b_row = jnp.sum(jnp.where(t_idx <= s_idx, lf_col, 0.0), axis=0, keepdims=True)
    b_last = jnp.sum(lf_row, axis=1, keepdims=True)

    m_prev = m_ref[...]
    dlog = jnp.where(tri, b_col - b_row + i_row, NEG_BIG)
    inter_log = b_col + m_prev
    m_rowv = jnp.maximum(inter_log, jnp.max(dlog, axis=1, keepdims=True))
    dw = jnp.exp(dlog - m_rowv)
    inter_w = jnp.exp(inter_log - m_rowv)

    q = q_ref[...]
    k = k_ref[...]
    v = v_ref[...]
    qk = lax.dot_general(q, k, (((1,), (1,)), ((), ())), preferred_element_type=F32)
    qc = jnp.dot(q, cb_ref[...], preferred_element_type=F32)
    s = qk * dw

    state_row = b_last - b_row + i_row
    state_col = b_last - b_col + i_col
    m_new = jnp.maximum(b_last + m_prev, jnp.max(state_row, axis=1, keepdims=True))
    decay = jnp.exp(b_last + m_prev - m_new)
    ws_col = jnp.exp(state_col - m_new)
    kwf = k.astype(F32) * ws_col

    num = jnp.dot(s.astype(BF16), v, preferred_element_type=F32) + inter_w * qc
    dc = lax.dot_general(kwf.astype(BF16), v, (((0,), (0,)), ((), ())), preferred_element_type=F32)
    den = (jnp.sum(s, axis=1, keepdims=True)
           + inter_w * jnp.sum(q.astype(F32) * n_ref[...], axis=1, keepdims=True))
    hh = num / jnp.maximum(jnp.abs(den), jnp.exp(-m_rowv))

    c_new = decay * c_ref[...] + dc
    c_ref[...] = c_new
    cb_ref[...] = c_new.astype(BF16)
    n_ref[...] = decay * n_ref[...] + jnp.sum(kwf, axis=0, keepdims=True)
    m_ref[...] = m_new

    mu = jnp.mean(hh, axis=-1, keepdims=True)
    cen = hh - mu
    var = jnp.mean(cen * cen, axis=-1, keepdims=True)
    hn = cen * lax.rsqrt(var + NORM_EPS) * ng_ref[...]
    y = (hn + skip_ref[...] * xc_ref[...].astype(F32)) * _silu(z_ref[...].astype(F32))
    o_ref[...] = y.astype(o_ref.dtype)


def _mlstm(q, k, v, gates, xc, proj, norm_g, skip, *, B, S, C, L, z_col0):
    T = B * S
    H = ML_HEADS
    d = C // H
    nc = S // L
    blk = lambda off: pl.BlockSpec((L, d), lambda b, h, c: (b * nc + c, off + h))
    par = pl.BlockSpec((1, d), lambda b, h, c: (0, h))
    return pl.pallas_call(
        functools.partial(_mlstm_body, L=L),
        grid=(B, H, nc),
        in_specs=[blk(0), blk(0), blk(0),
                  pl.BlockSpec((L, LANES), lambda b, h, c: (b * nc + c, 0)),
                  blk(0), blk(z_col0), par, par],
        out_specs=blk(0),
        out_shape=jax.ShapeDtypeStruct((T, C), BF16),
        scratch_shapes=[pltpu.VMEM((d, d), F32), pltpu.VMEM((d, d), BF16), pltpu.VMEM((1, d), F32),
                        pltpu.VMEM((1, 1), F32)],
        compiler_params=_cparams(("arbitrary", "arbitrary", "arbitrary")),
        name="mlstm_scan",
    )(q, k, v, gates, xc, proj, norm_g, skip)


def _memkv_body(mem_ref, mg_ref, w_ref, kg_ref, k_ref, v_ref, *, W, dh):
    x = mem_ref[...]
    ms = jnp.mean(x * x, axis=-1, keepdims=True)
    xn = (x * lax.rsqrt(ms + NORM_EPS) * mg_ref[...]).astype(BF16)
    kv = jnp.dot(xn, w_ref[...], preferred_element_type=F32)
    for hd in range(W // dh):
        sl = slice(hd * dh, (hd + 1) * dh)
        kh = kv[:, sl]
        msk = jnp.mean(kh * kh, axis=-1, keepdims=True)
        k_ref[:, sl] = (kh * lax.rsqrt(msk + NORM_EPS) * kg_ref[...]).astype(BF16)
    v_ref[...] = kv[:, W:].astype(BF16)


def _memkv(mem2, mem_g, w_kv_bf, k_g, *, B, M, W):
    D = mem2.shape[1]
    dh = W // CA_HEADS
    return pl.pallas_call(
        functools.partial(_memkv_body, W=W, dh=dh),
        grid=(B,),
        in_specs=[pl.BlockSpec((M, D), lambda b: (b, 0)),
                  pl.BlockSpec((1, D), lambda b: (0, 0)),
                  pl.BlockSpec((D, 2 * W), lambda b: (0, 0)),
                  pl.BlockSpec((1, dh), lambda b: (0, 0))],
        out_specs=[pl.BlockSpec((M, W), lambda b: (b, 0))] * 2,
        out_shape=[jax.ShapeDtypeStruct((B * M, W), BF16)] * 2,
        compiler_params=_cparams(("arbitrary",)),
        name="mem_kv",
    )(mem2, mem_g, w_kv_bf, k_g)


def _xattn_body(q_ref, k_ref, v_ref, qg_ref, o_ref, *, W, dh):
    scale = dh ** -0.5
    for hd in range(W // dh):
        sl = slice(hd * dh, (hd + 1) * dh)
        qh = q_ref[:, sl].astype(F32)
        ms = jnp.mean(qh * qh, axis=-1, keepdims=True)
        qn = (qh * lax.rsqrt(ms + NORM_EPS) * (qg_ref[...] * scale)).astype(BF16)
        s = lax.dot_general(qn, k_ref[:, sl], (((1,), (1,)), ((), ())), preferred_element_type=F32)
        mx = jnp.max(s, axis=-1, keepdims=True)
        p = jnp.exp(s - mx)
        p = p / jnp.sum(p, axis=-1, keepdims=True)
        o_ref[:, sl] = jnp.dot(p.astype(BF16), v_ref[:, sl],
                               preferred_element_type=F32).astype(o_ref.dtype)


def _xattn(proj, kmem, vmem, q_g, *, B, S, M, W, tm, q_col):
    T = B * S
    n_s = S // tm
    dh = W // CA_HEADS
    return pl.pallas_call(
        functools.partial(_xattn_body, W=W, dh=dh),
        grid=(T // tm,),
        in_specs=[pl.BlockSpec((tm, W), lambda i: (i, q_col)),
                  pl.BlockSpec((M, W), lambda i: (i // n_s, 0)),
                  pl.BlockSpec((M, W), lambda i: (i // n_s, 0)),
                  pl.BlockSpec((1, dh), lambda i: (0, 0))],
        out_specs=pl.BlockSpec((tm, W), lambda i: (i, 0)),
        out_shape=jax.ShapeDtypeStruct((T, W), BF16),
        compiler_params=_cparams(("arbitrary",)),
        name="mem_xattn",
    )(proj, kmem, vmem, q_g)


def _store_tile_rows(ref, x):
    n = x.shape[0]
    for c in range(SUBLANES):
        ref[pl.ds(c, n, stride=SUBLANES), :] = x[:, c * LANES:(c + 1) * LANES]


def _load_tile_rows(ref, n):
    return [ref[pl.ds(c, n, stride=SUBLANES), :] for c in range(SUBLANES)]


def _mix_body(x_ref, yda_ref, yml_ref, yca_ref, gda_ref, gml_ref, gca_ref, bg_ref, wda_ref, wml_ref, wca_ref,
              wout_ref, fg_ref, rw_ref, rb_ref, tri_ref,
              h_ref, xp_ref, route_ref, cnt_ref, carry_ref, *, n_exp):
    i = pl.program_id(0)

    @pl.when(i == 0)
    def _():
        carry_ref[...] = jnp.zeros(carry_ref.shape, F32)

    def gate(g_ref, b):
        return _sigmoid(g_ref[...].astype(F32) + bg_ref[b])

    mix = (gate(gda_ref, 0) * jnp.dot(yda_ref[...], wda_ref[...], preferred_element_type=F32)
           + gate(gml_ref, 1) * jnp.dot(yml_ref[...], wml_ref[...], preferred_element_type=F32)
           + gate(gca_ref, 2) * jnp.dot(yca_ref[...], wca_ref[...], preferred_element_type=F32))
    h1 = x_ref[...] + jnp.dot(mix.astype(BF16), wout_ref[...], preferred_element_type=F32)
    h_ref[...] = h1
    ms = jnp.mean(h1 * h1, axis=-1, keepdims=True)
    xn = h1 * lax.rsqrt(ms + NORM_EPS) * fg_ref[...]
    _store_tile_rows(xp_ref, xn)

    logits = jnp.dot(xn.astype(BF16), rw_ref[...], preferred_element_type=F32) + rb_ref[...]
    tm = logits.shape[0]
    lane = lax.broadcasted_iota(I32, (tm, LANES), 1)
    work = jnp.where(lane < n_exp, logits, NEG_BIG)
    sel = jnp.zeros((tm, LANES), F32)
    vals, idxs = [], []
    for _ in range(TOP_K):
        mx = jnp.max(work, axis=-1, keepdims=True)
        idx = jnp.min(jnp.where(work == mx, lane, LANES), axis=-1, keepdims=True)
        hit = lane == idx
        sel = jnp.where(hit, 1.0, sel)
        work = jnp.where(hit, NEG_BIG, work)
        vals.append(mx)
        idxs.append(idx)
    exps = [jnp.exp(v - vals[0]) for v in vals]
    tot = exps[0] + exps[1] + exps[2] + exps[3]

    cum = jnp.dot(tri_ref[...], sel.astype(BF16), preferred_element_type=F32) + carry_ref[...]
    route = jnp.zeros((tm, LANES), F32)
    for kk in range(TOP_K):
        rank = jnp.sum(jnp.where(lane == idxs[kk], cum, 0.0), axis=-1, keepdims=True)
        route = jnp.where(lane == kk, idxs[kk].astype(F32), route)
        route = jnp.where(lane == TOP_K + kk, exps[kk] / tot, route)
        route = jnp.where(lane == 2 * TOP_K + kk, rank, route)
    route_ref[...] = route
    carry_ref[...] = carry_ref[...] + jnp.sum(sel, axis=0, keepdims=True)
    cnt_ref[...] = carry_ref[...]


def _mix(x2, yda, yml, yca, proj, bg, wda, wml, wca, wout, fg, rw, rb, tri, *, tm, g_col, n_exp):
    T, D = x2.shape
    C = yml.shape[1]
    const = lambda shape: pl.BlockSpec(shape, lambda i: (0,) * len(shape))
    return pl.pallas_call(
        functools.partial(_mix_body, n_exp=n_exp),
        grid=(T // tm,),
        in_specs=[pl.BlockSpec((tm, D), lambda i: (i, 0)),
                  pl.BlockSpec((tm, D), lambda i: (i, 0)),
                  pl.BlockSpec((tm, C), lambda i: (i, 0)),
                  pl.BlockSpec((tm, D), lambda i: (i, 0)),
                  pl.BlockSpec((tm, D), lambda i: (i, g_col)),
                  pl.BlockSpec((tm, D), lambda i: (i, g_col + 1)),
                  pl.BlockSpec((tm, D), lambda i: (i, g_col + 2)),
                  const((N_BRANCH, 1, D)),
                  const((D, D)), const((C, D)), const((D, D)), const((D, D)),
                  const((1, D)), const((D, LANES)), const((1, LANES)), const((tm, tm))],
        out_specs=[pl.BlockSpec((tm, D), lambda i: (i, 0)),
                   pl.BlockSpec((tm * SUBLANES, LANES), lambda i: (i, 0)),
                   pl.BlockSpec((tm, LANES), lambda i: (i, 0)),
                   pl.BlockSpec((1, LANES), lambda i: (0, 0))],
        out_shape=[jax.ShapeDtypeStruct((T, D), F32),
                   jax.ShapeDtypeStruct((T * SUBLANES, LANES), F32),
                   jax.ShapeDtypeStruct((T, LANES), F32),
                   jax.ShapeDtypeStruct((1, LANES), F32)],
        scratch_shapes=[pltpu.VMEM((1, LANES), F32)],
        compiler_params=_cparams(("arbitrary",)),
        name="mix_route",
    )(x2, yda, yml, yca, proj, proj, proj, bg, wda, wml, wca, wout, fg, rw, rb, tri)


ROW_DMA_UNROLL = 8


def _issue_rows(row_copy, tm):
    def issue(t, carry):
        for kk in range(TOP_K):
            row_copy(t, kk).start(priority=kk % 2)
        return carry

    lax.fori_loop(0, tm, issue, 0, unroll=ROW_DMA_UNROLL)


def _drain_rows(row_copy, tm):
    def drain(t, carry):
        for kk in range(TOP_K):
            row_copy(t, kk).wait()
        return carry

    lax.fori_loop(0, tm, drain, 0, unroll=ROW_DMA_UNROLL)


def _issue_and_drain_rows(row_copy, tm):
    _issue_rows(row_copy, tm)
    _drain_rows(row_copy, tm)


def _group_starts(cnt, tg, n_exp):
    lane_r = lax.broadcasted_iota(I32, (LANES, LANES), 0)
    lane_c = lax.broadcasted_iota(I32, (LANES, LANES), 1)
    padded = jnp.ceil(cnt * (1.0 / tg)) * tg
    padded_col = jnp.sum(jnp.where(lane_r == lane_c, padded, 0.0), axis=1, keepdims=True)
    start = jnp.sum(jnp.where(lane_r < lane_c, padded_col, 0.0), axis=0, keepdims=True)
    return start, start + padded


def _scatter_body(route_ref, route_next_ref, cnt_ref, xp_ref, xs_in_hbm, pos_ref, te_ref, nu_ref, xs_hbm,
                  posv_ref, pos_smem, sem_p, sem_d, *, tm, tg, n_exp, n_tiles):
    del xs_in_hbm
    i = pl.program_id(0)
    n = pl.num_programs(0)
    slot = i & 1
    lane = lax.broadcasted_iota(I32, (tm, LANES), 1)
    start, end = _group_starts(cnt_ref[...], tg, n_exp)

    def stage_positions(route, s):
        posm = jnp.zeros((tm, LANES), F32)
        for kk in range(TOP_K):
            e = jnp.sum(jnp.where(lane == kk, route, 0.0), axis=-1, keepdims=True)
            rank = jnp.sum(jnp.where(lane == 2 * TOP_K + kk, route, 0.0), axis=-1, keepdims=True)
            st = jnp.sum(jnp.where(lane == e.astype(I32), start, 0.0), axis=-1, keepdims=True)
            posm = jnp.where(lane == kk, (st + rank) * SUBLANES, posm)
        posv_ref[s] = posm.astype(I32).T[:SUBLANES]
        cp = pltpu.make_async_copy(posv_ref.at[s], pos_smem.at[s], sem_p)
        cp.start()
        cp.wait()

    @pl.when(i == 0)
    def _():
        stage_positions(route_ref[...], 0)
        tstart = (lax.broadcasted_iota(I32, (n_tiles, LANES), 0) * tg).astype(F32)
        lane_t = lax.broadcasted_iota(I32, (n_tiles, LANES), 1)
        done = jnp.where((lane_t < n_exp) & (end <= tstart), 1.0, 0.0)
        te = jnp.sum(done, axis=-1, keepdims=True)
        te_ref[...] = jnp.broadcast_to(te, (n_tiles, LANES)).astype(I32)
        n_used = jnp.max(end, axis=-1, keepdims=True) * (1.0 / tg)
        nu_ref[...] = jnp.broadcast_to(n_used, (1, LANES)).astype(I32)

    def row_copy(t, kk):
        dst = pl.multiple_of(pos_smem[slot, kk, t], SUBLANES)
        return pltpu.make_async_copy(xp_ref.at[pl.ds(pl.multiple_of(t * SUBLANES, SUBLANES), SUBLANES)],
                                     xs_hbm.at[pl.ds(dst, SUBLANES)], sem_d)

    _issue_rows(row_copy, tm)

    @pl.when(i + 1 < n)
    def _():
        stage_positions(route_next_ref[...], 1 - slot)

    pos_ref[...] = posv_ref[slot]
    _drain_rows(row_copy, tm)


def _scatter(route, cnt, xp, xs_init, *, tm, tg, n_exp, n_tiles):
    T = route.shape[0]
    n_rows, W = xs_init.shape
    n_steps = T // tm
    return pl.pallas_call(
        functools.partial(_scatter_body, tm=tm, tg=tg, n_exp=n_exp, n_tiles=n_tiles),
        grid=(n_steps,),
        in_specs=[pl.BlockSpec((tm, LANES), lambda i: (i, 0)),
                  pl.BlockSpec((tm, LANES), lambda i: (jnp.minimum(i + 1, n_steps - 1), 0)),
                  pl.BlockSpec((1, LANES), lambda i: (0, 0)),
                  pl.BlockSpec((tm * SUBLANES, W), lambda i: (i, 0)),
                  pl.BlockSpec(memory_space=pl.ANY)],
        out_specs=[pl.BlockSpec((SUBLANES, tm), lambda i: (0, i)),
                   pl.BlockSpec((n_tiles, LANES), lambda i: (0, 0)),
                   pl.BlockSpec((1, LANES), lambda i: (0, 0)),
                   pl.BlockSpec(memory_space=pl.ANY)],
        out_shape=[jax.ShapeDtypeStruct((SUBLANES, T), I32),
                   jax.ShapeDtypeStruct((n_tiles, LANES), I32),
                   jax.ShapeDtypeStruct((1, LANES), I32),
                   jax.ShapeDtypeStruct((n_rows, W), F32)],
        scratch_shapes=[pltpu.VMEM((2, SUBLANES, tm), I32), pltpu.SMEM((2, SUBLANES, tm), I32),
                        pltpu.SemaphoreType.DMA, pltpu.SemaphoreType.DMA],
        input_output_aliases={4: 3},
        compiler_params=_cparams(("arbitrary",)),
        name="moe_scatter",
    )(route, route, cnt, xp, xs_init)


def _experts_body(te_ref, nu_ref, xs_ref, wgu_ref, bgu_ref, wd_ref, bd_ref, y_ref, wgu16_ref, wd16_ref,
                  *, F, tg):
    i = pl.program_id(0)
    live = i < nu_ref[0]
    new_expert = (i == 0) | (te_ref[i] != te_ref[jnp.maximum(i - 1, 0)])

    @pl.when(live & new_expert)
    def _():
        wgu16_ref[...] = wgu_ref[...].astype(BF16)
        wd16_ref[...] = wd_ref[...].astype(BF16)

    @pl.when(live)
    def _():
        x = jnp.concatenate([c.astype(BF16) for c in _load_tile_rows(xs_ref, tg)], axis=1)
        h = jnp.dot(x, wgu16_ref[...], preferred_element_type=F32) + bgu_ref[...]
        gate = jnp.minimum(h[:, :F], SWIGLU_LIMIT)
        up = jnp.clip(h[:, F:], -SWIGLU_LIMIT, SWIGLU_LIMIT)
        a = (up + 1.0) * (gate * _sigmoid(SWIGLU_ALPHA * gate))
        y = jnp.dot(a.astype(BF16), wd16_ref[...], preferred_element_type=F32) + bd_ref[...]
        _store_tile_rows(y_ref, y)


def _experts(te, n_used, xs, wgu, bgu, wd, bd, *, tg, n_tiles):
    n_rows, W = xs.shape
    E, D, F2 = wgu.shape
    F = F2 // 2
    row = lambda i, te, nu: (jnp.minimum(i, nu[0] - 1), 0)
    exp3 = lambda i, te, nu: (te[jnp.minimum(i, nu[0] - 1)], 0, 0)
    grid_spec = pltpu.PrefetchScalarGridSpec(
        num_scalar_prefetch=2,
        grid=(n_tiles,),
        in_specs=[pl.BlockSpec((tg * SUBLANES, W), row),
                  pl.BlockSpec((None, D, F2), exp3),
                  pl.BlockSpec((None, 1, F2), exp3),
                  pl.BlockSpec((None, F, D), exp3),
                  pl.BlockSpec((None, 1, D), exp3)],
        out_specs=pl.BlockSpec((tg * SUBLANES, W), row),
        scratch_shapes=[pltpu.VMEM((D, F2), BF16), pltpu.VMEM((F, D), BF16)],
    )
    return pl.pallas_call(
        functools.partial(_experts_body, F=F, tg=tg),
        grid_spec=grid_spec,
        out_shape=jax.ShapeDtypeStruct((n_rows, W), F32),
        compiler_params=_cparams(("arbitrary",)),
        name="moe_experts",
    )(te, n_used, xs, wgu, bgu, wd, bd)


def _combine_body(pos_ref, route_ref, h_ref, y_hbm, o_ref, pos_smem, buf_ref, sem_p, sem_d, *, tm):
    i = pl.program_id(0)
    n = pl.num_programs(0)
    slot = i & 1

    def gather_rows(step, s):
        cp = pltpu.make_async_copy(pos_ref.at[:, pl.ds(pl.multiple_of(step * tm, tm), tm)],
                                   pos_smem.at[s], sem_p)
        cp.start()
        cp.wait()
        return functools.partial(row_copy, s)

    def row_copy(s, t, kk):
        src = pl.multiple_of(pos_smem[s, kk, t], SUBLANES)
        return pltpu.make_async_copy(
            y_hbm.at[pl.ds(src, SUBLANES)],
            buf_ref.at[s, kk, pl.ds(pl.multiple_of(t * SUBLANES, SUBLANES), SUBLANES)], sem_d.at[s])

    @pl.when(i == 0)
    def _():
        _issue_rows(gather_rows(0, 0), tm)

    @pl.when(i + 1 < n)
    def _():
        _issue_rows(gather_rows(i + 1, 1 - slot), tm)

    _drain_rows(functools.partial(row_copy, slot), tm)

    route = route_ref[...]
    lane = lax.broadcasted_iota(I32, (tm, LANES), 1)
    ws = [jnp.sum(jnp.where(lane == TOP_K + kk, route, 0.0), axis=-1, keepdims=True)
          for kk in range(TOP_K)]
    for c in range(SUBLANES):
        sl = slice(c * LANES, (c + 1) * LANES)
        acc = h_ref[:, sl]
        for kk in range(TOP_K):
            acc = acc + ws[kk] * buf_ref[slot, kk, pl.ds(c, tm, stride=SUBLANES), :]
        o_ref[:, sl] = acc


def _combine(pos, route, h1, y, *, tm):
    T, D = h1.shape
    W = y.shape[1]
    return pl.pallas_call(
        functools.partial(_combine_body, tm=tm),
        grid=(T // tm,),
        in_specs=[pl.BlockSpec((SUBLANES, T), lambda i: (0, 0)),
                  pl.BlockSpec((tm, LANES), lambda i: (i, 0)),
                  pl.BlockSpec((tm, D), lambda i: (i, 0)),
                  pl.BlockSpec(memory_space=pl.ANY)],
        out_specs=pl.BlockSpec((tm, D), lambda i: (i, 0)),
        out_shape=jax.ShapeDtypeStruct((T, D), F32),
        scratch_shapes=[pltpu.SMEM((2, SUBLANES, tm), I32),
                        pltpu.VMEM((2, TOP_K, tm * SUBLANES, W), F32),
                        pltpu.SemaphoreType.DMA, pltpu.SemaphoreType.DMA((2,))],
        compiler_params=_cparams(("arbitrary",)),
        name="moe_combine",
    )(pos, route, h1, y)


def _blockdiag_dense(w, width):
    nb, bs, _ = w.shape
    per = width // bs
    wt = w.reshape(nb // per, per, bs, bs)
    eye = jnp.eye(per, dtype=w.dtype)
    dense = jnp.einsum('gpio,pq->gpiqo', wt, eye)
    return dense.reshape(nb // per, width, width)


def _rope_tables(S):
    half = DA_HEAD_DIM // 2
    inv = ROPE_THETA ** (-(jnp.arange(half, dtype=F32) * 2.0 / DA_HEAD_DIM))
    ang = jnp.arange(S, dtype=F32)[:, None] * inv[None, :]
    cos = jnp.tile(jnp.cos(ang), (1, LANES // half))
    sign = jnp.where((jnp.arange(LANES) % DA_HEAD_DIM) < half, -1.0, 1.0).astype(F32)
    sin = jnp.tile(jnp.sin(ang), (1, LANES // half)) * sign[None, :]
    return cos, sin


def _tile(n, pref):
    return pref if n % pref == 0 else n


def _layer(h2, mem2, B, S, lambda_init, attn_norm_g, w_in, b_gate, da_q_norm_g, da_k_norm_g,
           da_lambda_q1, da_lambda_k1, da_lambda_q2, da_lambda_k2, da_subln_g, ml_conv_w, ml_conv_b,
           ml_wq, ml_wk, ml_wv, ml_w_if, ml_b_if, ml_out_norm_g, ml_skip, mem_norm_g, ca_w_kv,
           ca_q_norm_g, ca_k_norm_g, w_branch_da, w_branch_ml, w_branch_ca, w_out, ffn_norm_g,
           router_w, router_b, w_gate_up, b_gate_up, w_down, b_down):
    T, D = h2.shape
    M = mem2.shape[0] // B
    QK = DA_HEADS * 2 * DA_HEAD_DIM
    VW = DA_HEADS * DA_V_DIM
    C = ml_conv_w.shape[1]
    CAW = ca_w_kv.shape[1] // 2
    E = router_w.shape[1]

    o = [0, QK, 2 * QK, 2 * QK + VW, 2 * QK + VW + C, 2 * QK + VW + 2 * C, 2 * QK + VW + 2 * C + CAW]
    w_re = jnp.concatenate([w_in[:, o[0]:o[3]], w_in[:, o[5]:o[6]], w_in[:, o[3]:o[5]], w_in[:, o[6]:]],
                           axis=1).astype(BF16)
    tn = 1024
    col_v, col_caq, col_mlx, col_mlz, col_gate = 2 * QK, 2 * QK + VW, 2 * QK + VW + CAW, \
        2 * QK + VW + CAW + C, 2 * QK + VW + CAW + 2 * C
    proj = _inproj(h2, attn_norm_g[None, :], w_re, tm=_tile(T, 2048), tn=tn)

    g2 = jnp.stack([jnp.tile(da_q_norm_g, LANES // DA_HEAD_DIM),
                    jnp.tile(da_k_norm_g, LANES // DA_HEAD_DIM)])[:, None, :]
    cos_t, sin_t = _rope_tables(S)
    lane_grp = jnp.arange(LANES) // DA_HEAD_DIM
    gm = (lane_grp[:, None] == lane_grp[None, :]).astype(BF16)
    qk = _qkrope(proj, g2, cos_t, sin_t, gm, T=T, S=S, width=QK, tm=_tile(S, 1024))
    y_da = _diff_attention(qk, proj, da_lambda_q1[None, :], da_lambda_k1[None, :], da_lambda_q2[None, :],
                           da_lambda_k2[None, :], da_subln_g[None, :], B=B, S=S, tq=_tile(S, 512),
                           v_col0=col_v // LANES, lambda_init=lambda_init)

    bd = jnp.stack([_blockdiag_dense(ml_wq, MXU_DIM), _blockdiag_dense(ml_wk, MXU_DIM),
                    _blockdiag_dense(ml_wv, MXU_DIM)]).astype(BF16)
    wif = jnp.pad(ml_w_if.reshape(3, C, 2 * ML_HEADS), ((0, 0), (0, 0), (0, LANES - 2 * ML_HEADS))).astype(BF16)
    bif = jnp.pad(ml_b_if, (0, LANES - 2 * ML_HEADS))[None, :]
    xc, mq, mk, mv, gates = _mlpre(proj, ml_conv_w, ml_conv_b[None, :], bd, wif, bif, B=B, S=S, C=C,
                                   tm=_tile(S, 512), x_col=col_mlx // C)
    dml = C // ML_HEADS
    y_ml = _mlstm(mq, mk, mv, gates, xc, proj, ml_out_norm_g[None, :], ml_skip[None, :], B=B, S=S, C=C,
                  L=_tile(S, 256), z_col0=col_mlz // dml)

    kmem, vmem = _memkv(mem2, mem_norm_g[None, :], ca_w_kv.astype(BF16), ca_k_norm_g[None, :], B=B, M=M, W=CAW)
    y_ca = _xattn(proj, kmem, vmem, ca_q_norm_g[None, :], B=B, S=S, M=M, W=CAW, tm=_tile(S, 512),
                  q_col=col_caq // CAW)

    tmx = _tile(T, 512)
    tri = (jnp.arange(tmx)[None, :] < jnp.arange(tmx)[:, None]).astype(BF16)
    rw = jnp.pad(router_w, ((0, 0), (0, LANES - E))).astype(BF16)
    rb = jnp.pad(router_b, (0, LANES - E))[None, :]
    h1, xp, route, cnt = _mix(h2, y_da, y_ml, y_ca, proj, b_gate.reshape(N_BRANCH, 1, D),
                              w_branch_da.astype(BF16), w_branch_ml.astype(BF16),
                              w_branch_ca.astype(BF16), w_out.astype(BF16), ffn_norm_g[None, :], rw, rb, tri,
                              tm=tmx, g_col=col_gate // D, n_exp=E)

    tg = _tile(T, 512)
    n_tiles = (T * TOP_K) // tg + E
    n_rows = n_tiles * tg
    assert D == SUBLANES * LANES, "MoE rows are moved as one (8,128) f32 tile each"
    xs0 = jnp.zeros((n_rows * SUBLANES, LANES), F32)
    pos, te, n_used, xs = _scatter(route, cnt, xp, xs0, tm=_tile(T, 512), tg=tg, n_exp=E, n_tiles=n_tiles)
    y = _experts(te[:, 0], n_used[0, :1], xs, w_gate_up, b_gate_up[:, None, :],
                 w_down, b_down[:, None, :], tg=tg, n_tiles=n_tiles)
    return _combine(pos, route, h1, y, tm=_tile(T, 256))


def kernel(x, mem, attn_norm_g, w_in, b_gate, da_q_norm_g, da_k_norm_g, da_lambda_q1, da_lambda_k1, da_lambda_q2, da_lambda_k2, da_subln_g, ml_conv_w, ml_conv_b, ml_wq, ml_wk, ml_wv, ml_w_if, ml_b_if, ml_out_norm_g, ml_skip, mem_norm_g, ca_w_kv, ca_q_norm_g, ca_k_norm_g, w_branch_da, w_branch_ml, w_branch_ca, w_out, ffn_norm_g, router_w, router_b, w_gate_up, b_gate_up, w_down, b_down):
    B, S, D = x.shape
    depth = w_in.shape[0]
    h2 = x.reshape(B * S, D)
    mem2 = mem.reshape(B * mem.shape[1], D)
    params = (attn_norm_g, w_in, b_gate, da_q_norm_g, da_k_norm_g, da_lambda_q1, da_lambda_k1,
              da_lambda_q2, da_lambda_k2, da_subln_g, ml_conv_w, ml_conv_b, ml_wq, ml_wk, ml_wv, ml_w_if,
              ml_b_if, ml_out_norm_g, ml_skip, mem_norm_g, ca_w_kv, ca_q_norm_g, ca_k_norm_g, w_branch_da,
              w_branch_ml, w_branch_ca, w_out, ffn_norm_g, router_w, router_b, w_gate_up, b_gate_up,
              w_down, b_down)
    for l in range(depth):
        lambda_init = 0.8 - 0.6 * math.exp(-0.3 * l)
        h2 = _layer(h2, mem2, B, S, lambda_init, *[p[l] for p in params])
    return h2.reshape(B, S, D)
```

```python
import functools
import math

import jax
import jax.numpy as jnp
from jax import lax
from jax.experimental import pallas as pl
from jax.experimental.pallas import tpu as pltpu

F32 = jnp.float32
BF16 = jnp.bfloat16
I32 = jnp.int32
U32 = jnp.uint32

NORM_EPS = 1e-6
ROPE_THETA = 10000.0
CHUNK = 64

DA_HEADS = 8
DA_HEAD_DIM = 64
DA_V_DIM = 128
ML_HEADS = 4
ML_CONV = 4
ML_QKV_BLOCK = 4
CA_HEADS = 4
N_BRANCH = 3
TOP_K = 4
SWIGLU_LIMIT = 7.0
SWIGLU_ALPHA = 1.702

LANES = 128
SUBLANES = 8
MXU_DIM = 256
VMEM_LIMIT = 56 * 1024 * 1024
NEG_BIG = -1e30
DA_Q_SCALE = DA_HEAD_DIM ** -0.5 * math.log2(math.e)
DA_BOUND_SLACK = 1.01
DA_BOUND_LIMIT = 40.0


def _cparams(sem):
    return pltpu.CompilerParams(dimension_semantics=sem, vmem_limit_bytes=VMEM_LIMIT)


def _sigmoid(x):
    return 0.5 * jnp.tanh(0.5 * x) + 0.5


def _silu(x):
    return x * _sigmoid(x)


def _inproj_body(x_ref, g_ref, w_ref, o_ref, xn_ref):
    j = pl.program_id(1)

    @pl.when(j == 0)
    def _():
        x = x_ref[...]
        ms = jnp.mean(x * x, axis=-1, keepdims=True)
        xn_ref[...] = (x * lax.rsqrt(ms + NORM_EPS) * g_ref[...]).astype(BF16)

    o_ref[...] = jnp.dot(xn_ref[...], w_ref[...], preferred_element_type=F32).astype(o_ref.dtype)


def _inproj(x2, g, w_bf, *, tm, tn):
    T, D = x2.shape
    N = w_bf.shape[1]
    return pl.pallas_call(
        _inproj_body,
        grid=(T // tm, N // tn),
        in_specs=[
            pl.BlockSpec((tm, D), lambda i, j: (i, 0)),
            pl.BlockSpec((1, D), lambda i, j: (0, 0)),
            pl.BlockSpec((D, tn), lambda i, j: (0, j)),
        ],
        out_specs=pl.BlockSpec((tm, tn), lambda i, j: (i, j)),
        out_shape=jax.ShapeDtypeStruct((T, N), BF16),
        scratch_shapes=[pltpu.VMEM((tm, D), BF16)],
        compiler_params=_cparams(("arbitrary", "arbitrary")),
        name="inproj",
    )(x2, g, w_bf)


def _group_sumsq(xb, gm):
    sq = xb * xb
    hi = sq.astype(BF16)
    lo = (sq - hi.astype(F32)).astype(BF16)
    return (jnp.dot(hi, gm, preferred_element_type=F32)
            + jnp.dot(lo, gm, preferred_element_type=F32))


def _qkrope_body(x_ref, g_ref, cos_ref, sin_ref, gm_ref, o_ref, *, ncb, scale):
    c = pl.program_id(1)
    g = g_ref[...] * jnp.where(c == 0, scale, 1.0).astype(F32)
    cos = cos_ref[...]
    sin = sin_ref[...]
    gm = gm_ref[...]
    lane = lax.broadcasted_iota(I32, (1, LANES), 1)
    first_half = (lane & (DA_HEAD_DIM - 1)) < (DA_HEAD_DIM // 2)
    for cb in range(ncb):
        sl = slice(cb * LANES, (cb + 1) * LANES)
        xb = x_ref[:, sl].astype(F32)
        ss = _group_sumsq(xb, gm)
        y = xb * lax.rsqrt(ss * (1.0 / DA_HEAD_DIM) + NORM_EPS) * g
        sw = jnp.where(first_half, pltpu.roll(y, LANES - DA_HEAD_DIM // 2, 1),
                       pltpu.roll(y, DA_HEAD_DIM // 2, 1))
        o_ref[:, sl] = (y * cos + sw * sin).astype(o_ref.dtype)


def _qkrope(proj, g2, cos_t, sin_t, gm, *, T, S, width, tm):
    n_s = S // tm
    return pl.pallas_call(
        functools.partial(_qkrope_body, ncb=width // LANES, scale=DA_Q_SCALE),
        grid=(T // tm, 2),
        in_specs=[
            pl.BlockSpec((tm, width), lambda i, c: (i, c)),
            pl.BlockSpec((None, 1, LANES), lambda i, c: (c, 0, 0)),
            pl.BlockSpec((tm, LANES), lambda i, c: (i % n_s, 0)),
            pl.BlockSpec((tm, LANES), lambda i, c: (i % n_s, 0)),
            pl.BlockSpec((LANES, LANES), lambda i, c: (0, 0)),
        ],
        out_specs=pl.BlockSpec((tm, width), lambda i, c: (i, c)),
        out_shape=jax.ShapeDtypeStruct((T, 2 * width), BF16),
        compiler_params=_cparams(("arbitrary", "arbitrary")),
        name="qkrope",
    )(proj, g2, cos_t, sin_t, gm)


def _da_body(q_ref, k_ref, v_ref, lq1_ref, lk1_ref, lq2_ref, lk2_ref, subg_ref, qg_ref, kg_ref, o_ref,
             m_ref, acc_ref, kt_ref, *, tq, n_kv, q_scale, lambda_init):
    i = pl.program_id(2)
    q = q_ref[...]
    lane = lax.broadcasted_iota(I32, (1, LANES), 1)
    lo = lane < DA_HEAD_DIM
    zero = jnp.zeros_like(q)
    qc = (jnp.where(lo, q, zero), jnp.where(lo, zero, q))

    @pl.when(i == 0)
    def _():
        def tbody(j, carry):
            start = pl.multiple_of(j * tq, tq)
            kt_ref[:, pl.ds(start, tq)] = k_ref[pl.ds(start, tq), :].astype(F32).T.astype(BF16)
            return carry
        lax.fori_loop(0, n_kv, tbody, 0)

    bound = (DA_HEAD_DIM * q_scale * DA_BOUND_SLACK
             * jnp.max(jnp.abs(qg_ref[...]), keepdims=True) * jnp.max(jnp.abs(kg_ref[...]), keepdims=True))
    bound_max = jnp.max(bound)

    acc_ref[...] = jnp.zeros(acc_ref.shape, F32)
    ones = jnp.ones((tq, LANES), BF16)

    def scores(j):
        ktj = kt_ref[:, pl.ds(pl.multiple_of(j * tq, tq), tq)]
        return [jnp.dot(qc[c], ktj, preferred_element_type=F32) for c in range(2)]

    def diag_mask():
        r = lax.broadcasted_iota(I32, (tq, tq), 0)
        cidx = lax.broadcasted_iota(I32, (tq, tq), 1)
        return (cidx | (CHUNK - 1)) <= (r | (CHUNK - 1))

    def v_aug(j):
        return jnp.concatenate([v_ref[pl.ds(pl.multiple_of(j * tq, tq), tq), :], ones], axis=1)

    def consume_bounded(j, ss, masked):
        vj = v_aug(j)
        ps = []
        for c in range(2):
            p = jnp.exp2(ss[c] - bound)
            if masked:
                p = jnp.where(diag_mask(), p, 0.0)
            ps.append(p.astype(BF16))
        pvs = [jnp.dot(ps[c], vj, preferred_element_type=F32) for c in range(2)]
        for c in range(2):
            acc_ref[c] = acc_ref[c] + pvs[c]

    def consume_online(j, ss, masked):
        vj = v_aug(j)
        ps, alphas = [], []
        for c in range(2):
            s = jnp.where(diag_mask(), ss[c], NEG_BIG) if masked else ss[c]
            m_old = m_ref[c]
            m_new = jnp.maximum(m_old, jnp.max(s, axis=-1, keepdims=True))
            ps.append(jnp.exp2(s - jnp.concatenate([m_new] * (tq // LANES), axis=1)).astype(BF16))
            alphas.append(jnp.exp2(m_old - m_new))
            m_ref[c] = m_new
        pvs = [jnp.dot(ps[c], vj, preferred_element_type=F32) for c in range(2)]
        for c in range(2):
            acc_ref[c] = jnp.concatenate([alphas[c], alphas[c]], axis=1) * acc_ref[c] + pvs[c]

    def sweep(consume):
        def pair(jj, carry):
            sa = scores(2 * jj)
            sb = scores(2 * jj + 1)
            consume(2 * jj, sa, False)
            consume(2 * jj + 1, sb, False)
            return carry

        lax.fori_loop(0, lax.shift_right_logical(i, 1), pair, 0)

        @pl.when((i & 1) == 1)
        def _():
            consume(i - 1, scores(i - 1), False)

        consume(i, scores(i), True)

    @pl.when(bound_max < DA_BOUND_LIMIT)
    def _():
        sweep(consume_bounded)

    @pl.when(jnp.logical_not(bound_max < DA_BOUND_LIMIT))
    def _():
        m_ref[...] = jnp.full(m_ref.shape, NEG_BIG, F32)
        sweep(consume_online)

    f32 = F32
    lam = (jnp.exp(jnp.sum(lq1_ref[...].astype(f32) * lk1_ref[...].astype(f32), keepdims=True))
           - jnp.exp(jnp.sum(lq2_ref[...].astype(f32) * lk2_ref[...].astype(f32), keepdims=True))
           + lambda_init)
    a1 = acc_ref[0]
    a2 = acc_ref[1]
    o = a1[:, :DA_V_DIM] / a1[:, DA_V_DIM:] - lam * (a2[:, :DA_V_DIM] / a2[:, DA_V_DIM:])
    ms = jnp.mean(o * o, axis=-1, keepdims=True)
    o = o * lax.rsqrt(ms + NORM_EPS) * subg_ref[...]
    o_ref[...] = (o * (1.0 - lambda_init)).astype(o_ref.dtype)


def _diff_attention(qk, proj, lq1, lk1, lq2, lk2, subg, qg, kg, *, B, S, tq, v_col0, q_scale, lambda_init):
    T = B * S
    H = DA_HEADS
    n_q = S // tq
    vec = lambda n: pl.BlockSpec((1, n), lambda b, h, i: (0, 0))
    return pl.pallas_call(
        functools.partial(_da_body, tq=tq, n_kv=n_q, q_scale=q_scale, lambda_init=lambda_init),
        grid=(B, H, n_q),
        in_specs=[
            pl.BlockSpec((tq, LANES), lambda b, h, i: (b * n_q + i, h)),
            pl.BlockSpec((S, LANES), lambda b, h, i: (b, H + h)),
            pl.BlockSpec((S, LANES), lambda b, h, i: (b, v_col0 + h)),
            vec(DA_HEAD_DIM), vec(DA_HEAD_DIM), vec(DA_HEAD_DIM), vec(DA_HEAD_DIM),
            vec(DA_V_DIM), vec(DA_HEAD_DIM), vec(DA_HEAD_DIM),
        ],
        out_specs=pl.BlockSpec((tq, LANES), lambda b, h, i: (b * n_q + i, h)),
        out_shape=jax.ShapeDtypeStruct((T, H * DA_V_DIM), BF16),
        scratch_shapes=[pltpu.VMEM((2, tq, LANES), F32), pltpu.VMEM((2, tq, 2 * DA_V_DIM), F32),
                        pltpu.VMEM((LANES, S), BF16)],
        compiler_params=_cparams(("arbitrary", "arbitrary", "arbitrary")),
        name="diff_attn",
    )(qk, qk, proj, lq1, lk1, lq2, lk2, subg, qg, kg)


def _mlpre_body(x_ref, cw_ref, cb_ref, bd_ref, wif_ref, bif_ref,
                xc_ref, q_ref, k_ref, v_ref, g_ref, prev_ref, *, tm, ncb, kscale):
    s_idx = pl.program_id(1)

    @pl.when(s_idx == 0)
    def _():
        prev_ref[...] = jnp.zeros(prev_ref.shape, F32)

    row8 = lax.broadcasted_iota(I32, (SUBLANES, MXU_DIM), 0)
    gacc = jnp.zeros((tm, LANES), F32) + bif_ref[...]
    for cb in range(ncb):
        sl = slice(cb * MXU_DIM, (cb + 1) * MXU_DIM)
        xb16 = x_ref[:, sl]
        x = xb16.astype(F32)
        prev = prev_ref[:, sl]
        conv = x * cw_ref[ML_CONV - 1:ML_CONV, sl] + cb_ref[:, sl]
        for d in range(1, ML_CONV):
            xs = pltpu.roll(x, d, 0)
            ps = pltpu.roll(prev, d, 0)
            head = jnp.where(row8 < d, ps, xs[:SUBLANES])
            shifted = jnp.concatenate([head, xs[SUBLANES:]], axis=0)
            conv = conv + shifted * cw_ref[ML_CONV - 1 - d:ML_CONV - d, sl]
        prev_ref[:, sl] = x[tm - SUBLANES:]
        xc = _silu(conv)
        xc16 = xc.astype(BF16)
        q = jnp.dot(xc16, bd_ref[0, cb], preferred_element_type=F32)
        k = jnp.dot(xc16, bd_ref[1, cb], preferred_element_type=F32)
        v = jnp.dot(xb16, bd_ref[2, cb], preferred_element_type=F32)
        q16, k16, v16 = q.astype(BF16), k.astype(BF16), v.astype(BF16)
        gacc = gacc + jnp.dot(q16, wif_ref[0, sl, :], preferred_element_type=F32)
        gacc = gacc + jnp.dot(k16, wif_ref[1, sl, :], preferred_element_type=F32)
        gacc = gacc + jnp.dot(v16, wif_ref[2, sl, :], preferred_element_type=F32)
        xc_ref[:, sl] = xc16
        q_ref[:, sl] = q16
        k_ref[:, sl] = (k * kscale).astype(BF16)
        v_ref[:, sl] = v16
    g_ref[...] = gacc


def _mlpre(proj, conv_w, conv_b, bd, wif, bif, *, B, S, C, tm, x_col):
    T = B * S
    n_s = S // tm
    row = lambda i: pl.BlockSpec((tm, C), lambda b, s: (b * n_s + s, i))
    outs = pl.pallas_call(
        functools.partial(_mlpre_body, tm=tm, ncb=C // MXU_DIM, kscale=(C // ML_HEADS) ** -0.5),
        grid=(B, n_s),
        in_specs=[
            row(x_col),
            pl.BlockSpec((ML_CONV, C), lambda b, s: (0, 0)),
            pl.BlockSpec((1, C), lambda b, s: (0, 0)),
            pl.BlockSpec(bd.shape, lambda b, s: (0, 0, 0, 0)),
            pl.BlockSpec(wif.shape, lambda b, s: (0, 0, 0)),
            pl.BlockSpec((1, LANES), lambda b, s: (0, 0)),
        ],
        out_specs=[row(0), row(0), row(0), row(0),
                   pl.BlockSpec((tm, LANES), lambda b, s: (b * n_s + s, 0))],
        out_shape=[jax.ShapeDtypeStruct((T, C), BF16)] * 4 + [jax.ShapeDtypeStruct((T, LANES), F32)],
        scratch_shapes=[pltpu.VMEM((SUBLANES, C), F32)],
        compiler_params=_cparams(("arbitrary", "arbitrary")),
        name="mlstm_pre",
    )(proj, conv_w, conv_b, bd, wif, bif)
    return outs


def _mlstm_body(q_ref, k_ref, v_ref, g_ref, xc_ref, z_ref, ng_ref, skip_ref, o_ref,
                c_ref, cb_ref, n_ref, m_ref, *, L, d, hpb):
    hg = pl.program_id(1)
    c_idx = pl.program_id(2)

    @pl.when(c_idx == 0)
    def _():
        c_ref[...] = jnp.zeros(c_ref.shape, F32)
        cb_ref[...] = jnp.zeros(cb_ref.shape, BF16)
        n_ref[...] = jnp.zeros(n_ref.shape, F32)
        m_ref[...] = jnp.zeros(m_ref.shape, F32)

    g = g_ref[...]
    gt = g.T
    lane = lax.broadcasted_iota(I32, (1, LANES), 1)
    sub = lax.broadcasted_iota(I32, (LANES, 1), 0)
    t_idx = lax.broadcasted_iota(I32, (L, L), 0)
    s_idx = lax.broadcasted_iota(I32, (L, L), 1)
    tri = s_idx <= t_idx

    for hh in range(hpb):
        h = hg * hpb + hh
        sl = slice(hh * d, (hh + 1) * d)
        i_col = jnp.sum(jnp.where(lane == h, g, 0.0), axis=1, keepdims=True)
        f_col = jnp.sum(jnp.where(lane == ML_HEADS + h, g, 0.0), axis=1, keepdims=True)
        i_row = jnp.sum(jnp.where(sub == h, gt, 0.0), axis=0, keepdims=True)
        f_row = jnp.sum(jnp.where(sub == ML_HEADS + h, gt, 0.0), axis=0, keepdims=True)
        lf_col = jax.nn.log_sigmoid(f_col)
        lf_row = jax.nn.log_sigmoid(f_row)

        b_col = jnp.sum(jnp.where(tri, lf_row, 0.0), axis=1, keepdims=True)
        b_row = jnp.sum(jnp.where(t_idx <= s_idx, lf_col, 0.0), axis=0, keepdims=True)
        b_last = jnp.sum(lf_row, axis=1, keepdims=True)

        m_prev = m_ref[hh]
        dlog = jnp.where(tri, b_col - b_row + i_row, NEG_BIG)
        inter_log = b_col + m_prev
        m_rowv = jnp.maximum(inter_log, jnp.max(dlog, axis=1, keepdims=True))
        dw = jnp.exp(dlog - m_rowv)
        inter_w = jnp.exp(inter_log - m_rowv)

        q = q_ref[:, sl]
        k = k_ref[:, sl]
        v = v_ref[:, sl]
        qk = lax.dot_general(q, k, (((1,), (1,)), ((), ())), preferred_element_type=F32)
        qc = jnp.dot(q, cb_ref[hh], preferred_element_type=F32)
        s = qk * dw

        state_row = b_last - b_row + i_row
        state_col = b_last - b_col + i_col
        m_new = jnp.maximum(b_last + m_prev, jnp.max(state_row, axis=1, keepdims=True))
        decay = jnp.exp(b_last + m_prev - m_new)
        ws_col = jnp.exp(state_col - m_new)
        kwf = k.astype(F32) * ws_col

        num = jnp.dot(s.astype(BF16), v, preferred_element_type=F32) + inter_w * qc
        dc = lax.dot_general(kwf.astype(BF16), v, (((0,), (0,)), ((), ())), preferred_element_type=F32)
        den = (jnp.sum(s, axis=1, keepdims=True)
               + inter_w * jnp.sum(q.astype(F32) * n_ref[hh], axis=1, keepdims=True))
        hv = num / jnp.maximum(jnp.abs(den), jnp.exp(-m_rowv))

        c_new = decay * c_ref[hh] + dc
        c_ref[hh] = c_new
        cb_ref[hh] = c_new.astype(BF16)
        n_ref[hh] = decay * n_ref[hh] + jnp.sum(kwf, axis=0, keepdims=True)
        m_ref[hh] = m_new

        mu = jnp.mean(hv, axis=-1, keepdims=True)
        cen = hv - mu
        var = jnp.mean(cen * cen, axis=-1, keepdims=True)
        hn = cen * lax.rsqrt(var + NORM_EPS) * ng_ref[:, sl]
        y = (hn + skip_ref[:, sl] * xc_ref[:, sl].astype(F32)) * _silu(z_ref[:, sl].astype(F32))
        o_ref[:, sl] = y.astype(o_ref.dtype)


def _mlstm(q, k, v, gates, xc, proj, norm_g, skip, *, B, S, C, L, z_col0, hpb):
    T = B * S
    H = ML_HEADS
    d = C // H
    w = hpb * d
    nc = S // L
    blk = lambda off: pl.BlockSpec((L, w), lambda b, h, c: (b * nc + c, off + h))
    par = pl.BlockSpec((1, w), lambda b, h, c: (0, h))
    return pl.pallas_call(
        functools.partial(_mlstm_body, L=L, d=d, hpb=hpb),
        grid=(B, H // hpb, nc),
        in_specs=[blk(0), blk(0), blk(0),
                  pl.BlockSpec((L, LANES), lambda b, h, c: (b * nc + c, 0)),
                  blk(0), blk(z_col0 // hpb), par, par],
        out_specs=blk(0),
        out_shape=jax.ShapeDtypeStruct((T, C), BF16),
        scratch_shapes=[pltpu.VMEM((hpb, d, d), F32), pltpu.VMEM((hpb, d, d), BF16),
                        pltpu.VMEM((hpb, 1, d), F32), pltpu.VMEM((hpb, 1, 1), F32)],
        compiler_params=_cparams(("arbitrary", "arbitrary", "arbitrary")),
        name="mlstm_scan",
    )(q, k, v, gates, xc, proj, norm_g, skip)


def _memkv_body(mem_ref, mg_ref, w_ref, kg_ref, k_ref, v_ref, *, W, dh):
    x = mem_ref[...]
    ms = jnp.mean(x * x, axis=-1, keepdims=True)
    xn = (x * lax.rsqrt(ms + NORM_EPS) * mg_ref[...]).astype(BF16)
    kv = jnp.dot(xn, w_ref[...], preferred_element_type=F32)
    for hd in range(W // dh):
        sl = slice(hd * dh, (hd + 1) * dh)
        kh = kv[:, sl]
        msk = jnp.mean(kh * kh, axis=-1, keepdims=True)
        k_ref[:, sl] = (kh * lax.rsqrt(msk + NORM_EPS) * kg_ref[...]).astype(BF16)
    v_ref[...] = kv[:, W:].astype(BF16)


def _memkv(mem2, mem_g, w_kv_bf, k_g, *, B, M, W):
    D = mem2.shape[1]
    dh = W // CA_HEADS
    return pl.pallas_call(
        functools.partial(_memkv_body, W=W, dh=dh),
        grid=(B,),
        in_specs=[pl.BlockSpec((M, D), lambda b: (b, 0)),
                  pl.BlockSpec((1, D), lambda b: (0, 0)),
                  pl.BlockSpec((D, 2 * W), lambda b: (0, 0)),
                  pl.BlockSpec((1, dh), lambda b: (0, 0))],
        out_specs=[pl.BlockSpec((M, W), lambda b: (b, 0))] * 2,
        out_shape=[jax.ShapeDtypeStruct((B * M, W), BF16)] * 2,
        compiler_params=_cparams(("arbitrary",)),
        name="mem_kv",
    )(mem2, mem_g, w_kv_bf, k_g)


def _xattn_body(q_ref, k_ref, v_ref, qg_ref, o_ref, *, W, dh):
    scale = dh ** -0.5
    for hd in range(W // dh):
        sl = slice(hd * dh, (hd + 1) * dh)
        qh = q_ref[:, sl].astype(F32)
        ms = jnp.mean(qh * qh, axis=-1, keepdims=True)
        qn = (qh * lax.rsqrt(ms + NORM_EPS) * (qg_ref[...] * scale)).astype(BF16)
        s = lax.dot_general(qn, k_ref[:, sl], (((1,), (1,)), ((), ())), preferred_element_type=F32)
        mx = jnp.max(s, axis=-1, keepdims=True)
        p = jnp.exp(s - mx)
        p = p / jnp.sum(p, axis=-1, keepdims=True)
        o_ref[:, sl] = jnp.dot(p.astype(BF16), v_ref[:, sl],
                               preferred_element_type=F32).astype(o_ref.dtype)


def _xattn(proj, kmem, vmem, q_g, *, B, S, M, W, tm, q_col):
    T = B * S
    n_s = S // tm
    dh = W // CA_HEADS
    return pl.pallas_call(
        functools.partial(_xattn_body, W=W, dh=dh),
        grid=(T // tm,),
        in_specs=[pl.BlockSpec((tm, W), lambda i: (i, q_col)),
                  pl.BlockSpec((M, W), lambda i: (i // n_s, 0)),
                  pl.BlockSpec((M, W), lambda i: (i // n_s, 0)),
                  pl.BlockSpec((1, dh), lambda i: (0, 0))],
        out_specs=pl.BlockSpec((tm, W), lambda i: (i, 0)),
        out_shape=jax.ShapeDtypeStruct((T, W), BF16),
        compiler_params=_cparams(("arbitrary",)),
        name="mem_xattn",
    )(proj, kmem, vmem, q_g)


def _store_tile_rows(ref, x):
    n = x.shape[0]
    for c in range(SUBLANES):
        ref[pl.ds(c, n, stride=SUBLANES), :] = x[:, c * LANES:(c + 1) * LANES]


def _load_tile_rows(ref, n):
    return [ref[pl.ds(c, n, stride=SUBLANES), :] for c in range(SUBLANES)]


def _mix_body(x_ref, yda_ref, yml_ref, yca_ref, gda_ref, gml_ref, gca_ref, bg_ref, wda_ref, wml_ref, wca_ref,
              wout_ref, fg_ref, rw_ref, rb_ref, tri_ref,
              h_ref, xp_ref, route_ref, cnt_ref, carry_ref, *, n_exp):
    i = pl.program_id(0)

    @pl.when(i == 0)
    def _():
        carry_ref[...] = jnp.zeros(carry_ref.shape, F32)

    def gate(g_ref, b):
        return _sigmoid(g_ref[...].astype(F32) + bg_ref[b])

    mix = (gate(gda_ref, 0) * jnp.dot(yda_ref[...], wda_ref[...], preferred_element_type=F32)
           + gate(gml_ref, 1) * jnp.dot(yml_ref[...], wml_ref[...], preferred_element_type=F32)
           + gate(gca_ref, 2) * jnp.dot(yca_ref[...], wca_ref[...], preferred_element_type=F32))
    h1 = x_ref[...] + jnp.dot(mix.astype(BF16), wout_ref[...], preferred_element_type=F32)
    h_ref[...] = h1
    ms = jnp.mean(h1 * h1, axis=-1, keepdims=True)
    xn = h1 * lax.rsqrt(ms + NORM_EPS) * fg_ref[...]
    _store_tile_rows(xp_ref, xn)

    logits = jnp.dot(xn.astype(BF16), rw_ref[...], preferred_element_type=F32) + rb_ref[...]
    tm = logits.shape[0]
    lane = lax.broadcasted_iota(I32, (tm, LANES), 1)
    work = jnp.where(lane < n_exp, logits, NEG_BIG)
    sel = jnp.zeros((tm, LANES), F32)
    vals, idxs = [], []
    for _ in range(TOP_K):
        mx = jnp.max(work, axis=-1, keepdims=True)
        idx = jnp.min(jnp.where(work == mx, lane, LANES), axis=-1, keepdims=True)
        hit = lane == idx
        sel = jnp.where(hit, 1.0, sel)
        work = jnp.where(hit, NEG_BIG, work)
        vals.append(mx)
        idxs.append(idx)
    exps = [jnp.exp(v - vals[0]) for v in vals]
    tot = exps[0] + exps[1] + exps[2] + exps[3]

    cum = jnp.dot(tri_ref[...], sel.astype(BF16), preferred_element_type=F32) + carry_ref[...]
    route = jnp.zeros((tm, LANES), F32)
    for kk in range(TOP_K):
        rank = jnp.sum(jnp.where(lane == idxs[kk], cum, 0.0), axis=-1, keepdims=True)
        route = jnp.where(lane == kk, idxs[kk].astype(F32), route)
        route = jnp.where(lane == TOP_K + kk, exps[kk] / tot, route)
        route = jnp.where(lane == 2 * TOP_K + kk, rank, route)
    route_ref[...] = route
    carry_ref[...] = carry_ref[...] + jnp.sum(sel, axis=0, keepdims=True)
    cnt_ref[...] = carry_ref[...]


def _mix(x2, yda, yml, yca, proj, bg, wda, wml, wca, wout, fg, rw, rb, tri, *, tm, g_col, n_exp):
    T, D = x2.shape
    C = yml.shape[1]
    const = lambda shape: pl.BlockSpec(shape, lambda i: (0,) * len(shape))
    return pl.pallas_call(
        functools.partial(_mix_body, n_exp=n_exp),
        grid=(T // tm,),
        in_specs=[pl.BlockSpec((tm, D), lambda i: (i, 0)),
                  pl.BlockSpec((tm, D), lambda i: (i, 0)),
                  pl.BlockSpec((tm, C), lambda i: (i, 0)),
                  pl.BlockSpec((tm, D), lambda i: (i, 0)),
                  pl.BlockSpec((tm, D), lambda i: (i, g_col)),
                  pl.BlockSpec((tm, D), lambda i: (i, g_col + 1)),
                  pl.BlockSpec((tm, D), lambda i: (i, g_col + 2)),
                  const((N_BRANCH, 1, D)),
                  const((D, D)), const((C, D)), const((D, D)), const((D, D)),
                  const((1, D)), const((D, LANES)), const((1, LANES)), const((tm, tm))],
        out_specs=[pl.BlockSpec((tm, D), lambda i: (i, 0)),
                   pl.BlockSpec((tm * SUBLANES, LANES), lambda i: (i, 0)),
                   pl.BlockSpec((tm, LANES), lambda i: (i, 0)),
                   pl.BlockSpec((1, LANES), lambda i: (0, 0))],
        out_shape=[jax.ShapeDtypeStruct((T, D), F32),
                   jax.ShapeDtypeStruct((T * SUBLANES, LANES), F32),
                   jax.ShapeDtypeStruct((T, LANES), F32),
                   jax.ShapeDtypeStruct((1, LANES), F32)],
        scratch_shapes=[pltpu.VMEM((1, LANES), F32)],
        compiler_params=_cparams(("arbitrary",)),
        name="mix_route",
    )(x2, yda, yml, yca, proj, proj, proj, bg, wda, wml, wca, wout, fg, rw, rb, tri)


ROW_DMA_UNROLL = 8


def _issue_rows(row_copy, tm):
    def issue(t, carry):
        for kk in range(TOP_K):
            row_copy(t, kk).start(priority=kk % 2)
        return carry

    lax.fori_loop(0, tm, issue, 0, unroll=ROW_DMA_UNROLL)


def _drain_rows(row_copy, tm):
    def drain(t, carry):
        for kk in range(TOP_K):
            row_copy(t, kk).wait()
        return carry

    lax.fori_loop(0, tm, drain, 0, unroll=ROW_DMA_UNROLL)


def _issue_and_drain_rows(row_copy, tm):
    _issue_rows(row_copy, tm)
    _drain_rows(row_copy, tm)


def _group_starts(cnt, tg, n_exp):
    lane_r = lax.broadcasted_iota(I32, (LANES, LANES), 0)
    lane_c = lax.broadcasted_iota(I32, (LANES, LANES), 1)
    padded = jnp.ceil(cnt * (1.0 / tg)) * tg
    padded_col = jnp.sum(jnp.where(lane_r == lane_c, padded, 0.0), axis=1, keepdims=True)
    start = jnp.sum(jnp.where(lane_r < lane_c, padded_col, 0.0), axis=0, keepdims=True)
    return start, start + padded


def _scatter_body(route_ref, route_next_ref, cnt_ref, xp_ref, xs_in_hbm, pos_ref, te_ref, nu_ref, xs_hbm,
                  posv_ref, pos_smem, sem_p, sem_d, *, tm, tg, n_exp, n_tiles):
    del xs_in_hbm
    i = pl.program_id(0)
    n = pl.num_programs(0)
    slot = i & 1
    lane = lax.broadcasted_iota(I32, (tm, LANES), 1)
    start, end = _group_starts(cnt_ref[...], tg, n_exp)

    def stage_positions(route, s):
        posm = jnp.zeros((tm, LANES), F32)
        for kk in range(TOP_K):
            e = jnp.sum(jnp.where(lane == kk, route, 0.0), axis=-1, keepdims=True)
            rank = jnp.sum(jnp.where(lane == 2 * TOP_K + kk, route, 0.0), axis=-1, keepdims=True)
            st = jnp.sum(jnp.where(lane == e.astype(I32), start, 0.0), axis=-1, keepdims=True)
            posm = jnp.where(lane == kk, (st + rank) * SUBLANES, posm)
        posv_ref[s] = posm.astype(I32).T[:SUBLANES]
        cp = pltpu.make_async_copy(posv_ref.at[s], pos_smem.at[s], sem_p)
        cp.start()
        cp.wait()

    @pl.when(i == 0)
    def _():
        stage_positions(route_ref[...], 0)
        tstart = (lax.broadcasted_iota(I32, (n_tiles, LANES), 0) * tg).astype(F32)
        lane_t = lax.broadcasted_iota(I32, (n_tiles, LANES), 1)
        done = jnp.where((lane_t < n_exp) & (end <= tstart), 1.0, 0.0)
        te = jnp.sum(done, axis=-1, keepdims=True)
        te_ref[...] = jnp.broadcast_to(te, (n_tiles, LANES)).astype(I32)
        n_used = jnp.max(end, axis=-1, keepdims=True) * (1.0 / tg)
        nu_ref[...] = jnp.broadcast_to(n_used, (1, LANES)).astype(I32)

    def row_copy(t, kk):
        dst = pl.multiple_of(pos_smem[slot, kk, t], SUBLANES)
        return pltpu.make_async_copy(xp_ref.at[pl.ds(pl.multiple_of(t * SUBLANES, SUBLANES), SUBLANES)],
                                     xs_hbm.at[pl.ds(dst, SUBLANES)], sem_d)

    _issue_rows(row_copy, tm)

    @pl.when(i + 1 < n)
    def _():
        stage_positions(route_next_ref[...], 1 - slot)

    pos_ref[...] = posv_ref[slot]
    _drain_rows(row_copy, tm)


def _scatter(route, cnt, xp, xs_init, *, tm, tg, n_exp, n_tiles):
    T = route.shape[0]
    n_rows, W = xs_init.shape
    n_steps = T // tm
    return pl.pallas_call(
        functools.partial(_scatter_body, tm=tm, tg=tg, n_exp=n_exp, n_tiles=n_tiles),
        grid=(n_steps,),
        in_specs=[pl.BlockSpec((tm, LANES), lambda i: (i, 0)),
                  pl.BlockSpec((tm, LANES), lambda i: (jnp.minimum(i + 1, n_steps - 1), 0)),
                  pl.BlockSpec((1, LANES), lambda i: (0, 0)),
                  pl.BlockSpec((tm * SUBLANES, W), lambda i: (i, 0)),
                  pl.BlockSpec(memory_space=pl.ANY)],
        out_specs=[pl.BlockSpec((SUBLANES, tm), lambda i: (0, i)),
                   pl.BlockSpec((n_tiles, LANES), lambda i: (0, 0)),
                   pl.BlockSpec((1, LANES), lambda i: (0, 0)),
                   pl.BlockSpec(memory_space=pl.ANY)],
        out_shape=[jax.ShapeDtypeStruct((SUBLANES, T), I32),
                   jax.ShapeDtypeStruct((n_tiles, LANES), I32),
                   jax.ShapeDtypeStruct((1, LANES), I32),
                   jax.ShapeDtypeStruct((n_rows, W), F32)],
        scratch_shapes=[pltpu.VMEM((2, SUBLANES, tm), I32), pltpu.SMEM((2, SUBLANES, tm), I32),
                        pltpu.SemaphoreType.DMA, pltpu.SemaphoreType.DMA],
        input_output_aliases={4: 3},
        compiler_params=_cparams(("arbitrary",)),
        name="moe_scatter",
    )(route, route, cnt, xp, xs_init)


def _experts_body(te_ref, nu_ref, xs_ref, wgu_ref, bgu_ref, wd_ref, bd_ref, y_ref, wgu16_ref, wd16_ref,
                  *, F, tg):
    i = pl.program_id(0)
    live = i < nu_ref[0]
    new_expert = (i == 0) | (te_ref[i] != te_ref[jnp.maximum(i - 1, 0)])

    @pl.when(live & new_expert)
    def _():
        wgu16_ref[...] = wgu_ref[...].astype(BF16)
        wd16_ref[...] = wd_ref[...].astype(BF16)

    @pl.when(live)
    def _():
        x = jnp.concatenate([c.astype(BF16) for c in _load_tile_rows(xs_ref, tg)], axis=1)
        h = jnp.dot(x, wgu16_ref[...], preferred_element_type=F32) + bgu_ref[...]
        gate = jnp.minimum(h[:, :F], SWIGLU_LIMIT)
        up = jnp.clip(h[:, F:], -SWIGLU_LIMIT, SWIGLU_LIMIT)
        a = (up + 1.0) * (gate * _sigmoid(SWIGLU_ALPHA * gate))
        y = jnp.dot(a.astype(BF16), wd16_ref[...], preferred_element_type=F32) + bd_ref[...]
        _store_tile_rows(y_ref, y)


def _experts(te, n_used, xs, wgu, bgu, wd, bd, *, tg, n_tiles):
    n_rows, W = xs.shape
    E, D, F2 = wgu.shape
    F = F2 // 2
    row = lambda i, te, nu: (jnp.minimum(i, nu[0] - 1), 0)
    exp3 = lambda i, te, nu: (te[jnp.minimum(i, nu[0] - 1)], 0, 0)
    grid_spec = pltpu.PrefetchScalarGridSpec(
        num_scalar_prefetch=2,
        grid=(n_tiles,),
        in_specs=[pl.BlockSpec((tg * SUBLANES, W), row),
                  pl.BlockSpec((None, D, F2), exp3),
                  pl.BlockSpec((None, 1, F2), exp3),
                  pl.BlockSpec((None, F, D), exp3),
                  pl.BlockSpec((None, 1, D), exp3)],
        out_specs=pl.BlockSpec((tg * SUBLANES, W), row),
        scratch_shapes=[pltpu.VMEM((D, F2), BF16), pltpu.VMEM((F, D), BF16)],
    )
    return pl.pallas_call(
        functools.partial(_experts_body, F=F, tg=tg),
        grid_spec=grid_spec,
        out_shape=jax.ShapeDtypeStruct((n_rows, W), F32),
        compiler_params=_cparams(("arbitrary",)),
        name="moe_experts",
    )(te, n_used, xs, wgu, bgu, wd, bd)


def _combine_body(pos_ref, route_ref, h_ref, y_hbm, o_ref, pos_smem, buf_ref, sem_p, sem_d, *, tm):
    i = pl.program_id(0)
    n = pl.num_programs(0)
    slot = i & 1

    def gather_rows(step, s):
        cp = pltpu.make_async_copy(pos_ref.at[:, pl.ds(pl.multiple_of(step * tm, tm), tm)],
                                   pos_smem.at[s], sem_p)
        cp.start()
        cp.wait()
        return functools.partial(row_copy, s)

    def row_copy(s, t, kk):
        src = pl.multiple_of(pos_smem[s, kk, t], SUBLANES)
        return pltpu.make_async_copy(
            y_hbm.at[pl.ds(src, SUBLANES)],
            buf_ref.at[s, kk, pl.ds(pl.multiple_of(t * SUBLANES, SUBLANES), SUBLANES)], sem_d.at[s])

    @pl.when(i == 0)
    def _():
        _issue_rows(gather_rows(0, 0), tm)

    @pl.when(i + 1 < n)
    def _():
        _issue_rows(gather_rows(i + 1, 1 - slot), tm)

    _drain_rows(functools.partial(row_copy, slot), tm)

    route = route_ref[...]
    lane = lax.broadcasted_iota(I32, (tm, LANES), 1)
    ws = [jnp.sum(jnp.where(lane == TOP_K + kk, route, 0.0), axis=-1, keepdims=True)
          for kk in range(TOP_K)]
    for c in range(SUBLANES):
        sl = slice(c * LANES, (c + 1) * LANES)
        acc = h_ref[:, sl]
        for kk in range(TOP_K):
            acc = acc + ws[kk] * buf_ref[slot, kk, pl.ds(c, tm, stride=SUBLANES), :]
        o_ref[:, sl] = acc


def _combine(pos, route, h1, y, *, tm):
    T, D = h1.shape
    W = y.shape[1]
    return pl.pallas_call(
        functools.partial(_combine_body, tm=tm),
        grid=(T // tm,),
        in_specs=[pl.BlockSpec((SUBLANES, T), lambda i: (0, 0)),
                  pl.BlockSpec((tm, LANES), lambda i: (i, 0)),
                  pl.BlockSpec((tm, D), lambda i: (i, 0)),
                  pl.BlockSpec(memory_space=pl.ANY)],
        out_specs=pl.BlockSpec((tm, D), lambda i: (i, 0)),
        out_shape=jax.ShapeDtypeStruct((T, D), F32),
        scratch_shapes=[pltpu.SMEM((2, SUBLANES, tm), I32),
                        pltpu.VMEM((2, TOP_K, tm * SUBLANES, W), F32),
                        pltpu.SemaphoreType.DMA, pltpu.SemaphoreType.DMA((2,))],
        compiler_params=_cparams(("arbitrary",)),
        name="moe_combine",
    )(pos, route, h1, y)


def _blockdiag_dense(w, width):
    nb, bs, _ = w.shape
    per = width // bs
    wt = w.reshape(nb // per, per, bs, bs)
    eye = jnp.eye(per, dtype=w.dtype)
    dense = jnp.einsum('gpio,pq->gpiqo', wt, eye)
    return dense.reshape(nb // per, width, width)


def _rope_tables(S):
    half = DA_HEAD_DIM // 2
    inv = ROPE_THETA ** (-(jnp.arange(half, dtype=F32) * 2.0 / DA_HEAD_DIM))
    ang = jnp.arange(S, dtype=F32)[:, None] * inv[None, :]
    cos = jnp.tile(jnp.cos(ang), (1, LANES // half))
    sign = jnp.where((jnp.arange(LANES) % DA_HEAD_DIM) < half, -1.0, 1.0).astype(F32)
    sin = jnp.tile(jnp.sin(ang), (1, LANES // half)) * sign[None, :]
    return cos, sin


def _tile(n, pref):
    return pref if n % pref == 0 else n


def _layer(h2, mem2, B, S, lambda_init, attn_norm_g, w_in, b_gate, da_q_norm_g, da_k_norm_g,
           da_lambda_q1, da_lambda_k1, da_lambda_q2, da_lambda_k2, da_subln_g, ml_conv_w, ml_conv_b,
           ml_wq, ml_wk, ml_wv, ml_w_if, ml_b_if, ml_out_norm_g, ml_skip, mem_norm_g, ca_w_kv,
           ca_q_norm_g, ca_k_norm_g, w_branch_da, w_branch_ml, w_branch_ca, w_out, ffn_norm_g,
           router_w, router_b, w_gate_up, b_gate_up, w_down, b_down):
    T, D = h2.shape
    M = mem2.shape[0] // B
    QK = DA_HEADS * 2 * DA_HEAD_DIM
    VW = DA_HEADS * DA_V_DIM
    C = ml_conv_w.shape[1]
    CAW = ca_w_kv.shape[1] // 2
    E = router_w.shape[1]

    o = [0, QK, 2 * QK, 2 * QK + VW, 2 * QK + VW + C, 2 * QK + VW + 2 * C, 2 * QK + VW + 2 * C + CAW]
    w_re = jnp.concatenate([w_in[:, o[0]:o[3]], w_in[:, o[5]:o[6]], w_in[:, o[3]:o[5]], w_in[:, o[6]:]],
                           axis=1).astype(BF16)
    tn = 1024
    col_v, col_caq, col_mlx, col_mlz, col_gate = 2 * QK, 2 * QK + VW, 2 * QK + VW + CAW, \
        2 * QK + VW + CAW + C, 2 * QK + VW + CAW + 2 * C
    proj = _inproj(h2, attn_norm_g[None, :], w_re, tm=_tile(T, 2048), tn=tn)

    g2 = jnp.stack([jnp.tile(da_q_norm_g, LANES // DA_HEAD_DIM),
                    jnp.tile(da_k_norm_g, LANES // DA_HEAD_DIM)])[:, None, :]
    cos_t, sin_t = _rope_tables(S)
    lane_grp = jnp.arange(LANES) // DA_HEAD_DIM
    gm = (lane_grp[:, None] == lane_grp[None, :]).astype(BF16)
    qk = _qkrope(proj, g2, cos_t, sin_t, gm, T=T, S=S, width=QK, tm=_tile(S, 1024))
    y_da = _diff_attention(qk, proj, da_lambda_q1[None, :], da_lambda_k1[None, :], da_lambda_q2[None, :],
                           da_lambda_k2[None, :], da_subln_g[None, :], da_q_norm_g[None, :],
                           da_k_norm_g[None, :], B=B, S=S, tq=_tile(S, 512),
                           v_col0=col_v // LANES, q_scale=DA_Q_SCALE, lambda_init=lambda_init)

    bd = jnp.stack([_blockdiag_dense(ml_wq, MXU_DIM), _blockdiag_dense(ml_wk, MXU_DIM),
                    _blockdiag_dense(ml_wv, MXU_DIM)]).astype(BF16)
    wif = jnp.pad(ml_w_if.reshape(3, C, 2 * ML_HEADS), ((0, 0), (0, 0), (0, LANES - 2 * ML_HEADS))).astype(BF16)
    bif = jnp.pad(ml_b_if, (0, LANES - 2 * ML_HEADS))[None, :]
    xc, mq, mk, mv, gates = _mlpre(proj, ml_conv_w, ml_conv_b[None, :], bd, wif, bif, B=B, S=S, C=C,
                                   tm=_tile(S, 512), x_col=col_mlx // C)
    dml = C // ML_HEADS
    y_ml = _mlstm(mq, mk, mv, gates, xc, proj, ml_out_norm_g[None, :], ml_skip[None, :], B=B, S=S, C=C,
                  L=_tile(S, 256), z_col0=col_mlz // dml, hpb=2)

    kmem, vmem = _memkv(mem2, mem_norm_g[None, :], ca_w_kv.astype(BF16), ca_k_norm_g[None, :], B=B, M=M, W=CAW)
    y_ca = _xattn(proj, kmem, vmem, ca_q_norm_g[None, :], B=B, S=S, M=M, W=CAW, tm=_tile(S, 512),
                  q_col=col_caq // CAW)

    tmx = _tile(T, 512)
    tri = (jnp.arange(tmx)[None, :] < jnp.arange(tmx)[:, None]).astype(BF16)
    rw = jnp.pad(router_w, ((0, 0), (0, LANES - E))).astype(BF16)
    rb = jnp.pad(router_b, (0, LANES - E))[None, :]
    h1, xp, route, cnt = _mix(h2, y_da, y_ml, y_ca, proj, b_gate.reshape(N_BRANCH, 1, D),
                              w_branch_da.astype(BF16), w_branch_ml.astype(BF16),
                              w_branch_ca.astype(BF16), w_out.astype(BF16), ffn_norm_g[None, :], rw, rb, tri,
                              tm=tmx, g_col=col_gate // D, n_exp=E)

    tg = _tile(T, 512)
    n_tiles = (T * TOP_K) // tg + E
    n_rows = n_tiles * tg
    assert D == SUBLANES * LANES, "MoE rows are moved as one (8,128) f32 tile each"
    xs0 = jnp.zeros((n_rows * SUBLANES, LANES), F32)
    pos, te, n_used, xs = _scatter(route, cnt, xp, xs0, tm=_tile(T, 512), tg=tg, n_exp=E, n_tiles=n_tiles)
    y = _experts(te[:, 0], n_used[0, :1], xs, w_gate_up, b_gate_up[:, None, :],
                 w_down, b_down[:, None, :], tg=tg, n_tiles=n_tiles)
    return _combine(pos, route, h1, y, tm=_tile(T, 256))


def kernel(x, mem, attn_norm_g, w_in, b_gate, da_q_norm_g, da_k_norm_g, da_lambda_q1, da_lambda_k1, da_lambda_q2, da_lambda_k2, da_subln_g, ml_conv_w, ml_conv_b, ml_wq, ml_wk, ml_wv, ml_w_if, ml_b_if, ml_out_norm_g, ml_skip, mem_norm_g, ca_w_kv, ca_q_norm_g, ca_k_norm_g, w_branch_da, w_branch_ml, w_branch_ca, w_out, ffn_norm_g, router_w, router_b, w_gate_up, b_gate_up, w_down, b_down):
    B, S, D = x.shape
    depth = w_in.shape[0]
    h2 = x.reshape(B * S, D)
    mem2 = mem.reshape(B * mem.shape[1], D)
    params = (attn_norm_g, w_in, b_gate, da_q_norm_g, da_k_norm_g, da_lambda_q1, da_lambda_k1,
              da_lambda_q2, da_lambda_k2, da_subln_g, ml_conv_w, ml_conv_b, ml_wq, ml_wk, ml_wv, ml_w_if,
              ml_b_if, ml_out_norm_g, ml_skip, mem_norm_g, ca_w_kv, ca_q_norm_g, ca_k_norm_g, w_branch_da,
              w_branch_ml, w_branch_ca, w_out, ffn_norm_g, router_w, router_b, w_gate_up, b_gate_up,
              w_down, b_down)
    for l in range(depth):
        lambda_init = 0.8 - 0.6 * math.exp(-0.3 * l)
        h2 = _layer(h2, mem2, B, S, lambda_init, *[p[l] for p in params])
    return h2.reshape(B, S, D)
```

```python
import functools
import math

import jax
import jax.numpy as jnp
import numpy as np
from jax import lax
from jax.experimental import pallas as pl
from jax.experimental.pallas import tpu as pltpu

F32 = jnp.float32
BF16 = jnp.bfloat16
I32 = jnp.int32
U32 = jnp.uint32

NORM_EPS = 1e-6
ROPE_THETA = 10000.0
CHUNK = 64

DA_HEADS = 8
DA_HEAD_DIM = 64
DA_V_DIM = 128
ML_HEADS = 4
ML_CONV = 4
ML_QKV_BLOCK = 4
CA_HEADS = 4
N_BRANCH = 3
TOP_K = 4
SWIGLU_LIMIT = 7.0
SWIGLU_ALPHA = 1.702

LANES = 128
SUBLANES = 8
MXU_DIM = 256
VMEM_LIMIT = 56 * 1024 * 1024
NEG_BIG = -1e30
DA_Q_SCALE = DA_HEAD_DIM ** -0.5 * math.log2(math.e)
DA_BOUND_SLACK = 1.01
DA_BOUND_LIMIT = 40.0


def _cparams(sem):
    return pltpu.CompilerParams(dimension_semantics=sem, vmem_limit_bytes=VMEM_LIMIT)


def _sigmoid(x):
    return 0.5 * jnp.tanh(0.5 * x) + 0.5


def _silu(x):
    return x * _sigmoid(x)


def _inproj_body(x_ref, g_ref, w_ref, o_ref, xn_ref):
    j = pl.program_id(1)

    @pl.when(j == 0)
    def _():
        x = x_ref[...]
        ms = jnp.mean(x * x, axis=-1, keepdims=True)
        xn_ref[...] = (x * lax.rsqrt(ms + NORM_EPS) * g_ref[...]).astype(BF16)

    o_ref[...] = jnp.dot(xn_ref[...], w_ref[...], preferred_element_type=F32).astype(o_ref.dtype)


def _inproj(x2, g, w_bf, *, tm, tn):
    T, D = x2.shape
    N = w_bf.shape[1]
    return pl.pallas_call(
        _inproj_body,
        grid=(T // tm, N // tn),
        in_specs=[
            pl.BlockSpec((tm, D), lambda i, j: (i, 0)),
            pl.BlockSpec((1, D), lambda i, j: (0, 0)),
            pl.BlockSpec((D, tn), lambda i, j: (0, j)),
        ],
        out_specs=pl.BlockSpec((tm, tn), lambda i, j: (i, j)),
        out_shape=jax.ShapeDtypeStruct((T, N), BF16),
        scratch_shapes=[pltpu.VMEM((tm, D), BF16)],
        compiler_params=_cparams(("arbitrary", "arbitrary")),
        name="inproj",
    )(x2, g, w_bf)


def _group_sumsq(xb, gm):
    sq = xb * xb
    hi = sq.astype(BF16)
    lo = (sq - hi.astype(F32)).astype(BF16)
    return (jnp.dot(hi, gm, preferred_element_type=F32)
            + jnp.dot(lo, gm, preferred_element_type=F32))


def _qkrope_body(x_ref, g_ref, cos_ref, sin_ref, gm_ref, o_ref, *, ncb, scale):
    c = pl.program_id(1)
    g = g_ref[...] * jnp.where(c == 0, scale, 1.0).astype(F32)
    cos = cos_ref[...]
    sin = sin_ref[...]
    gm = gm_ref[...]
    for cb in range(ncb):
        sl = slice(cb * LANES, (cb + 1) * LANES)
        xb = x_ref[:, sl].astype(F32)
        ss = _group_sumsq(xb, gm)
        y = xb * lax.rsqrt(ss * (1.0 / DA_HEAD_DIM) + NORM_EPS) * g
        sw = pltpu.roll(y, LANES // 2, 1)
        o_ref[:, sl] = (y * cos + sw * sin).astype(o_ref.dtype)


def _qkrope(proj, g2, cos_t, sin_t, gm, *, T, S, width, tm):
    n_s = S // tm
    return pl.pallas_call(
        functools.partial(_qkrope_body, ncb=width // LANES, scale=DA_Q_SCALE),
        grid=(T // tm, 2),
        in_specs=[
            pl.BlockSpec((tm, width), lambda i, c: (i, c)),
            pl.BlockSpec((None, 1, LANES), lambda i, c: (c, 0, 0)),
            pl.BlockSpec((tm, LANES), lambda i, c: (i % n_s, 0)),
            pl.BlockSpec((tm, LANES), lambda i, c: (i % n_s, 0)),
            pl.BlockSpec((LANES, LANES), lambda i, c: (0, 0)),
        ],
        out_specs=pl.BlockSpec((tm, width), lambda i, c: (i, c)),
        out_shape=jax.ShapeDtypeStruct((T, 2 * width), BF16),
        compiler_params=_cparams(("arbitrary", "arbitrary")),
        name="qkrope",
    )(proj, g2, cos_t, sin_t, gm)


def _da_body(q_ref, k_ref, v_ref, lq1_ref, lk1_ref, lq2_ref, lk2_ref, subg_ref, qg_ref, kg_ref, o_ref,
             m_ref, acc_ref, kt_ref, *, tq, n_kv, q_scale, lambda_init):
    i = pl.program_id(2)
    q = q_ref[...]
    lane = lax.broadcasted_iota(I32, (1, LANES), 1)
    lo = (lane & (DA_HEAD_DIM // 2)) == 0
    zero = jnp.zeros_like(q)
    qc = (jnp.where(lo, q, zero), jnp.where(lo, zero, q))

    @pl.when(i == 0)
    def _():
        def tbody(j, carry):
            start = pl.multiple_of(j * tq, tq)
            kt_ref[:, pl.ds(start, tq)] = k_ref[pl.ds(start, tq), :].astype(F32).T.astype(BF16)
            return carry
        lax.fori_loop(0, n_kv, tbody, 0)

    bound = (DA_HEAD_DIM * q_scale * DA_BOUND_SLACK
             * jnp.max(jnp.abs(qg_ref[...]), keepdims=True) * jnp.max(jnp.abs(kg_ref[...]), keepdims=True))
    bound_max = jnp.max(bound)

    acc_ref[...] = jnp.zeros(acc_ref.shape, F32)
    ones = jnp.ones((tq, LANES), BF16)

    def scores(j):
        ktj = kt_ref[:, pl.ds(pl.multiple_of(j * tq, tq), tq)]
        return [jnp.dot(qc[c], ktj, preferred_element_type=F32) for c in range(2)]

    def diag_mask():
        r = lax.broadcasted_iota(I32, (tq, tq), 0)
        cidx = lax.broadcasted_iota(I32, (tq, tq), 1)
        return (cidx | (CHUNK - 1)) <= (r | (CHUNK - 1))

    def v_aug(j):
        return jnp.concatenate([v_ref[pl.ds(pl.multiple_of(j * tq, tq), tq), :], ones], axis=1)

    def consume_bounded(j, ss, masked):
        vj = v_aug(j)
        ps = []
        for c in range(2):
            p = jnp.exp2(ss[c] - bound)
            if masked:
                p = jnp.where(diag_mask(), p, 0.0)
            ps.append(p.astype(BF16))
        pvs = [jnp.dot(ps[c], vj, preferred_element_type=F32) for c in range(2)]
        for c in range(2):
            acc_ref[c] = acc_ref[c] + pvs[c]

    def consume_online(j, ss, masked):
        vj = v_aug(j)
        ps, alphas = [], []
        for c in range(2):
            s = jnp.where(diag_mask(), ss[c], NEG_BIG) if masked else ss[c]
            m_old = m_ref[c]
            m_new = jnp.maximum(m_old, jnp.max(s, axis=-1, keepdims=True))
            ps.append(jnp.exp2(s - jnp.concatenate([m_new] * (tq // LANES), axis=1)).astype(BF16))
            alphas.append(jnp.exp2(m_old - m_new))
            m_ref[c] = m_new
        pvs = [jnp.dot(ps[c], vj, preferred_element_type=F32) for c in range(2)]
        for c in range(2):
            acc_ref[c] = jnp.concatenate([alphas[c], alphas[c]], axis=1) * acc_ref[c] + pvs[c]

    def sweep(consume):
        def pair(jj, carry):
            sa = scores(2 * jj)
            sb = scores(2 * jj + 1)
            consume(2 * jj, sa, False)
            consume(2 * jj + 1, sb, False)
            return carry

        lax.fori_loop(0, lax.shift_right_logical(i, 1), pair, 0)

        @pl.when((i & 1) == 1)
        def _():
            consume(i - 1, scores(i - 1), False)

        consume(i, scores(i), True)

    @pl.when(bound_max < DA_BOUND_LIMIT)
    def _():
        sweep(consume_bounded)

    @pl.when(jnp.logical_not(bound_max < DA_BOUND_LIMIT))
    def _():
        m_ref[...] = jnp.full(m_ref.shape, NEG_BIG, F32)
        sweep(consume_online)

    f32 = F32
    lam = (jnp.exp(jnp.sum(lq1_ref[...].astype(f32) * lk1_ref[...].astype(f32), keepdims=True))
           - jnp.exp(jnp.sum(lq2_ref[...].astype(f32) * lk2_ref[...].astype(f32), keepdims=True))
           + lambda_init)
    a1 = acc_ref[0]
    a2 = acc_ref[1]
    o = a1[:, :DA_V_DIM] / a1[:, DA_V_DIM:] - lam * (a2[:, :DA_V_DIM] / a2[:, DA_V_DIM:])
    ms = jnp.mean(o * o, axis=-1, keepdims=True)
    o = o * lax.rsqrt(ms + NORM_EPS) * subg_ref[...]
    o_ref[...] = (o * (1.0 - lambda_init)).astype(o_ref.dtype)


def _diff_attention(qk, proj, lq1, lk1, lq2, lk2, subg, qg, kg, *, B, S, tq, v_col0, q_scale, lambda_init):
    T = B * S
    H = DA_HEADS
    n_q = S // tq
    vec = lambda n: pl.BlockSpec((1, n), lambda b, h, i: (0, 0))
    return pl.pallas_call(
        functools.partial(_da_body, tq=tq, n_kv=n_q, q_scale=q_scale, lambda_init=lambda_init),
        grid=(B, H, n_q),
        in_specs=[
            pl.BlockSpec((tq, LANES), lambda b, h, i: (b * n_q + i, h)),
            pl.BlockSpec((S, LANES), lambda b, h, i: (b, H + h)),
            pl.BlockSpec((S, LANES), lambda b, h, i: (b, v_col0 + h)),
            vec(DA_HEAD_DIM), vec(DA_HEAD_DIM), vec(DA_HEAD_DIM), vec(DA_HEAD_DIM),
            vec(DA_V_DIM), vec(DA_HEAD_DIM), vec(DA_HEAD_DIM),
        ],
        out_specs=pl.BlockSpec((tq, LANES), lambda b, h, i: (b * n_q + i, h)),
        out_shape=jax.ShapeDtypeStruct((T, H * DA_V_DIM), BF16),
        scratch_shapes=[pltpu.VMEM((2, tq, LANES), F32), pltpu.VMEM((2, tq, 2 * DA_V_DIM), F32),
                        pltpu.VMEM((LANES, S), BF16)],
        compiler_params=_cparams(("arbitrary", "arbitrary", "arbitrary")),
        name="diff_attn",
    )(qk, qk, proj, lq1, lk1, lq2, lk2, subg, qg, kg)


def _mlpre_body(x_ref, cw_ref, cb_ref, bd_ref, wif_ref, bif_ref,
                xc_ref, q_ref, k_ref, v_ref, g_ref, prev_ref, *, tm, ncb, kscale):
    s_idx = pl.program_id(1)

    @pl.when(s_idx == 0)
    def _():
        prev_ref[...] = jnp.zeros(prev_ref.shape, F32)

    row8 = lax.broadcasted_iota(I32, (SUBLANES, MXU_DIM), 0)
    gacc = jnp.zeros((tm, LANES), F32) + bif_ref[...]
    for cb in range(ncb):
        sl = slice(cb * MXU_DIM, (cb + 1) * MXU_DIM)
        xb16 = x_ref[:, sl]
        x = xb16.astype(F32)
        prev = prev_ref[:, sl]
        conv = x * cw_ref[ML_CONV - 1:ML_CONV, sl] + cb_ref[:, sl]
        for d in range(1, ML_CONV):
            xs = pltpu.roll(x, d, 0)
            ps = pltpu.roll(prev, d, 0)
            head = jnp.where(row8 < d, ps, xs[:SUBLANES])
            shifted = jnp.concatenate([head, xs[SUBLANES:]], axis=0)
            conv = conv + shifted * cw_ref[ML_CONV - 1 - d:ML_CONV - d, sl]
        prev_ref[:, sl] = x[tm - SUBLANES:]
        xc = _silu(conv)
        xc16 = xc.astype(BF16)
        q = jnp.dot(xc16, bd_ref[0, cb], preferred_element_type=F32)
        k = jnp.dot(xc16, bd_ref[1, cb], preferred_element_type=F32)
        v = jnp.dot(xb16, bd_ref[2, cb], preferred_element_type=F32)
        q16, k16, v16 = q.astype(BF16), k.astype(BF16), v.astype(BF16)
        gacc = gacc + jnp.dot(q16, wif_ref[0, sl, :], preferred_element_type=F32)
        gacc = gacc + jnp.dot(k16, wif_ref[1, sl, :], preferred_element_type=F32)
        gacc = gacc + jnp.dot(v16, wif_ref[2, sl, :], preferred_element_type=F32)
        xc_ref[:, sl] = xc16
        q_ref[:, sl] = q16
        k_ref[:, sl] = (k * kscale).astype(BF16)
        v_ref[:, sl] = v16
    g_ref[...] = gacc


def _mlpre(proj, conv_w, conv_b, bd, wif, bif, *, B, S, C, tm, x_col):
    T = B * S
    n_s = S // tm
    row = lambda i: pl.BlockSpec((tm, C), lambda b, s: (b * n_s + s, i))
    outs = pl.pallas_call(
        functools.partial(_mlpre_body, tm=tm, ncb=C // MXU_DIM, kscale=(C // ML_HEADS) ** -0.5),
        grid=(B, n_s),
        in_specs=[
            row(x_col),
            pl.BlockSpec((ML_CONV, C), lambda b, s: (0, 0)),
            pl.BlockSpec((1, C), lambda b, s: (0, 0)),
            pl.BlockSpec(bd.shape, lambda b, s: (0, 0, 0, 0)),
            pl.BlockSpec(wif.shape, lambda b, s: (0, 0, 0)),
            pl.BlockSpec((1, LANES), lambda b, s: (0, 0)),
        ],
        out_specs=[row(0), row(0), row(0), row(0),
                   pl.BlockSpec((tm, LANES), lambda b, s: (b * n_s + s, 0))],
        out_shape=[jax.ShapeDtypeStruct((T, C), BF16)] * 4 + [jax.ShapeDtypeStruct((T, LANES), F32)],
        scratch_shapes=[pltpu.VMEM((SUBLANES, C), F32)],
        compiler_params=_cparams(("arbitrary", "arbitrary")),
        name="mlstm_pre",
    )(proj, conv_w, conv_b, bd, wif, bif)
    return outs


def _mlstm_body(q_ref, k_ref, v_ref, g_ref, xc_ref, z_ref, ng_ref, skip_ref, o_ref,
                c_ref, cb_ref, n_ref, m_ref, *, L, d, hpb):
    hg = pl.program_id(1)
    c_idx = pl.program_id(2)

    @pl.when(c_idx == 0)
    def _():
        c_ref[...] = jnp.zeros(c_ref.shape, F32)
        cb_ref[...] = jnp.zeros(cb_ref.shape, BF16)
        n_ref[...] = jnp.zeros(n_ref.shape, F32)
        m_ref[...] = jnp.zeros(m_ref.shape, F32)

    g = g_ref[...]
    gt = g.T
    lane = lax.broadcasted_iota(I32, (1, LANES), 1)
    sub = lax.broadcasted_iota(I32, (LANES, 1), 0)
    t_idx = lax.broadcasted_iota(I32, (L, L), 0)
    s_idx = lax.broadcasted_iota(I32, (L, L), 1)
    tri = s_idx <= t_idx

    for hh in range(hpb):
        h = hg * hpb + hh
        sl = slice(hh * d, (hh + 1) * d)
        i_col = jnp.sum(jnp.where(lane == h, g, 0.0), axis=1, keepdims=True)
        f_col = jnp.sum(jnp.where(lane == ML_HEADS + h, g, 0.0), axis=1, keepdims=True)
        i_row = jnp.sum(jnp.where(sub == h, gt, 0.0), axis=0, keepdims=True)
        f_row = jnp.sum(jnp.where(sub == ML_HEADS + h, gt, 0.0), axis=0, keepdims=True)
        lf_col = jax.nn.log_sigmoid(f_col)
        lf_row = jax.nn.log_sigmoid(f_row)

        b_col = jnp.sum(jnp.where(tri, lf_row, 0.0), axis=1, keepdims=True)
        b_row = jnp.sum(jnp.where(t_idx <= s_idx, lf_col, 0.0), axis=0, keepdims=True)
        b_last = jnp.sum(lf_row, axis=1, keepdims=True)

        m_prev = m_ref[hh]
        dlog = jnp.where(tri, b_col - b_row + i_row, NEG_BIG)
        inter_log = b_col + m_prev
        m_rowv = jnp.maximum(inter_log, jnp.max(dlog, axis=1, keepdims=True))
        dw = jnp.exp(dlog - m_rowv)
        inter_w = jnp.exp(inter_log - m_rowv)

        q = q_ref[:, sl]
        k = k_ref[:, sl]
        v = v_ref[:, sl]
        qk = lax.dot_general(q, k, (((1,), (1,)), ((), ())), preferred_element_type=F32)
        qc = jnp.dot(q, cb_ref[hh], preferred_element_type=F32)
        s = qk * dw

        state_row = b_last - b_row + i_row
        state_col = b_last - b_col + i_col
        m_new = jnp.maximum(b_last + m_prev, jnp.max(state_row, axis=1, keepdims=True))
        decay = jnp.exp(b_last + m_prev - m_new)
        ws_col = jnp.exp(state_col - m_new)
        kwf = k.astype(F32) * ws_col

        num = jnp.dot(s.astype(BF16), v, preferred_element_type=F32) + inter_w * qc
        dc = lax.dot_general(kwf.astype(BF16), v, (((0,), (0,)), ((), ())), preferred_element_type=F32)
        den = (jnp.sum(s, axis=1, keepdims=True)
               + inter_w * jnp.sum(q.astype(F32) * n_ref[hh], axis=1, keepdims=True))
        hv = num / jnp.maximum(jnp.abs(den), jnp.exp(-m_rowv))

        c_new = decay * c_ref[hh] + dc
        c_ref[hh] = c_new
        cb_ref[hh] = c_new.astype(BF16)
        n_ref[hh] = decay * n_ref[hh] + jnp.sum(kwf, axis=0, keepdims=True)
        m_ref[hh] = m_new

        mu = jnp.mean(hv, axis=-1, keepdims=True)
        cen = hv - mu
        var = jnp.mean(cen * cen, axis=-1, keepdims=True)
        hn = cen * lax.rsqrt(var + NORM_EPS) * ng_ref[:, sl]
        y = (hn + skip_ref[:, sl] * xc_ref[:, sl].astype(F32)) * _silu(z_ref[:, sl].astype(F32))
        o_ref[:, sl] = y.astype(o_ref.dtype)


def _mlstm(q, k, v, gates, xc, proj, norm_g, skip, *, B, S, C, L, z_col0, hpb):
    T = B * S
    H = ML_HEADS
    d = C // H
    w = hpb * d
    nc = S // L
    blk = lambda off: pl.BlockSpec((L, w), lambda b, h, c: (b * nc + c, off + h))
    par = pl.BlockSpec((1, w), lambda b, h, c: (0, h))
    return pl.pallas_call(
        functools.partial(_mlstm_body, L=L, d=d, hpb=hpb),
        grid=(B, H // hpb, nc),
        in_specs=[blk(0), blk(0), blk(0),
                  pl.BlockSpec((L, LANES), lambda b, h, c: (b * nc + c, 0)),
                  blk(0), blk(z_col0 // hpb), par, par],
        out_specs=blk(0),
        out_shape=jax.ShapeDtypeStruct((T, C), BF16),
        scratch_shapes=[pltpu.VMEM((hpb, d, d), F32), pltpu.VMEM((hpb, d, d), BF16),
                        pltpu.VMEM((hpb, 1, d), F32), pltpu.VMEM((hpb, 1, 1), F32)],
        compiler_params=_cparams(("arbitrary", "arbitrary", "arbitrary")),
        name="mlstm_scan",
    )(q, k, v, gates, xc, proj, norm_g, skip)


def _memkv_body(mem_ref, mg_ref, w_ref, kg_ref, k_ref, v_ref, *, W, dh):
    x = mem_ref[...]
    ms = jnp.mean(x * x, axis=-1, keepdims=True)
    xn = (x * lax.rsqrt(ms + NORM_EPS) * mg_ref[...]).astype(BF16)
    kv = jnp.dot(xn, w_ref[...], preferred_element_type=F32)
    for hd in range(W // dh):
        sl = slice(hd * dh, (hd + 1) * dh)
        kh = kv[:, sl]
        msk = jnp.mean(kh * kh, axis=-1, keepdims=True)
        k_ref[:, sl] = (kh * lax.rsqrt(msk + NORM_EPS) * kg_ref[...]).astype(BF16)
    v_ref[...] = kv[:, W:].astype(BF16)


def _memkv(mem2, mem_g, w_kv_bf, k_g, *, B, M, W):
    D = mem2.shape[1]
    dh = W // CA_HEADS
    return pl.pallas_call(
        functools.partial(_memkv_body, W=W, dh=dh),
        grid=(B,),
        in_specs=[pl.BlockSpec((M, D), lambda b: (b, 0)),
                  pl.BlockSpec((1, D), lambda b: (0, 0)),
                  pl.BlockSpec((D, 2 * W), lambda b: (0, 0)),
                  pl.BlockSpec((1, dh), lambda b: (0, 0))],
        out_specs=[pl.BlockSpec((M, W), lambda b: (b, 0))] * 2,
        out_shape=[jax.ShapeDtypeStruct((B * M, W), BF16)] * 2,
        compiler_params=_cparams(("arbitrary",)),
        name="mem_kv",
    )(mem2, mem_g, w_kv_bf, k_g)


def _xattn_body(q_ref, k_ref, v_ref, qg_ref, o_ref, *, W, dh):
    scale = dh ** -0.5
    for hd in range(W // dh):
        sl = slice(hd * dh, (hd + 1) * dh)
        qh = q_ref[:, sl].astype(F32)
        ms = jnp.mean(qh * qh, axis=-1, keepdims=True)
        qn = (qh * lax.rsqrt(ms + NORM_EPS) * (qg_ref[...] * scale)).astype(BF16)
        s = lax.dot_general(qn, k_ref[:, sl], (((1,), (1,)), ((), ())), preferred_element_type=F32)
        mx = jnp.max(s, axis=-1, keepdims=True)
        p = jnp.exp(s - mx)
        p = p / jnp.sum(p, axis=-1, keepdims=True)
        o_ref[:, sl] = jnp.dot(p.astype(BF16), v_ref[:, sl],
                               preferred_element_type=F32).astype(o_ref.dtype)


def _xattn(proj, kmem, vmem, q_g, *, B, S, M, W, tm, q_col):
    T = B * S
    n_s = S // tm
    dh = W // CA_HEADS
    return pl.pallas_call(
        functools.partial(_xattn_body, W=W, dh=dh),
        grid=(T // tm,),
        in_specs=[pl.BlockSpec((tm, W), lambda i: (i, q_col)),
                  pl.BlockSpec((M, W), lambda i: (i // n_s, 0)),
                  pl.BlockSpec((M, W), lambda i: (i // n_s, 0)),
                  pl.BlockSpec((1, dh), lambda i: (0, 0))],
        out_specs=pl.BlockSpec((tm, W), lambda i: (i, 0)),
        out_shape=jax.ShapeDtypeStruct((T, W), BF16),
        compiler_params=_cparams(("arbitrary",)),
        name="mem_xattn",
    )(proj, kmem, vmem, q_g)


def _store_tile_rows(ref, x):
    n = x.shape[0]
    for c in range(SUBLANES):
        ref[pl.ds(c, n, stride=SUBLANES), :] = x[:, c * LANES:(c + 1) * LANES]


def _load_tile_rows(ref, n):
    return [ref[pl.ds(c, n, stride=SUBLANES), :] for c in range(SUBLANES)]


def _mix_body(x_ref, yda_ref, yml_ref, yca_ref, gda_ref, gml_ref, gca_ref, bg_ref, wda_ref, wml_ref, wca_ref,
              wout_ref, fg_ref, rw_ref, rb_ref, tri_ref,
              h_ref, xp_ref, route_ref, cnt_ref, carry_ref, *, n_exp):
    i = pl.program_id(0)

    @pl.when(i == 0)
    def _():
        carry_ref[...] = jnp.zeros(carry_ref.shape, F32)

    def gate(g_ref, b):
        return _sigmoid(g_ref[...].astype(F32) + bg_ref[b])

    mix = (gate(gda_ref, 0) * jnp.dot(yda_ref[...], wda_ref[...], preferred_element_type=F32)
           + gate(gml_ref, 1) * jnp.dot(yml_ref[...], wml_ref[...], preferred_element_type=F32)
           + gate(gca_ref, 2) * jnp.dot(yca_ref[...], wca_ref[...], preferred_element_type=F32))
    h1 = x_ref[...] + jnp.dot(mix.astype(BF16), wout_ref[...], preferred_element_type=F32)
    h_ref[...] = h1
    ms = jnp.mean(h1 * h1, axis=-1, keepdims=True)
    xn = h1 * lax.rsqrt(ms + NORM_EPS) * fg_ref[...]
    _store_tile_rows(xp_ref, xn)

    logits = jnp.dot(xn.astype(BF16), rw_ref[...], preferred_element_type=F32) + rb_ref[...]
    tm = logits.shape[0]
    lane = lax.broadcasted_iota(I32, (tm, LANES), 1)
    work = jnp.where(lane < n_exp, logits, NEG_BIG)
    sel = jnp.zeros((tm, LANES), F32)
    vals, idxs = [], []
    for _ in range(TOP_K):
        mx = jnp.max(work, axis=-1, keepdims=True)
        idx = jnp.min(jnp.where(work == mx, lane, LANES), axis=-1, keepdims=True)
        hit = lane == idx
        sel = jnp.where(hit, 1.0, sel)
        work = jnp.where(hit, NEG_BIG, work)
        vals.append(mx)
        idxs.append(idx)
    exps = [jnp.exp(v - vals[0]) for v in vals]
    tot = exps[0] + exps[1] + exps[2] + exps[3]

    cum = jnp.dot(tri_ref[...], sel.astype(BF16), preferred_element_type=F32) + carry_ref[...]
    route = jnp.zeros((tm, LANES), F32)
    for kk in range(TOP_K):
        rank = jnp.sum(jnp.where(lane == idxs[kk], cum, 0.0), axis=-1, keepdims=True)
        route = jnp.where(lane == kk, idxs[kk].astype(F32), route)
        route = jnp.where(lane == TOP_K + kk, exps[kk] / tot, route)
        route = jnp.where(lane == 2 * TOP_K + kk, rank, route)
    route_ref[...] = route
    carry_ref[...] = carry_ref[...] + jnp.sum(sel, axis=0, keepdims=True)
    cnt_ref[...] = carry_ref[...]


def _mix(x2, yda, yml, yca, proj, bg, wda, wml, wca, wout, fg, rw, rb, tri, *, tm, g_col, n_exp):
    T, D = x2.shape
    C = yml.shape[1]
    const = lambda shape: pl.BlockSpec(shape, lambda i: (0,) * len(shape))
    return pl.pallas_call(
        functools.partial(_mix_body, n_exp=n_exp),
        grid=(T // tm,),
        in_specs=[pl.BlockSpec((tm, D), lambda i: (i, 0)),
                  pl.BlockSpec((tm, D), lambda i: (i, 0)),
                  pl.BlockSpec((tm, C), lambda i: (i, 0)),
                  pl.BlockSpec((tm, D), lambda i: (i, 0)),
                  pl.BlockSpec((tm, D), lambda i: (i, g_col)),
                  pl.BlockSpec((tm, D), lambda i: (i, g_col + 1)),
                  pl.BlockSpec((tm, D), lambda i: (i, g_col + 2)),
                  const((N_BRANCH, 1, D)),
                  const((D, D)), const((C, D)), const((D, D)), const((D, D)),
                  const((1, D)), const((D, LANES)), const((1, LANES)), const((tm, tm))],
        out_specs=[pl.BlockSpec((tm, D), lambda i: (i, 0)),
                   pl.BlockSpec((tm * SUBLANES, LANES), lambda i: (i, 0)),
                   pl.BlockSpec((tm, LANES), lambda i: (i, 0)),
                   pl.BlockSpec((1, LANES), lambda i: (0, 0))],
        out_shape=[jax.ShapeDtypeStruct((T, D), F32),
                   jax.ShapeDtypeStruct((T * SUBLANES, LANES), F32),
                   jax.ShapeDtypeStruct((T, LANES), F32),
                   jax.ShapeDtypeStruct((1, LANES), F32)],
        scratch_shapes=[pltpu.VMEM((1, LANES), F32)],
        compiler_params=_cparams(("arbitrary",)),
        name="mix_route",
    )(x2, yda, yml, yca, proj, proj, proj, bg, wda, wml, wca, wout, fg, rw, rb, tri)


ROW_DMA_UNROLL = 8


def _issue_rows(row_copy, tm):
    def issue(t, carry):
        for kk in range(TOP_K):
            row_copy(t, kk).start(priority=kk % 2)
        return carry

    lax.fori_loop(0, tm, issue, 0, unroll=ROW_DMA_UNROLL)


def _drain_rows(row_copy, tm):
    def drain(t, carry):
        for kk in range(TOP_K):
            row_copy(t, kk).wait()
        return carry

    lax.fori_loop(0, tm, drain, 0, unroll=ROW_DMA_UNROLL)


def _issue_and_drain_rows(row_copy, tm):
    _issue_rows(row_copy, tm)
    _drain_rows(row_copy, tm)


def _group_starts(cnt, tg, n_exp):
    lane_r = lax.broadcasted_iota(I32, (LANES, LANES), 0)
    lane_c = lax.broadcasted_iota(I32, (LANES, LANES), 1)
    padded = jnp.ceil(cnt * (1.0 / tg)) * tg
    padded_col = jnp.sum(jnp.where(lane_r == lane_c, padded, 0.0), axis=1, keepdims=True)
    start = jnp.sum(jnp.where(lane_r < lane_c, padded_col, 0.0), axis=0, keepdims=True)
    return start, start + padded


def _scatter_body(route_ref, route_next_ref, cnt_ref, xp_ref, pos_ref, te_ref, nu_ref, xs_hbm,
                  posv_ref, pos_smem, padv_ref, pad_smem, zero_ref, sem_p, sem_d, sem_z,
                  *, tm, tg, n_exp, n_tiles):
    i = pl.program_id(0)
    n = pl.num_programs(0)
    slot = i & 1
    lane = lax.broadcasted_iota(I32, (tm, LANES), 1)
    start, end = _group_starts(cnt_ref[...], tg, n_exp)

    def stage_positions(route, s):
        posm = jnp.zeros((tm, LANES), F32)
        for kk in range(TOP_K):
            e = jnp.sum(jnp.where(lane == kk, route, 0.0), axis=-1, keepdims=True)
            rank = jnp.sum(jnp.where(lane == 2 * TOP_K + kk, route, 0.0), axis=-1, keepdims=True)
            st = jnp.sum(jnp.where(lane == e.astype(I32), start, 0.0), axis=-1, keepdims=True)
            posm = jnp.where(lane == kk, (st + rank) * SUBLANES, posm)
        posv_ref[s] = posm.astype(I32).T[:SUBLANES]
        cp = pltpu.make_async_copy(posv_ref.at[s], pos_smem.at[s], sem_p)
        cp.start()
        cp.wait()

    @pl.when(i == 0)
    def _():
        stage_positions(route_ref[...], 0)
        tstart = (lax.broadcasted_iota(I32, (n_tiles, LANES), 0) * tg).astype(F32)
        lane_t = lax.broadcasted_iota(I32, (n_tiles, LANES), 1)
        done = jnp.where((lane_t < n_exp) & (end <= tstart), 1.0, 0.0)
        te = jnp.sum(done, axis=-1, keepdims=True)
        te_ref[...] = jnp.broadcast_to(te, (n_tiles, LANES)).astype(I32)
        n_used = jnp.max(end, axis=-1, keepdims=True) * (1.0 / tg)
        nu_ref[...] = jnp.broadcast_to(n_used, (1, LANES)).astype(I32)

        cnt = cnt_ref[...]
        padv_ref[...] = jnp.zeros(padv_ref.shape, I32)
        padv_ref[0:1, :] = ((start + cnt) * SUBLANES).astype(I32)
        padv_ref[1:2, :] = (end - start - cnt).astype(I32)
        zero_ref[...] = jnp.zeros(zero_ref.shape, F32)
        cpz = pltpu.make_async_copy(padv_ref, pad_smem, sem_p)
        cpz.start()
        cpz.wait()

        def pad_copy(e, r):
            off = pl.multiple_of(pad_smem[0, e] + r * SUBLANES, SUBLANES)
            return pltpu.make_async_copy(zero_ref, xs_hbm.at[pl.ds(off, SUBLANES)], sem_z)

        def for_each_pad_row(fn):
            def per_expert(e, carry):
                def per_row(r, c2):
                    fn(pad_copy(e, r))
                    return c2
                lax.fori_loop(0, pad_smem[1, e], per_row, 0)
                return carry
            lax.fori_loop(0, n_exp, per_expert, 0)

        for_each_pad_row(lambda cp: cp.start())
        for_each_pad_row(lambda cp: cp.wait())

    def row_copy(t, kk):
        dst = pl.multiple_of(pos_smem[slot, kk, t], SUBLANES)
        return pltpu.make_async_copy(xp_ref.at[pl.ds(pl.multiple_of(t * SUBLANES, SUBLANES), SUBLANES)],
                                     xs_hbm.at[pl.ds(dst, SUBLANES)], sem_d)

    _issue_rows(row_copy, tm)

    @pl.when(i + 1 < n)
    def _():
        stage_positions(route_next_ref[...], 1 - slot)

    pos_ref[...] = posv_ref[slot]
    _drain_rows(row_copy, tm)


def _scatter(route, cnt, xp, *, tm, tg, n_exp, n_tiles):
    T = route.shape[0]
    W = xp.shape[1]
    n_rows = n_tiles * tg * SUBLANES
    n_steps = T // tm
    return pl.pallas_call(
        functools.partial(_scatter_body, tm=tm, tg=tg, n_exp=n_exp, n_tiles=n_tiles),
        grid=(n_steps,),
        in_specs=[pl.BlockSpec((tm, LANES), lambda i: (i, 0)),
                  pl.BlockSpec((tm, LANES), lambda i: (jnp.minimum(i + 1, n_steps - 1), 0)),
                  pl.BlockSpec((1, LANES), lambda i: (0, 0)),
                  pl.BlockSpec((tm * SUBLANES, W), lambda i: (i, 0))],
        out_specs=[pl.BlockSpec((SUBLANES, tm), lambda i: (0, i)),
                   pl.BlockSpec((n_tiles, LANES), lambda i: (0, 0)),
                   pl.BlockSpec((1, LANES), lambda i: (0, 0)),
                   pl.BlockSpec(memory_space=pl.ANY)],
        out_shape=[jax.ShapeDtypeStruct((SUBLANES, T), I32),
                   jax.ShapeDtypeStruct((n_tiles, LANES), I32),
                   jax.ShapeDtypeStruct((1, LANES), I32),
                   jax.ShapeDtypeStruct((n_rows, W), F32)],
        scratch_shapes=[pltpu.VMEM((2, SUBLANES, tm), I32), pltpu.SMEM((2, SUBLANES, tm), I32),
                        pltpu.VMEM((SUBLANES, LANES), I32), pltpu.SMEM((SUBLANES, LANES), I32),
                        pltpu.VMEM((SUBLANES, LANES), F32),
                        pltpu.SemaphoreType.DMA, pltpu.SemaphoreType.DMA, pltpu.SemaphoreType.DMA],
        compiler_params=_cparams(("arbitrary",)),
        name="moe_scatter",
    )(route, route, cnt, xp)


def _experts_body(te_ref, nu_ref, xs_ref, wgu_ref, bgu_ref, wd_ref, bd_ref, y_ref, wgu16_ref, wd16_ref,
                  *, F, tg):
    i = pl.program_id(0)
    live = i < nu_ref[0]
    new_expert = (i == 0) | (te_ref[i] != te_ref[jnp.maximum(i - 1, 0)])

    @pl.when(live & new_expert)
    def _():
        wgu16_ref[...] = wgu_ref[...].astype(BF16)
        wd16_ref[...] = wd_ref[...].astype(BF16)

    @pl.when(live)
    def _():
        x = jnp.concatenate([c.astype(BF16) for c in _load_tile_rows(xs_ref, tg)], axis=1)
        h = jnp.dot(x, wgu16_ref[...], preferred_element_type=F32) + bgu_ref[...]
        gate = jnp.minimum(h[:, :F], SWIGLU_LIMIT)
        up = jnp.clip(h[:, F:], -SWIGLU_LIMIT, SWIGLU_LIMIT)
        a = (up + 1.0) * (gate * _sigmoid(SWIGLU_ALPHA * gate))
        y = jnp.dot(a.astype(BF16), wd16_ref[...], preferred_element_type=F32) + bd_ref[...]
        _store_tile_rows(y_ref, y)


def _experts(te, n_used, xs, wgu, bgu, wd, bd, *, tg, n_tiles):
    n_rows, W = xs.shape
    E, D, F2 = wgu.shape
    F = F2 // 2
    row = lambda i, te, nu: (jnp.minimum(i, nu[0] - 1), 0)
    exp3 = lambda i, te, nu: (te[jnp.minimum(i, nu[0] - 1)], 0, 0)
    grid_spec = pltpu.PrefetchScalarGridSpec(
        num_scalar_prefetch=2,
        grid=(n_tiles,),
        in_specs=[pl.BlockSpec((tg * SUBLANES, W), row),
                  pl.BlockSpec((None, D, F2), exp3),
                  pl.BlockSpec((None, 1, F2), exp3),
                  pl.BlockSpec((None, F, D), exp3),
                  pl.BlockSpec((None, 1, D), exp3)],
        out_specs=pl.BlockSpec((tg * SUBLANES, W), row),
        scratch_shapes=[pltpu.VMEM((D, F2), BF16), pltpu.VMEM((F, D), BF16)],
    )
    return pl.pallas_call(
        functools.partial(_experts_body, F=F, tg=tg),
        grid_spec=grid_spec,
        out_shape=jax.ShapeDtypeStruct((n_rows, W), F32),
        compiler_params=_cparams(("arbitrary",)),
        name="moe_experts",
    )(te, n_used, xs, wgu, bgu, wd, bd)


def _combine_body(pos_ref, route_ref, h_ref, y_hbm, o_ref, pos_smem, buf_ref, sem_p, sem_d, *, tm):
    i = pl.program_id(0)
    n = pl.num_programs(0)
    slot = i & 1

    def gather_rows(step, s):
        cp = pltpu.make_async_copy(pos_ref.at[:, pl.ds(pl.multiple_of(step * tm, tm), tm)],
                                   pos_smem.at[s], sem_p)
        cp.start()
        cp.wait()
        return functools.partial(row_copy, s)

    def row_copy(s, t, kk):
        src = pl.multiple_of(pos_smem[s, kk, t], SUBLANES)
        return pltpu.make_async_copy(
            y_hbm.at[pl.ds(src, SUBLANES)],
            buf_ref.at[s, kk, pl.ds(pl.multiple_of(t * SUBLANES, SUBLANES), SUBLANES)], sem_d.at[s])

    @pl.when(i == 0)
    def _():
        _issue_rows(gather_rows(0, 0), tm)

    @pl.when(i + 1 < n)
    def _():
        _issue_rows(gather_rows(i + 1, 1 - slot), tm)

    _drain_rows(functools.partial(row_copy, slot), tm)

    route = route_ref[...]
    lane = lax.broadcasted_iota(I32, (tm, LANES), 1)
    ws = [jnp.sum(jnp.where(lane == TOP_K + kk, route, 0.0), axis=-1, keepdims=True)
          for kk in range(TOP_K)]
    for c in range(SUBLANES):
        sl = slice(c * LANES, (c + 1) * LANES)
        acc = h_ref[:, sl]
        for kk in range(TOP_K):
            acc = acc + ws[kk] * buf_ref[slot, kk, pl.ds(c, tm, stride=SUBLANES), :]
        o_ref[:, sl] = acc


def _combine(pos, route, h1, y, *, tm):
    T, D = h1.shape
    W = y.shape[1]
    return pl.pallas_call(
        functools.partial(_combine_body, tm=tm),
        grid=(T // tm,),
        in_specs=[pl.BlockSpec((SUBLANES, T), lambda i: (0, 0)),
                  pl.BlockSpec((tm, LANES), lambda i: (i, 0)),
                  pl.BlockSpec((tm, D), lambda i: (i, 0)),
                  pl.BlockSpec(memory_space=pl.ANY)],
        out_specs=pl.BlockSpec((tm, D), lambda i: (i, 0)),
        out_shape=jax.ShapeDtypeStruct((T, D), F32),
        scratch_shapes=[pltpu.SMEM((2, SUBLANES, tm), I32),
                        pltpu.VMEM((2, TOP_K, tm * SUBLANES, W), F32),
                        pltpu.SemaphoreType.DMA, pltpu.SemaphoreType.DMA((2,))],
        compiler_params=_cparams(("arbitrary",)),
        name="moe_combine",
    )(pos, route, h1, y)


def _blockdiag_dense(w, width):
    nb, bs, _ = w.shape
    per = width // bs
    wt = w.reshape(nb // per, per, bs, bs)
    eye = jnp.eye(per, dtype=w.dtype)
    dense = jnp.einsum('gpio,pq->gpiqo', wt, eye)
    return dense.reshape(nb // per, width, width)


def _rope_tables(S):
    half = DA_HEAD_DIM // 2
    inv = ROPE_THETA ** (-(jnp.arange(half, dtype=F32) * 2.0 / DA_HEAD_DIM))
    ang = jnp.arange(S, dtype=F32)[:, None] * inv[None, :]
    cos = jnp.tile(jnp.cos(ang), (1, LANES // half))
    sign = jnp.asarray(np.where(_HEAD_LANE_HALF == 0, -1.0, 1.0), F32)
    sin = jnp.tile(jnp.sin(ang), (1, LANES // half)) * sign[None, :]
    return cos, sin


_HEAD_LANE = np.arange(LANES)
_HEAD_LANE_HALF = _HEAD_LANE // (LANES // 2)
_HEAD_LANE_COMP = (_HEAD_LANE // (DA_HEAD_DIM // 2)) % 2
_HEAD_LANE_DIM = _HEAD_LANE_HALF * (DA_HEAD_DIM // 2) + _HEAD_LANE % (DA_HEAD_DIM // 2)
_HEAD_LANE_SRC = _HEAD_LANE_COMP * DA_HEAD_DIM + _HEAD_LANE_DIM


def _tile(n, pref):
    return pref if n % pref == 0 else n


def _layer(h2, mem2, B, S, lambda_init, attn_norm_g, w_in, b_gate, da_q_norm_g, da_k_norm_g,
           da_lambda_q1, da_lambda_k1, da_lambda_q2, da_lambda_k2, da_subln_g, ml_conv_w, ml_conv_b,
           ml_wq, ml_wk, ml_wv, ml_w_if, ml_b_if, ml_out_norm_g, ml_skip, mem_norm_g, ca_w_kv,
           ca_q_norm_g, ca_k_norm_g, w_branch_da, w_branch_ml, w_branch_ca, w_out, ffn_norm_g,
           router_w, router_b, w_gate_up, b_gate_up, w_down, b_down):
    T, D = h2.shape
    M = mem2.shape[0] // B
    QK = DA_HEADS * 2 * DA_HEAD_DIM
    VW = DA_HEADS * DA_V_DIM
    C = ml_conv_w.shape[1]
    CAW = ca_w_kv.shape[1] // 2
    E = router_w.shape[1]

    o = [0, QK, 2 * QK, 2 * QK + VW, 2 * QK + VW + C, 2 * QK + VW + 2 * C, 2 * QK + VW + 2 * C + CAW]
    head_perm = (np.arange(DA_HEADS)[:, None] * LANES + _HEAD_LANE_SRC[None, :]).reshape(-1)
    w_re = jnp.concatenate([w_in[:, o[0]:o[1]][:, head_perm], w_in[:, o[1]:o[2]][:, head_perm],
                            w_in[:, o[2]:o[3]], w_in[:, o[5]:o[6]], w_in[:, o[3]:o[5]], w_in[:, o[6]:]],
                           axis=1).astype(BF16)
    tn = 1024
    col_v, col_caq, col_mlx, col_mlz, col_gate = 2 * QK, 2 * QK + VW, 2 * QK + VW + CAW, \
        2 * QK + VW + CAW + C, 2 * QK + VW + CAW + 2 * C
    proj = _inproj(h2, attn_norm_g[None, :], w_re, tm=_tile(T, 2048), tn=tn)

    g2 = jnp.stack([da_q_norm_g[_HEAD_LANE_DIM], da_k_norm_g[_HEAD_LANE_DIM]])[:, None, :]
    cos_t, sin_t = _rope_tables(S)
    gm = jnp.asarray(_HEAD_LANE_COMP[:, None] == _HEAD_LANE_COMP[None, :], BF16)
    qk = _qkrope(proj, g2, cos_t, sin_t, gm, T=T, S=S, width=QK, tm=_tile(S, 1024))
    y_da = _diff_attention(qk, proj, da_lambda_q1[None, :], da_lambda_k1[None, :], da_lambda_q2[None, :],
                           da_lambda_k2[None, :], da_subln_g[None, :], da_q_norm_g[None, :],
                           da_k_norm_g[None, :], B=B, S=S, tq=_tile(S, 512),
                           v_col0=col_v // LANES, q_scale=DA_Q_SCALE, lambda_init=lambda_init)

    bd = jnp.stack([_blockdiag_dense(ml_wq, MXU_DIM), _blockdiag_dense(ml_wk, MXU_DIM),
                    _blockdiag_dense(ml_wv, MXU_DIM)]).astype(BF16)
    wif = jnp.pad(ml_w_if.reshape(3, C, 2 * ML_HEADS), ((0, 0), (0, 0), (0, LANES - 2 * ML_HEADS))).astype(BF16)
    bif = jnp.pad(ml_b_if, (0, LANES - 2 * ML_HEADS))[None, :]
    xc, mq, mk, mv, gates = _mlpre(proj, ml_conv_w, ml_conv_b[None, :], bd, wif, bif, B=B, S=S, C=C,
                                   tm=_tile(S, 512), x_col=col_mlx // C)
    dml = C // ML_HEADS
    y_ml = _mlstm(mq, mk, mv, gates, xc, proj, ml_out_norm_g[None, :], ml_skip[None, :], B=B, S=S, C=C,
                  L=_tile(S, 256), z_col0=col_mlz // dml, hpb=4)

    kmem, vmem = _memkv(mem2, mem_norm_g[None, :], ca_w_kv.astype(BF16), ca_k_norm_g[None, :], B=B, M=M, W=CAW)
    y_ca = _xattn(proj, kmem, vmem, ca_q_norm_g[None, :], B=B, S=S, M=M, W=CAW, tm=_tile(S, 512),
                  q_col=col_caq // CAW)

    tmx = _tile(T, 512)
    tri = (jnp.arange(tmx)[None, :] < jnp.arange(tmx)[:, None]).astype(BF16)
    rw = jnp.pad(router_w, ((0, 0), (0, LANES - E))).astype(BF16)
    rb = jnp.pad(router_b, (0, LANES - E))[None, :]
    h1, xp, route, cnt = _mix(h2, y_da, y_ml, y_ca, proj, b_gate.reshape(N_BRANCH, 1, D),
                              w_branch_da.astype(BF16), w_branch_ml.astype(BF16),
                              w_branch_ca.astype(BF16), w_out.astype(BF16), ffn_norm_g[None, :], rw, rb, tri,
                              tm=tmx, g_col=col_gate // D, n_exp=E)

    tg = _tile(T, 512)
    n_tiles = (T * TOP_K) // tg + E
    assert D == SUBLANES * LANES, "MoE rows are moved as one (8,128) f32 tile each"
    pos, te, n_used, xs = _scatter(route, cnt, xp, tm=_tile(T, 512), tg=tg, n_exp=E, n_tiles=n_tiles)
    y = _experts(te[:, 0], n_used[0, :1], xs, w_gate_up, b_gate_up[:, None, :],
                 w_down, b_down[:, None, :], tg=tg, n_tiles=n_tiles)
    return _combine(pos, route, h1, y, tm=_tile(T, 256))


def kernel(x, mem, attn_norm_g, w_in, b_gate, da_q_norm_g, da_k_norm_g, da_lambda_q1, da_lambda_k1, da_lambda_q2, da_lambda_k2, da_subln_g, ml_conv_w, ml_conv_b, ml_wq, ml_wk, ml_wv, ml_w_if, ml_b_if, ml_out_norm_g, ml_skip, mem_norm_g, ca_w_kv, ca_q_norm_g, ca_k_norm_g, w_branch_da, w_branch_ml, w_branch_ca, w_out, ffn_norm_g, router_w, router_b, w_gate_up, b_gate_up, w_down, b_down):
    B, S, D = x.shape
    depth = w_in.shape[0]
    h2 = x.reshape(B * S, D)
    mem2 = mem.reshape(B * mem.shape[1], D)
    params = (attn_norm_g, w_in, b_gate, da_q_norm_g, da_k_norm_g, da_lambda_q1, da_lambda_k1,
              da_lambda_q2, da_lambda_k2, da_subln_g, ml_conv_w, ml_conv_b, ml_wq, ml_wk, ml_wv, ml_w_if,
              ml_b_if, ml_out_norm_g, ml_skip, mem_norm_g, ca_w_kv, ca_q_norm_g, ca_k_norm_g, w_branch_da,
              w_branch_ml, w_branch_ca, w_out, ffn_norm_g, router_w, router_b, w_gate_up, b_gate_up,
              w_down, b_down)
    for l in range(depth):
        lambda_init = 0.8 - 0.6 * math.exp(-0.3 * l)
        h2 = _layer(h2, mem2, B, S, lambda_init, *[p[l] for p in params])
    return h2.reshape(B, S, D)
```

```python
import functools
import math

import jax
import jax.numpy as jnp
import numpy as np
from jax import lax
from jax.experimental import pallas as pl
from jax.experimental.pallas import tpu as pltpu

F32 = jnp.float32
BF16 = jnp.bfloat16
I32 = jnp.int32
U32 = jnp.uint32

NORM_EPS = 1e-6
ROPE_THETA = 10000.0
CHUNK = 64

DA_HEADS = 8
DA_HEAD_DIM = 64
DA_V_DIM = 128
ML_HEADS = 4
ML_CONV = 4
ML_QKV_BLOCK = 4
CA_HEADS = 4
N_BRANCH = 3
TOP_K = 4
SWIGLU_LIMIT = 7.0
SWIGLU_ALPHA = 1.702

LANES = 128
SUBLANES = 8
MXU_DIM = 256
VMEM_LIMIT = 56 * 1024 * 1024
NEG_BIG = -1e30
DA_Q_SCALE = DA_HEAD_DIM ** -0.5 * math.log2(math.e)
DA_BOUND_SLACK = 1.01
DA_BOUND_LIMIT = 40.0


def _cparams(sem):
    return pltpu.CompilerParams(dimension_semantics=sem, vmem_limit_bytes=VMEM_LIMIT)


def _sigmoid(x):
    return 0.5 * jnp.tanh(0.5 * x) + 0.5


def _silu(x):
    return x * _sigmoid(x)


def _inproj_body(x_ref, g_ref, w_ref, o_ref, xn_ref):
    j = pl.program_id(1)

    @pl.when(j == 0)
    def _():
        x = x_ref[...]
        ms = jnp.mean(x * x, axis=-1, keepdims=True)
        xn_ref[...] = (x * lax.rsqrt(ms + NORM_EPS) * g_ref[...]).astype(BF16)

    o_ref[...] = jnp.dot(xn_ref[...], w_ref[...], preferred_element_type=F32).astype(o_ref.dtype)


def _inproj(x2, g, w_bf, *, tm, tn):
    T, D = x2.shape
    N = w_bf.shape[1]
    return pl.pallas_call(
        _inproj_body,
        grid=(T // tm, N // tn),
        in_specs=[
            pl.BlockSpec((tm, D), lambda i, j: (i, 0)),
            pl.BlockSpec((1, D), lambda i, j: (0, 0)),
            pl.BlockSpec((D, tn), lambda i, j: (0, j)),
        ],
        out_specs=pl.BlockSpec((tm, tn), lambda i, j: (i, j)),
        out_shape=jax.ShapeDtypeStruct((T, N), BF16),
        scratch_shapes=[pltpu.VMEM((tm, D), BF16)],
        compiler_params=_cparams(("arbitrary", "arbitrary")),
        name="inproj",
    )(x2, g, w_bf)


def _group_sumsq(xb, gm):
    sq = xb * xb
    hi = sq.astype(BF16)
    lo = (sq - hi.astype(F32)).astype(BF16)
    return (jnp.dot(hi, gm, preferred_element_type=F32)
            + jnp.dot(lo, gm, preferred_element_type=F32))


def _qkrope_body(x_ref, g_ref, cos_ref, sin_ref, gm_ref, o_ref, *, ncb, scale):
    c = pl.program_id(1)
    g = g_ref[...] * jnp.where(c == 0, scale, 1.0).astype(F32)
    cos = cos_ref[...]
    sin = sin_ref[...]
    gm = gm_ref[...]
    for cb in range(ncb):
        sl = slice(cb * LANES, (cb + 1) * LANES)
        xb = x_ref[:, sl].astype(F32)
        ss = _group_sumsq(xb, gm)
        y = xb * lax.rsqrt(ss * (1.0 / DA_HEAD_DIM) + NORM_EPS) * g
        sw = pltpu.roll(y, LANES // 2, 1)
        o_ref[:, sl] = (y * cos + sw * sin).astype(o_ref.dtype)


def _qkrope(proj, g2, cos_t, sin_t, gm, *, T, S, width, tm):
    n_s = S // tm
    return pl.pallas_call(
        functools.partial(_qkrope_body, ncb=width // LANES, scale=DA_Q_SCALE),
        grid=(T // tm, 2),
        in_specs=[
            pl.BlockSpec((tm, width), lambda i, c: (i, c)),
            pl.BlockSpec((None, 1, LANES), lambda i, c: (c, 0, 0)),
            pl.BlockSpec((tm, LANES), lambda i, c: (i % n_s, 0)),
            pl.BlockSpec((tm, LANES), lambda i, c: (i % n_s, 0)),
            pl.BlockSpec((LANES, LANES), lambda i, c: (0, 0)),
        ],
        out_specs=pl.BlockSpec((tm, width), lambda i, c: (i, c)),
        out_shape=jax.ShapeDtypeStruct((T, 2 * width), BF16),
        compiler_params=_cparams(("arbitrary", "arbitrary")),
        name="qkrope",
    )(proj, g2, cos_t, sin_t, gm)


def _da_body(q_ref, k_ref, v_ref, lq1_ref, lk1_ref, lq2_ref, lk2_ref, subg_ref, qg_ref, kg_ref, o_ref,
             m_ref, acc_ref, kt_ref, *, tq, n_kv, q_scale, lambda_init):
    i = pl.program_id(2)
    q = q_ref[...]
    lane = lax.broadcasted_iota(I32, (1, LANES), 1)
    lo = (lane & (DA_HEAD_DIM // 2)) == 0
    zero = jnp.zeros_like(q)
    qc = (jnp.where(lo, q, zero), jnp.where(lo, zero, q))

    @pl.when(i == 0)
    def _():
        def tbody(j, carry):
            start = pl.multiple_of(j * tq, tq)
            kt_ref[:, pl.ds(start, tq)] = k_ref[pl.ds(start, tq), :].astype(F32).T.astype(BF16)
            return carry
        lax.fori_loop(0, n_kv, tbody, 0)

    bound = (DA_HEAD_DIM * q_scale * DA_BOUND_SLACK
             * jnp.max(jnp.abs(qg_ref[...]), keepdims=True) * jnp.max(jnp.abs(kg_ref[...]), keepdims=True))
    bound_max = jnp.max(bound)

    acc_ref[...] = jnp.zeros(acc_ref.shape, F32)
    ones = jnp.ones((tq, LANES), BF16)

    def scores(j):
        ktj = kt_ref[:, pl.ds(pl.multiple_of(j * tq, tq), tq)]
        return [jnp.dot(qc[c], ktj, preferred_element_type=F32) for c in range(2)]

    def diag_mask():
        r = lax.broadcasted_iota(I32, (tq, tq), 0)
        cidx = lax.broadcasted_iota(I32, (tq, tq), 1)
        return (cidx | (CHUNK - 1)) <= (r | (CHUNK - 1))

    def v_aug(j):
        return jnp.concatenate([v_ref[pl.ds(pl.multiple_of(j * tq, tq), tq), :], ones], axis=1)

    def consume_bounded(j, ss, masked):
        del masked
        vj = v_aug(j)
        ps = [jnp.exp2(ss[c] - bound).astype(BF16) for c in range(2)]
        pvs = [jnp.dot(ps[c], vj, preferred_element_type=F32) for c in range(2)]
        for c in range(2):
            acc_ref[c] = acc_ref[c] + pvs[c]

    def consume_online(j, ss, masked):
        vj = v_aug(j)
        ps, alphas = [], []
        for c in range(2):
            s = jnp.where(diag_mask(), ss[c], NEG_BIG) if masked else ss[c]
            m_old = m_ref[c]
            m_new = jnp.maximum(m_old, jnp.max(s, axis=-1, keepdims=True))
            ps.append(jnp.exp2(s - jnp.concatenate([m_new] * (tq // LANES), axis=1)).astype(BF16))
            alphas.append(jnp.exp2(m_old - m_new))
            m_ref[c] = m_new
        pvs = [jnp.dot(ps[c], vj, preferred_element_type=F32) for c in range(2)]
        for c in range(2):
            acc_ref[c] = jnp.concatenate([alphas[c], alphas[c]], axis=1) * acc_ref[c] + pvs[c]

    def diag_bounded():
        hq = tq // 2
        start = pl.multiple_of(i * tq, tq)
        kt_i = kt_ref[:, pl.ds(start, tq)]
        v_i = v_aug(i)
        r = lax.broadcasted_iota(I32, (hq, tq), 0)
        cidx = lax.broadcasted_iota(I32, (hq, tq), 1)
        mask_top = ((cidx | (CHUNK - 1)) <= (r | (CHUNK - 1)))[:, :hq]
        mask_bot = (cidx | (CHUNK - 1)) <= ((r + hq) | (CHUNK - 1))
        s_top = [jnp.dot(qc[c][:hq], kt_i[:, :hq], preferred_element_type=F32) for c in range(2)]
        s_bot = [jnp.dot(qc[c][hq:], kt_i, preferred_element_type=F32) for c in range(2)]
        p_top = [jnp.where(mask_top, jnp.exp2(s_top[c] - bound), 0.0).astype(BF16) for c in range(2)]
        p_bot = [jnp.where(mask_bot, jnp.exp2(s_bot[c] - bound), 0.0).astype(BF16) for c in range(2)]
        pv_top = [jnp.dot(p_top[c], v_i[:hq], preferred_element_type=F32) for c in range(2)]
        pv_bot = [jnp.dot(p_bot[c], v_i, preferred_element_type=F32) for c in range(2)]
        for c in range(2):
            acc_ref[c, pl.ds(0, hq)] = acc_ref[c, pl.ds(0, hq)] + pv_top[c]
            acc_ref[c, pl.ds(hq, hq)] = acc_ref[c, pl.ds(hq, hq)] + pv_bot[c]

    def sweep(consume, diag):
        def pair(jj, carry):
            sa = scores(2 * jj)
            sb = scores(2 * jj + 1)
            consume(2 * jj, sa, False)
            consume(2 * jj + 1, sb, False)
            return carry

        lax.fori_loop(0, lax.shift_right_logical(i, 1), pair, 0)

        @pl.when((i & 1) == 1)
        def _():
            consume(i - 1, scores(i - 1), False)

        diag()

    @pl.when(bound_max < DA_BOUND_LIMIT)
    def _():
        sweep(consume_bounded, diag_bounded)

    @pl.when(jnp.logical_not(bound_max < DA_BOUND_LIMIT))
    def _():
        m_ref[...] = jnp.full(m_ref.shape, NEG_BIG, F32)
        sweep(consume_online, lambda: consume_online(i, scores(i), True))

    f32 = F32
    lam = (jnp.exp(jnp.sum(lq1_ref[...].astype(f32) * lk1_ref[...].astype(f32), keepdims=True))
           - jnp.exp(jnp.sum(lq2_ref[...].astype(f32) * lk2_ref[...].astype(f32), keepdims=True))
           + lambda_init)
    a1 = acc_ref[0]
    a2 = acc_ref[1]
    o = a1[:, :DA_V_DIM] / a1[:, DA_V_DIM:] - lam * (a2[:, :DA_V_DIM] / a2[:, DA_V_DIM:])
    ms = jnp.mean(o * o, axis=-1, keepdims=True)
    o = o * lax.rsqrt(ms + NORM_EPS) * subg_ref[...]
    o_ref[...] = (o * (1.0 - lambda_init)).astype(o_ref.dtype)


def _diff_attention(qk, proj, lq1, lk1, lq2, lk2, subg, qg, kg, *, B, S, tq, v_col0, q_scale, lambda_init):
    T = B * S
    H = DA_HEADS
    n_q = S // tq
    vec = lambda n: pl.BlockSpec((1, n), lambda b, h, i: (0, 0))
    return pl.pallas_call(
        functools.partial(_da_body, tq=tq, n_kv=n_q, q_scale=q_scale, lambda_init=lambda_init),
        grid=(B, H, n_q),
        in_specs=[
            pl.BlockSpec((tq, LANES), lambda b, h, i: (b * n_q + i, h)),
            pl.BlockSpec((S, LANES), lambda b, h, i: (b, H + h)),
            pl.BlockSpec((S, LANES), lambda b, h, i: (b, v_col0 + h)),
            vec(DA_HEAD_DIM), vec(DA_HEAD_DIM), vec(DA_HEAD_DIM), vec(DA_HEAD_DIM),
            vec(DA_V_DIM), vec(DA_HEAD_DIM), vec(DA_HEAD_DIM),
        ],
        out_specs=pl.BlockSpec((tq, LANES), lambda b, h, i: (b * n_q + i, h)),
        out_shape=jax.ShapeDtypeStruct((T, H * DA_V_DIM), BF16),
        scratch_shapes=[pltpu.VMEM((2, tq, LANES), F32), pltpu.VMEM((2, tq, 2 * DA_V_DIM), F32),
                        pltpu.VMEM((LANES, S), BF16)],
        compiler_params=_cparams(("arbitrary", "arbitrary", "arbitrary")),
        name="diff_attn",
    )(qk, qk, proj, lq1, lk1, lq2, lk2, subg, qg, kg)


def _mlpre_body(x_ref, cw_ref, cb_ref, bd_ref, wif_ref, bif_ref,
                xc_ref, q_ref, k_ref, v_ref, g_ref, prev_ref, *, tm, ncb, kscale):
    s_idx = pl.program_id(1)

    @pl.when(s_idx == 0)
    def _():
        prev_ref[...] = jnp.zeros(prev_ref.shape, F32)

    row8 = lax.broadcasted_iota(I32, (SUBLANES, MXU_DIM), 0)
    gacc = jnp.zeros((tm, LANES), F32) + bif_ref[...]
    for cb in range(ncb):
        sl = slice(cb * MXU_DIM, (cb + 1) * MXU_DIM)
        xb16 = x_ref[:, sl]
        x = xb16.astype(F32)
        prev = prev_ref[:, sl]
        conv = x * cw_ref[ML_CONV - 1:ML_CONV, sl] + cb_ref[:, sl]
        for d in range(1, ML_CONV):
            xs = pltpu.roll(x, d, 0)
            ps = pltpu.roll(prev, d, 0)
            head = jnp.where(row8 < d, ps, xs[:SUBLANES])
            shifted = jnp.concatenate([head, xs[SUBLANES:]], axis=0)
            conv = conv + shifted * cw_ref[ML_CONV - 1 - d:ML_CONV - d, sl]
        prev_ref[:, sl] = x[tm - SUBLANES:]
        xc = _silu(conv)
        xc16 = xc.astype(BF16)
        q = jnp.dot(xc16, bd_ref[0, cb], preferred_element_type=F32)
        k = jnp.dot(xc16, bd_ref[1, cb], preferred_element_type=F32)
        v = jnp.dot(xb16, bd_ref[2, cb], preferred_element_type=F32)
        q16, k16, v16 = q.astype(BF16), k.astype(BF16), v.astype(BF16)
        gacc = gacc + jnp.dot(q16, wif_ref[0, sl, :], preferred_element_type=F32)
        gacc = gacc + jnp.dot(k16, wif_ref[1, sl, :], preferred_element_type=F32)
        gacc = gacc + jnp.dot(v16, wif_ref[2, sl, :], preferred_element_type=F32)
        xc_ref[:, sl] = xc16
        q_ref[:, sl] = q16
        k_ref[:, sl] = (k * kscale).astype(BF16)
        v_ref[:, sl] = v16
    g_ref[...] = gacc


def _mlpre(proj, conv_w, conv_b, bd, wif, bif, *, B, S, C, tm, x_col):
    T = B * S
    n_s = S // tm
    row = lambda i: pl.BlockSpec((tm, C), lambda b, s: (b * n_s + s, i))
    outs = pl.pallas_call(
        functools.partial(_mlpre_body, tm=tm, ncb=C // MXU_DIM, kscale=(C // ML_HEADS) ** -0.5),
        grid=(B, n_s),
        in_specs=[
            row(x_col),
            pl.BlockSpec((ML_CONV, C), lambda b, s: (0, 0)),
            pl.BlockSpec((1, C), lambda b, s: (0, 0)),
            pl.BlockSpec(bd.shape, lambda b, s: (0, 0, 0, 0)),
            pl.BlockSpec(wif.shape, lambda b, s: (0, 0, 0)),
            pl.BlockSpec((1, LANES), lambda b, s: (0, 0)),
        ],
        out_specs=[row(0), row(0), row(0), row(0),
                   pl.BlockSpec((tm, LANES), lambda b, s: (b * n_s + s, 0))],
        out_shape=[jax.ShapeDtypeStruct((T, C), BF16)] * 4 + [jax.ShapeDtypeStruct((T, LANES), F32)],
        scratch_shapes=[pltpu.VMEM((SUBLANES, C), F32)],
        compiler_params=_cparams(("arbitrary", "arbitrary")),
        name="mlstm_pre",
    )(proj, conv_w, conv_b, bd, wif, bif)
    return outs


def _mlstm_body(q_ref, k_ref, v_ref, g_ref, xc_ref, z_ref, ng_ref, skip_ref, o_ref,
                c_ref, cb_ref, n_ref, m_ref, *, L, d, hpb):
    hg = pl.program_id(1)
    c_idx = pl.program_id(2)

    @pl.when(c_idx == 0)
    def _():
        c_ref[...] = jnp.zeros(c_ref.shape, F32)
        cb_ref[...] = jnp.zeros(cb_ref.shape, BF16)
        n_ref[...] = jnp.zeros(n_ref.shape, F32)
        m_ref[...] = jnp.zeros(m_ref.shape, F32)

    g = g_ref[...]
    gt = g.T
    lane = lax.broadcasted_iota(I32, (1, LANES), 1)
    sub = lax.broadcasted_iota(I32, (LANES, 1), 0)
    t_idx = lax.broadcasted_iota(I32, (L, L), 0)
    s_idx = lax.broadcasted_iota(I32, (L, L), 1)
    tri = s_idx <= t_idx

    for hh in range(hpb):
        h = hg * hpb + hh
        sl = slice(hh * d, (hh + 1) * d)
        i_col = jnp.sum(jnp.where(lane == h, g, 0.0), axis=1, keepdims=True)
        f_col = jnp.sum(jnp.where(lane == ML_HEADS + h, g, 0.0), axis=1, keepdims=True)
        i_row = jnp.sum(jnp.where(sub == h, gt, 0.0), axis=0, keepdims=True)
        f_row = jnp.sum(jnp.where(sub == ML_HEADS + h, gt, 0.0), axis=0, keepdims=True)
        lf_col = jax.nn.log_sigmoid(f_col)
        lf_row = jax.nn.log_sigmoid(f_row)

        b_col = jnp.sum(jnp.where(tri, lf_row, 0.0), axis=1, keepdims=True)
        b_row = jnp.sum(jnp.where(t_idx <= s_idx, lf_col, 0.0), axis=0, keepdims=True)
        b_last = jnp.sum(lf_row, axis=1, keepdims=True)

        m_prev = m_ref[hh]
        dlog = jnp.where(tri, b_col - b_row + i_row, NEG_BIG)
        inter_log = b_col + m_prev
        m_rowv = jnp.maximum(inter_log, jnp.max(dlog, axis=1, keepdims=True))
        dw = jnp.exp(dlog - m_rowv)
        inter_w = jnp.exp(inter_log - m_rowv)

        q = q_ref[:, sl]
        k = k_ref[:, sl]
        v = v_ref[:, sl]
        qk = lax.dot_general(q, k, (((1,), (1,)), ((), ())), preferred_element_type=F32)
        qc = jnp.dot(q, cb_ref[hh], preferred_element_type=F32)
        s = qk * dw

        state_row = b_last - b_row + i_row
        state_col = b_last - b_col + i_col
        m_new = jnp.maximum(b_last + m_prev, jnp.max(state_row, axis=1, keepdims=True))
        decay = jnp.exp(b_last + m_prev - m_new)
        ws_col = jnp.exp(state_col - m_new)
        kwf = k.astype(F32) * ws_col

        num = jnp.dot(s.astype(BF16), v, preferred_element_type=F32) + inter_w * qc
        dc = lax.dot_general(kwf.astype(BF16), v, (((0,), (0,)), ((), ())), preferred_element_type=F32)
        den = (jnp.sum(s, axis=1, keepdims=True)
               + inter_w * jnp.sum(q.astype(F32) * n_ref[hh], axis=1, keepdims=True))
        hv = num / jnp.maximum(jnp.abs(den), jnp.exp(-m_rowv))

        c_new = decay * c_ref[hh] + dc
        c_ref[hh] = c_new
        cb_ref[hh] = c_new.astype(BF16)
        n_ref[hh] = decay * n_ref[hh] + jnp.sum(kwf, axis=0, keepdims=True)
        m_ref[hh] = m_new

        mu = jnp.mean(hv, axis=-1, keepdims=True)
        cen = hv - mu
        var = jnp.mean(cen * cen, axis=-1, keepdims=True)
        hn = cen * lax.rsqrt(var + NORM_EPS) * ng_ref[:, sl]
        y = (hn + skip_ref[:, sl] * xc_ref[:, sl].astype(F32)) * _silu(z_ref[:, sl].astype(F32))
        o_ref[:, sl] = y.astype(o_ref.dtype)


def _mlstm(q, k, v, gates, xc, proj, norm_g, skip, *, B, S, C, L, z_col0, hpb):
    T = B * S
    H = ML_HEADS
    d = C // H
    w = hpb * d
    nc = S // L
    blk = lambda off: pl.BlockSpec((L, w), lambda b, h, c: (b * nc + c, off + h))
    par = pl.BlockSpec((1, w), lambda b, h, c: (0, h))
    return pl.pallas_call(
        functools.partial(_mlstm_body, L=L, d=d, hpb=hpb),
        grid=(B, H // hpb, nc),
        in_specs=[blk(0), blk(0), blk(0),
                  pl.BlockSpec((L, LANES), lambda b, h, c: (b * nc + c, 0)),
                  blk(0), blk(z_col0 // hpb), par, par],
        out_specs=blk(0),
        out_shape=jax.ShapeDtypeStruct((T, C), BF16),
        scratch_shapes=[pltpu.VMEM((hpb, d, d), F32), pltpu.VMEM((hpb, d, d), BF16),
                        pltpu.VMEM((hpb, 1, d), F32), pltpu.VMEM((hpb, 1, 1), F32)],
        compiler_params=_cparams(("arbitrary", "arbitrary", "arbitrary")),
        name="mlstm_scan",
    )(q, k, v, gates, xc, proj, norm_g, skip)


def _memkv_body(mem_ref, mg_ref, w_ref, kg_ref, k_ref, v_ref, *, W, dh):
    x = mem_ref[...]
    ms = jnp.mean(x * x, axis=-1, keepdims=True)
    xn = (x * lax.rsqrt(ms + NORM_EPS) * mg_ref[...]).astype(BF16)
    kv = jnp.dot(xn, w_ref[...], preferred_element_type=F32)
    for hd in range(W // dh):
        sl = slice(hd * dh, (hd + 1) * dh)
        kh = kv[:, sl]
        msk = jnp.mean(kh * kh, axis=-1, keepdims=True)
        k_ref[:, sl] = (kh * lax.rsqrt(msk + NORM_EPS) * kg_ref[...]).astype(BF16)
    v_ref[...] = kv[:, W:].astype(BF16)


def _memkv(mem2, mem_g, w_kv_bf, k_g, *, B, M, W):
    D = mem2.shape[1]
    dh = W // CA_HEADS
    return pl.pallas_call(
        functools.partial(_memkv_body, W=W, dh=dh),
        grid=(B,),
        in_specs=[pl.BlockSpec((M, D), lambda b: (b, 0)),
                  pl.BlockSpec((1, D), lambda b: (0, 0)),
                  pl.BlockSpec((D, 2 * W), lambda b: (0, 0)),
                  pl.BlockSpec((1, dh), lambda b: (0, 0))],
        out_specs=[pl.BlockSpec((M, W), lambda b: (b, 0))] * 2,
        out_shape=[jax.ShapeDtypeStruct((B * M, W), BF16)] * 2,
        compiler_params=_cparams(("arbitrary",)),
        name="mem_kv",
    )(mem2, mem_g, w_kv_bf, k_g)


def _xattn_body(q_ref, k_ref, v_ref, qg_ref, o_ref, *, W, dh):
    scale = dh ** -0.5
    for hd in range(W // dh):
        sl = slice(hd * dh, (hd + 1) * dh)
        qh = q_ref[:, sl].astype(F32)
        ms = jnp.mean(qh * qh, axis=-1, keepdims=True)
        qn = (qh * lax.rsqrt(ms + NORM_EPS) * (qg_ref[...] * scale)).astype(BF16)
        s = lax.dot_general(qn, k_ref[:, sl], (((1,), (1,)), ((), ())), preferred_element_type=F32)
        mx = jnp.max(s, axis=-1, keepdims=True)
        p = jnp.exp(s - mx)
        p = p / jnp.sum(p, axis=-1, keepdims=True)
        o_ref[:, sl] = jnp.dot(p.astype(BF16), v_ref[:, sl],
                               preferred_element_type=F32).astype(o_ref.dtype)


def _xattn(proj, kmem, vmem, q_g, *, B, S, M, W, tm, q_col):
    T = B * S
    n_s = S // tm
    dh = W // CA_HEADS
    return pl.pallas_call(
        functools.partial(_xattn_body, W=W, dh=dh),
        grid=(T // tm,),
        in_specs=[pl.BlockSpec((tm, W), lambda i: (i, q_col)),
                  pl.BlockSpec((M, W), lambda i: (i // n_s, 0)),
                  pl.BlockSpec((M, W), lambda i: (i // n_s, 0)),
                  pl.BlockSpec((1, dh), lambda i: (0, 0))],
        out_specs=pl.BlockSpec((tm, W), lambda i: (i, 0)),
        out_shape=jax.ShapeDtypeStruct((T, W), BF16),
        compiler_params=_cparams(("arbitrary",)),
        name="mem_xattn",
    )(proj, kmem, vmem, q_g)


def _store_tile_rows(ref, x):
    n = x.shape[0]
    for c in range(SUBLANES):
        ref[pl.ds(c, n, stride=SUBLANES), :] = x[:, c * LANES:(c + 1) * LANES]


def _load_tile_rows(ref, n):
    return [ref[pl.ds(c, n, stride=SUBLANES), :] for c in range(SUBLANES)]


def _mix_body(x_ref, yda_ref, yml_ref, yca_ref, gda_ref, gml_ref, gca_ref, bg_ref, wda_ref, wml_ref, wca_ref,
              wout_ref, fg_ref, rw_ref, rb_ref, tri_ref,
              h_ref, xp_ref, route_ref, cnt_ref, carry_ref, *, n_exp):
    i = pl.program_id(0)

    @pl.when(i == 0)
    def _():
        carry_ref[...] = jnp.zeros(carry_ref.shape, F32)

    def gate(g_ref, b):
        return _sigmoid(g_ref[...].astype(F32) + bg_ref[b])

    mix = (gate(gda_ref, 0) * jnp.dot(yda_ref[...], wda_ref[...], preferred_element_type=F32)
           + gate(gml_ref, 1) * jnp.dot(yml_ref[...], wml_ref[...], preferred_element_type=F32)
           + gate(gca_ref, 2) * jnp.dot(yca_ref[...], wca_ref[...], preferred_element_type=F32))
    h1 = x_ref[...] + jnp.dot(mix.astype(BF16), wout_ref[...], preferred_element_type=F32)
    h_ref[...] = h1
    ms = jnp.mean(h1 * h1, axis=-1, keepdims=True)
    xn = h1 * lax.rsqrt(ms + NORM_EPS) * fg_ref[...]
    _store_tile_rows(xp_ref, xn)

    logits = jnp.dot(xn.astype(BF16), rw_ref[...], preferred_element_type=F32) + rb_ref[...]
    tm = logits.shape[0]
    lane = lax.broadcasted_iota(I32, (tm, LANES), 1)
    work = jnp.where(lane < n_exp, logits, NEG_BIG)
    sel = jnp.zeros((tm, LANES), F32)
    vals, idxs = [], []
    for _ in range(TOP_K):
        mx = jnp.max(work, axis=-1, keepdims=True)
        idx = jnp.min(jnp.where(work == mx, lane, LANES), axis=-1, keepdims=True)
        hit = lane == idx
        sel = jnp.where(hit, 1.0, sel)
        work = jnp.where(hit, NEG_BIG, work)
        vals.append(mx)
        idxs.append(idx)
    exps = [jnp.exp(v - vals[0]) for v in vals]
    tot = exps[0] + exps[1] + exps[2] + exps[3]

    cum = jnp.dot(tri_ref[...], sel.astype(BF16), preferred_element_type=F32) + carry_ref[...]
    route = jnp.zeros((tm, LANES), F32)
    for kk in range(TOP_K):
        rank = jnp.sum(jnp.where(lane == idxs[kk], cum, 0.0), axis=-1, keepdims=True)
        route = jnp.where(lane == kk, idxs[kk].astype(F32), route)
        route = jnp.where(lane == TOP_K + kk, exps[kk] / tot, route)
        route = jnp.where(lane == 2 * TOP_K + kk, rank, route)
    route_ref[...] = route
    carry_ref[...] = carry_ref[...] + jnp.sum(sel, axis=0, keepdims=True)
    cnt_ref[...] = carry_ref[...]


def _mix(x2, yda, yml, yca, proj, bg, wda, wml, wca, wout, fg, rw, rb, tri, *, tm, g_col, n_exp):
    T, D = x2.shape
    C = yml.shape[1]
    const = lambda shape: pl.BlockSpec(shape, lambda i: (0,) * len(shape))
    return pl.pallas_call(
        functools.partial(_mix_body, n_exp=n_exp),
        grid=(T // tm,),
        in_specs=[pl.BlockSpec((tm, D), lambda i: (i, 0)),
                  pl.BlockSpec((tm, D), lambda i: (i, 0)),
                  pl.BlockSpec((tm, C), lambda i: (i, 0)),
                  pl.BlockSpec((tm, D), lambda i: (i, 0)),
                  pl.BlockSpec((tm, D), lambda i: (i, g_col)),
                  pl.BlockSpec((tm, D), lambda i: (i, g_col + 1)),
                  pl.BlockSpec((tm, D), lambda i: (i, g_col + 2)),
                  const((N_BRANCH, 1, D)),
                  const((D, D)), const((C, D)), const((D, D)), const((D, D)),
                  const((1, D)), const((D, LANES)), const((1, LANES)), const((tm, tm))],
        out_specs=[pl.BlockSpec((tm, D), lambda i: (i, 0)),
                   pl.BlockSpec((tm * SUBLANES, LANES), lambda i: (i, 0)),
                   pl.BlockSpec((tm, LANES), lambda i: (i, 0)),
                   pl.BlockSpec((1, LANES), lambda i: (0, 0))],
        out_shape=[jax.ShapeDtypeStruct((T, D), F32),
                   jax.ShapeDtypeStruct((T * SUBLANES, LANES), F32),
                   jax.ShapeDtypeStruct((T, LANES), F32),
                   jax.ShapeDtypeStruct((1, LANES), F32)],
        scratch_shapes=[pltpu.VMEM((1, LANES), F32)],
        compiler_params=_cparams(("arbitrary",)),
        name="mix_route",
    )(x2, yda, yml, yca, proj, proj, proj, bg, wda, wml, wca, wout, fg, rw, rb, tri)


ROW_DMA_UNROLL = 8
PAD_CHUNK_LOG2 = 5
PAD_CHUNK = 1 << PAD_CHUNK_LOG2


def _issue_rows(row_copy, tm):
    def issue(t, carry):
        for kk in range(TOP_K):
            row_copy(t, kk).start(priority=kk % 2)
        return carry

    lax.fori_loop(0, tm, issue, 0, unroll=ROW_DMA_UNROLL)


def _drain_rows(row_copy, tm):
    def drain(t, carry):
        for kk in range(TOP_K):
            row_copy(t, kk).wait()
        return carry

    lax.fori_loop(0, tm, drain, 0, unroll=ROW_DMA_UNROLL)


def _issue_and_drain_rows(row_copy, tm):
    _issue_rows(row_copy, tm)
    _drain_rows(row_copy, tm)


def _group_starts(cnt, tg, n_exp):
    lane_r = lax.broadcasted_iota(I32, (LANES, LANES), 0)
    lane_c = lax.broadcasted_iota(I32, (LANES, LANES), 1)
    padded = jnp.ceil(cnt * (1.0 / tg)) * tg
    padded_col = jnp.sum(jnp.where(lane_r == lane_c, padded, 0.0), axis=1, keepdims=True)
    start = jnp.sum(jnp.where(lane_r < lane_c, padded_col, 0.0), axis=0, keepdims=True)
    return start, start + padded


def _scatter_body(route_ref, route_next_ref, cnt_ref, xp_ref, pos_ref, te_ref, nu_ref, xs_hbm,
                  posv_ref, pos_smem, padv_ref, pad_smem, zero_ref, sem_p, sem_d, sem_z,
                  *, tm, tg, n_exp, n_tiles):
    i = pl.program_id(0)
    n = pl.num_programs(0)
    slot = i & 1
    lane = lax.broadcasted_iota(I32, (tm, LANES), 1)
    start, end = _group_starts(cnt_ref[...], tg, n_exp)

    def stage_positions(route, s):
        posm = jnp.zeros((tm, LANES), F32)
        for kk in range(TOP_K):
            e = jnp.sum(jnp.where(lane == kk, route, 0.0), axis=-1, keepdims=True)
            rank = jnp.sum(jnp.where(lane == 2 * TOP_K + kk, route, 0.0), axis=-1, keepdims=True)
            st = jnp.sum(jnp.where(lane == e.astype(I32), start, 0.0), axis=-1, keepdims=True)
            posm = jnp.where(lane == kk, (st + rank) * SUBLANES, posm)
        posv_ref[s] = posm.astype(I32).T[:SUBLANES]
        cp = pltpu.make_async_copy(posv_ref.at[s], pos_smem.at[s], sem_p)
        cp.start()
        cp.wait()

    @pl.when(i == 0)
    def _():
        stage_positions(route_ref[...], 0)
        tstart = (lax.broadcasted_iota(I32, (n_tiles, LANES), 0) * tg).astype(F32)
        lane_t = lax.broadcasted_iota(I32, (n_tiles, LANES), 1)
        done = jnp.where((lane_t < n_exp) & (end <= tstart), 1.0, 0.0)
        te = jnp.sum(done, axis=-1, keepdims=True)
        te_ref[...] = jnp.broadcast_to(te, (n_tiles, LANES)).astype(I32)
        n_used = jnp.max(end, axis=-1, keepdims=True) * (1.0 / tg)
        nu_ref[...] = jnp.broadcast_to(n_used, (1, LANES)).astype(I32)

        cnt = cnt_ref[...]
        padv_ref[...] = jnp.zeros(padv_ref.shape, I32)
        padv_ref[0:1, :] = ((start + cnt) * SUBLANES).astype(I32)
        padv_ref[1:2, :] = (end - start - cnt).astype(I32)
        zero_ref[...] = jnp.zeros(zero_ref.shape, F32)
        cpz = pltpu.make_async_copy(padv_ref, pad_smem, sem_p)
        cpz.start()
        cpz.wait()

        def pad_copy(off, rows):
            n = rows * SUBLANES
            return pltpu.make_async_copy(zero_ref.at[pl.ds(0, n)],
                                         xs_hbm.at[pl.ds(pl.multiple_of(off, SUBLANES), n)], sem_z)

        def for_each_pad_copy(fn):
            def per_expert(e, carry):
                base = pad_smem[0, e]
                n_pad = pad_smem[1, e]
                n_chunks = lax.shift_right_logical(n_pad, PAD_CHUNK_LOG2)

                def per_chunk(c, c2):
                    fn(pad_copy(base + c * (PAD_CHUNK * SUBLANES), PAD_CHUNK))
                    return c2
                lax.fori_loop(0, n_chunks, per_chunk, 0)

                def per_row(r, c2):
                    fn(pad_copy(base + (n_chunks * PAD_CHUNK + r) * SUBLANES, 1))
                    return c2
                lax.fori_loop(0, n_pad - n_chunks * PAD_CHUNK, per_row, 0)
                return carry
            lax.fori_loop(0, n_exp, per_expert, 0)

        for_each_pad_copy(lambda cp: cp.start())
        for_each_pad_copy(lambda cp: cp.wait())

    def row_copy(t, kk):
        dst = pl.multiple_of(pos_smem[slot, kk, t], SUBLANES)
        return pltpu.make_async_copy(xp_ref.at[pl.ds(pl.multiple_of(t * SUBLANES, SUBLANES), SUBLANES)],
                                     xs_hbm.at[pl.ds(dst, SUBLANES)], sem_d)

    _issue_rows(row_copy, tm)

    @pl.when(i + 1 < n)
    def _():
        stage_positions(route_next_ref[...], 1 - slot)

    pos_ref[...] = posv_ref[slot]
    _drain_rows(row_copy, tm)


def _scatter(route, cnt, xp, *, tm, tg, n_exp, n_tiles):
    T = route.shape[0]
    W = xp.shape[1]
    n_rows = n_tiles * tg * SUBLANES
    n_steps = T // tm
    return pl.pallas_call(
        functools.partial(_scatter_body, tm=tm, tg=tg, n_exp=n_exp, n_tiles=n_tiles),
        grid=(n_steps,),
        in_specs=[pl.BlockSpec((tm, LANES), lambda i: (i, 0)),
                  pl.BlockSpec((tm, LANES), lambda i: (jnp.minimum(i + 1, n_steps - 1), 0)),
                  pl.BlockSpec((1, LANES), lambda i: (0, 0)),
                  pl.BlockSpec((tm * SUBLANES, W), lambda i: (i, 0))],
        out_specs=[pl.BlockSpec((SUBLANES, tm), lambda i: (0, i)),
                   pl.BlockSpec((n_tiles, LANES), lambda i: (0, 0)),
                   pl.BlockSpec((1, LANES), lambda i: (0, 0)),
                   pl.BlockSpec(memory_space=pl.ANY)],
        out_shape=[jax.ShapeDtypeStruct((SUBLANES, T), I32),
                   jax.ShapeDtypeStruct((n_tiles, LANES), I32),
                   jax.ShapeDtypeStruct((1, LANES), I32),
                   jax.ShapeDtypeStruct((n_rows, W), F32)],
        scratch_shapes=[pltpu.VMEM((2, SUBLANES, tm), I32), pltpu.SMEM((2, SUBLANES, tm), I32),
                        pltpu.VMEM((SUBLANES, LANES), I32), pltpu.SMEM((SUBLANES, LANES), I32),
                        pltpu.VMEM((PAD_CHUNK * SUBLANES, LANES), F32),
                        pltpu.SemaphoreType.DMA, pltpu.SemaphoreType.DMA, pltpu.SemaphoreType.DMA],
        compiler_params=_cparams(("arbitrary",)),
        name="moe_scatter",
    )(route, route, cnt, xp)


def _experts_body(te_ref, nu_ref, xs_ref, wgu_ref, bgu_ref, wd_ref, bd_ref, y_ref, wgu16_ref, wd16_ref,
                  *, F, tg):
    i = pl.program_id(0)
    live = i < nu_ref[0]
    new_expert = (i == 0) | (te_ref[i] != te_ref[jnp.maximum(i - 1, 0)])

    @pl.when(live & new_expert)
    def _():
        wgu16_ref[...] = wgu_ref[...].astype(BF16)
        wd16_ref[...] = wd_ref[...].astype(BF16)

    @pl.when(live)
    def _():
        x = jnp.concatenate([c.astype(BF16) for c in _load_tile_rows(xs_ref, tg)], axis=1)
        h = jnp.dot(x, wgu16_ref[...], preferred_element_type=F32) + bgu_ref[...]
        gate = jnp.minimum(h[:, :F], SWIGLU_LIMIT)
        up = jnp.clip(h[:, F:], -SWIGLU_LIMIT, SWIGLU_LIMIT)
        a = (up + 1.0) * (gate * _sigmoid(SWIGLU_ALPHA * gate))
        y = jnp.dot(a.astype(BF16), wd16_ref[...], preferred_element_type=F32) + bd_ref[...]
        _store_tile_rows(y_ref, y)


def _experts(te, n_used, xs, wgu, bgu, wd, bd, *, tg, n_tiles):
    n_rows, W = xs.shape
    E, D, F2 = wgu.shape
    F = F2 // 2
    row = lambda i, te, nu: (jnp.minimum(i, nu[0] - 1), 0)
    exp3 = lambda i, te, nu: (te[jnp.minimum(i, nu[0] - 1)], 0, 0)
    grid_spec = pltpu.PrefetchScalarGridSpec(
        num_scalar_prefetch=2,
        grid=(n_tiles,),
        in_specs=[pl.BlockSpec((tg * SUBLANES, W), row),
                  pl.BlockSpec((None, D, F2), exp3),
                  pl.BlockSpec((None, 1, F2), exp3),
                  pl.BlockSpec((None, F, D), exp3),
                  pl.BlockSpec((None, 1, D), exp3)],
        out_specs=pl.BlockSpec((tg * SUBLANES, W), row),
        scratch_shapes=[pltpu.VMEM((D, F2), BF16), pltpu.VMEM((F, D), BF16)],
    )
    return pl.pallas_call(
        functools.partial(_experts_body, F=F, tg=tg),
        grid_spec=grid_spec,
        out_shape=jax.ShapeDtypeStruct((n_rows, W), F32),
        compiler_params=_cparams(("arbitrary",)),
        name="moe_experts",
    )(te, n_used, xs, wgu, bgu, wd, bd)


def _combine_body(pos_ref, route_ref, h_ref, y_hbm, o_ref, pos_smem, buf_ref, sem_p, sem_d, *, tm):
    i = pl.program_id(0)
    n = pl.num_programs(0)
    slot = i & 1

    def gather_rows(step, s):
        cp = pltpu.make_async_copy(pos_ref.at[:, pl.ds(pl.multiple_of(step * tm, tm), tm)],
                                   pos_smem.at[s], sem_p)
        cp.start()
        cp.wait()
        return functools.partial(row_copy, s)

    def row_copy(s, t, kk):
        src = pl.multiple_of(pos_smem[s, kk, t], SUBLANES)
        return pltpu.make_async_copy(
            y_hbm.at[pl.ds(src, SUBLANES)],
            buf_ref.at[s, kk, pl.ds(pl.multiple_of(t * SUBLANES, SUBLANES), SUBLANES)], sem_d.at[s])

    @pl.when(i == 0)
    def _():
        _issue_rows(gather_rows(0, 0), tm)

    @pl.when(i + 1 < n)
    def _():
        _issue_rows(gather_rows(i + 1, 1 - slot), tm)

    _drain_rows(functools.partial(row_copy, slot), tm)

    route = route_ref[...]
    lane = lax.broadcasted_iota(I32, (tm, LANES), 1)
    ws = [jnp.sum(jnp.where(lane == TOP_K + kk, route, 0.0), axis=-1, keepdims=True)
          for kk in range(TOP_K)]
    for c in range(SUBLANES):
        sl = slice(c * LANES, (c + 1) * LANES)
        acc = h_ref[:, sl]
        for kk in range(TOP_K):
            acc = acc + ws[kk] * buf_ref[slot, kk, pl.ds(c, tm, stride=SUBLANES), :]
        o_ref[:, sl] = acc


def _combine(pos, route, h1, y, *, tm):
    T, D = h1.shape
    W = y.shape[1]
    return pl.pallas_call(
        functools.partial(_combine_body, tm=tm),
        grid=(T // tm,),
        in_specs=[pl.BlockSpec((SUBLANES, T), lambda i: (0, 0)),
                  pl.BlockSpec((tm, LANES), lambda i: (i, 0)),
                  pl.BlockSpec((tm, D), lambda i: (i, 0)),
                  pl.BlockSpec(memory_space=pl.ANY)],
        out_specs=pl.BlockSpec((tm, D), lambda i: (i, 0)),
        out_shape=jax.ShapeDtypeStruct((T, D), F32),
        scratch_shapes=[pltpu.SMEM((2, SUBLANES, tm), I32),
                        pltpu.VMEM((2, TOP_K, tm * SUBLANES, W), F32),
                        pltpu.SemaphoreType.DMA, pltpu.SemaphoreType.DMA((2,))],
        compiler_params=_cparams(("arbitrary",)),
        name="moe_combine",
    )(pos, route, h1, y)


def _blockdiag_dense(w, width):
    nb, bs, _ = w.shape
    per = width // bs
    wt = w.reshape(nb // per, per, bs, bs)
    eye = jnp.eye(per, dtype=w.dtype)
    dense = jnp.einsum('gpio,pq->gpiqo', wt, eye)
    return dense.reshape(nb // per, width, width)


def _rope_tables(S):
    half = DA_HEAD_DIM // 2
    inv = ROPE_THETA ** (-(jnp.arange(half, dtype=F32) * 2.0 / DA_HEAD_DIM))
    ang = jnp.arange(S, dtype=F32)[:, None] * inv[None, :]
    cos = jnp.tile(jnp.cos(ang), (1, LANES // half))
    sign = jnp.asarray(np.where(_HEAD_LANE_HALF == 0, -1.0, 1.0), F32)
    sin = jnp.tile(jnp.sin(ang), (1, LANES // half)) * sign[None, :]
    return cos, sin


_HEAD_LANE = np.arange(LANES)
_HEAD_LANE_HALF = _HEAD_LANE // (LANES // 2)
_HEAD_LANE_COMP = (_HEAD_LANE // (DA_HEAD_DIM // 2)) % 2
_HEAD_LANE_DIM = _HEAD_LANE_HALF * (DA_HEAD_DIM // 2) + _HEAD_LANE % (DA_HEAD_DIM // 2)
_HEAD_LANE_SRC = _HEAD_LANE_COMP * DA_HEAD_DIM + _HEAD_LANE_DIM


def _tile(n, pref):
    return pref if n % pref == 0 else n


def _layer(h2, mem2, B, S, lambda_init, attn_norm_g, w_in, b_gate, da_q_norm_g, da_k_norm_g,
           da_lambda_q1, da_lambda_k1, da_lambda_q2, da_lambda_k2, da_subln_g, ml_conv_w, ml_conv_b,
           ml_wq, ml_wk, ml_wv, ml_w_if, ml_b_if, ml_out_norm_g, ml_skip, mem_norm_g, ca_w_kv,
           ca_q_norm_g, ca_k_norm_g, w_branch_da, w_branch_ml, w_branch_ca, w_out, ffn_norm_g,
           router_w, router_b, w_gate_up, b_gate_up, w_down, b_down):
    T, D = h2.shape
    M = mem2.shape[0] // B
    QK = DA_HEADS * 2 * DA_HEAD_DIM
    VW = DA_HEADS * DA_V_DIM
    C = ml_conv_w.shape[1]
    CAW = ca_w_kv.shape[1] // 2
    E = router_w.shape[1]

    o = [0, QK, 2 * QK, 2 * QK + VW, 2 * QK + VW + C, 2 * QK + VW + 2 * C, 2 * QK + VW + 2 * C + CAW]
    head_perm = (np.arange(DA_HEADS)[:, None] * LANES + _HEAD_LANE_SRC[None, :]).reshape(-1)
    w_re = jnp.concatenate([w_in[:, o[0]:o[1]][:, head_perm], w_in[:, o[1]:o[2]][:, head_perm],
                            w_in[:, o[2]:o[3]], w_in[:, o[5]:o[6]], w_in[:, o[3]:o[5]], w_in[:, o[6]:]],
                           axis=1).astype(BF16)
    tn = 1024
    col_v, col_caq, col_mlx, col_mlz, col_gate = 2 * QK, 2 * QK + VW, 2 * QK + VW + CAW, \
        2 * QK + VW + CAW + C, 2 * QK + VW + CAW + 2 * C
    proj = _inproj(h2, attn_norm_g[None, :], w_re, tm=_tile(T, 2048), tn=tn)

    g2 = jnp.stack([da_q_norm_g[_HEAD_LANE_DIM], da_k_norm_g[_HEAD_LANE_DIM]])[:, None, :]
    cos_t, sin_t = _rope_tables(S)
    gm = jnp.asarray(_HEAD_LANE_COMP[:, None] == _HEAD_LANE_COMP[None, :], BF16)
    qk = _qkrope(proj, g2, cos_t, sin_t, gm, T=T, S=S, width=QK, tm=_tile(S, 1024))
    y_da = _diff_attention(qk, proj, da_lambda_q1[None, :], da_lambda_k1[None, :], da_lambda_q2[None, :],
                           da_lambda_k2[None, :], da_subln_g[None, :], da_q_norm_g[None, :],
                           da_k_norm_g[None, :], B=B, S=S, tq=_tile(S, 512),
                           v_col0=col_v // LANES, q_scale=DA_Q_SCALE, lambda_init=lambda_init)

    bd = jnp.stack([_blockdiag_dense(ml_wq, MXU_DIM), _blockdiag_dense(ml_wk, MXU_DIM),
                    _blockdiag_dense(ml_wv, MXU_DIM)]).astype(BF16)
    wif = jnp.pad(ml_w_if.reshape(3, C, 2 * ML_HEADS), ((0, 0), (0, 0), (0, LANES - 2 * ML_HEADS))).astype(BF16)
    bif = jnp.pad(ml_b_if, (0, LANES - 2 * ML_HEADS))[None, :]
    xc, mq, mk, mv, gates = _mlpre(proj, ml_conv_w, ml_conv_b[None, :], bd, wif, bif, B=B, S=S, C=C,
                                   tm=_tile(S, 512), x_col=col_mlx // C)
    dml = C // ML_HEADS
    y_ml = _mlstm(mq, mk, mv, gates, xc, proj, ml_out_norm_g[None, :], ml_skip[None, :], B=B, S=S, C=C,
                  L=_tile(S, 256), z_col0=col_mlz // dml, hpb=4)

    kmem, vmem = _memkv(mem2, mem_norm_g[None, :], ca_w_kv.astype(BF16), ca_k_norm_g[None, :], B=B, M=M, W=CAW)
    y_ca = _xattn(proj, kmem, vmem, ca_q_norm_g[None, :], B=B, S=S, M=M, W=CAW, tm=_tile(S, 512),
                  q_col=col_caq // CAW)

    tmx = _tile(T, 512)
    tri = (jnp.arange(tmx)[None, :] < jnp.arange(tmx)[:, None]).astype(BF16)
    rw = jnp.pad(router_w, ((0, 0), (0, LANES - E))).astype(BF16)
    rb = jnp.pad(router_b, (0, LANES - E))[None, :]
    h1, xp, route, cnt = _mix(h2, y_da, y_ml, y_ca, proj, b_gate.reshape(N_BRANCH, 1, D),
                              w_branch_da.astype(BF16), w_branch_ml.astype(BF16),
                              w_branch_ca.astype(BF16), w_out.astype(BF16), ffn_norm_g[None, :], rw, rb, tri,
                              tm=tmx, g_col=col_gate // D, n_exp=E)

    tg = _tile(T, 512)
    n_tiles = (T * TOP_K) // tg + E
    assert D == SUBLANES * LANES, "MoE rows are moved as one (8,128) f32 tile each"
    pos, te, n_used, xs = _scatter(route, cnt, xp, tm=_tile(T, 512), tg=tg, n_exp=E, n_tiles=n_tiles)
    y = _experts(te[:, 0], n_used[0, :1], xs, w_gate_up, b_gate_up[:, None, :],
                 w_down, b_down[:, None, :], tg=tg, n_tiles=n_tiles)
    return _combine(pos, route, h1, y, tm=_tile(T, 256))


def kernel(x, mem, attn_norm_g, w_in, b_gate, da_q_norm_g, da_k_norm_g, da_lambda_q1, da_lambda_k1, da_lambda_q2, da_lambda_k2, da_subln_g, ml_conv_w, ml_conv_b, ml_wq, ml_wk, ml_wv, ml_w_if, ml_b_if, ml_out_norm_g, ml_skip, mem_norm_g, ca_w_kv, ca_q_norm_g, ca_k_norm_g, w_branch_da, w_branch_ml, w_branch_ca, w_out, ffn_norm_g, router_w, router_b, w_gate_up, b_gate_up, w_down, b_down):
    B, S, D = x.shape
    depth = w_in.shape[0]
    h2 = x.reshape(B * S, D)
    mem2 = mem.reshape(B * mem.shape[1], D)
    params = (attn_norm_g, w_in, b_gate, da_q_norm_g, da_k_norm_g, da_lambda_q1, da_lambda_k1,
              da_lambda_q2, da_lambda_k2, da_subln_g, ml_conv_w, ml_conv_b, ml_wq, ml_wk, ml_wv, ml_w_if,
              ml_b_if, ml_out_norm_g, ml_skip, mem_norm_g, ca_w_kv, ca_q_norm_g, ca_k_norm_g, w_branch_da,
              w_branch_ml, w_branch_ca, w_out, ffn_norm_g, router_w, router_b, w_gate_up, b_gate_up,
              w_down, b_down)
    for l in range(depth):
        lambda_init = 0.8 - 0.6 * math.exp(-0.3 * l)
        h2 = _layer(h2, mem2, B, S, lambda_init, *[p[l] for p in params])
    return h2.reshape(B, S, D)
```

```python
import functools
import math

import jax
import jax.numpy as jnp
import numpy as np
from jax import lax
from jax.experimental import pallas as pl
from jax.experimental.pallas import tpu as pltpu

F32 = jnp.float32
BF16 = jnp.bfloat16
I32 = jnp.int32
U32 = jnp.uint32

NORM_EPS = 1e-6
ROPE_THETA = 10000.0
CHUNK = 64

DA_HEADS = 8
DA_HEAD_DIM = 64
DA_V_DIM = 128
ML_HEADS = 4
ML_CONV = 4
ML_QKV_BLOCK = 4
CA_HEADS = 4
N_BRANCH = 3
TOP_K = 4
SWIGLU_LIMIT = 7.0
SWIGLU_ALPHA = 1.702

LANES = 128
SUBLANES = 8
MXU_DIM = 256
VMEM_LIMIT = 56 * 1024 * 1024
NEG_BIG = -1e30
DA_Q_SCALE = DA_HEAD_DIM ** -0.5 * math.log2(math.e)
DA_BOUND_SLACK = 1.01
DA_BOUND_LIMIT = 40.0


def _cparams(sem):
    return pltpu.CompilerParams(dimension_semantics=sem, vmem_limit_bytes=VMEM_LIMIT)


def _sigmoid(x):
    return 0.5 * jnp.tanh(0.5 * x) + 0.5


def _silu(x):
    return x * _sigmoid(x)


def _inproj_body(x_ref, g_ref, w_ref, o_ref, xn_ref):
    j = pl.program_id(1)

    @pl.when(j == 0)
    def _():
        x = x_ref[...]
        ms = jnp.mean(x * x, axis=-1, keepdims=True)
        xn_ref[...] = (x * lax.rsqrt(ms + NORM_EPS) * g_ref[...]).astype(BF16)

    o_ref[...] = jnp.dot(xn_ref[...], w_ref[...], preferred_element_type=F32).astype(o_ref.dtype)


def _inproj(x2, g, w_bf, *, tm, tn):
    T, D = x2.shape
    N = w_bf.shape[1]
    return pl.pallas_call(
        _inproj_body,
        grid=(T // tm, N // tn),
        in_specs=[
            pl.BlockSpec((tm, D), lambda i, j: (i, 0)),
            pl.BlockSpec((1, D), lambda i, j: (0, 0)),
            pl.BlockSpec((D, tn), lambda i, j: (0, j)),
        ],
        out_specs=pl.BlockSpec((tm, tn), lambda i, j: (i, j)),
        out_shape=jax.ShapeDtypeStruct((T, N), BF16),
        scratch_shapes=[pltpu.VMEM((tm, D), BF16)],
        compiler_params=_cparams(("arbitrary", "arbitrary")),
        name="inproj",
    )(x2, g, w_bf)


def _group_sumsq(xb, gm):
    sq = xb * xb
    hi = sq.astype(BF16)
    lo = (sq - hi.astype(F32)).astype(BF16)
    return (jnp.dot(hi, gm, preferred_element_type=F32)
            + jnp.dot(lo, gm, preferred_element_type=F32))


def _qkrope_body(x_ref, g_ref, cos_ref, sin_ref, gm_ref, o_ref, *, ncb, scale):
    c = pl.program_id(1)
    g = g_ref[...] * jnp.where(c == 0, scale, 1.0).astype(F32)
    cos = cos_ref[...]
    sin = sin_ref[...]
    gm = gm_ref[...]
    for cb in range(ncb):
        sl = slice(cb * LANES, (cb + 1) * LANES)
        xb = x_ref[:, sl].astype(F32)
        ss = _group_sumsq(xb, gm)
        y = xb * lax.rsqrt(ss * (1.0 / DA_HEAD_DIM) + NORM_EPS) * g
        sw = pltpu.roll(y, LANES // 2, 1)
        o_ref[:, sl] = (y * cos + sw * sin).astype(o_ref.dtype)


def _qkrope(proj, g2, cos_t, sin_t, gm, *, T, S, width, tm):
    n_s = S // tm
    return pl.pallas_call(
        functools.partial(_qkrope_body, ncb=width // LANES, scale=DA_Q_SCALE),
        grid=(T // tm, 2),
        in_specs=[
            pl.BlockSpec((tm, width), lambda i, c: (i, c)),
            pl.BlockSpec((None, 1, LANES), lambda i, c: (c, 0, 0)),
            pl.BlockSpec((tm, LANES), lambda i, c: (i % n_s, 0)),
            pl.BlockSpec((tm, LANES), lambda i, c: (i % n_s, 0)),
            pl.BlockSpec((LANES, LANES), lambda i, c: (0, 0)),
        ],
        out_specs=pl.BlockSpec((tm, width), lambda i, c: (i, c)),
        out_shape=jax.ShapeDtypeStruct((T, 2 * width), BF16),
        compiler_params=_cparams(("arbitrary", "arbitrary")),
        name="qkrope",
    )(proj, g2, cos_t, sin_t, gm)


def _da_body(q_ref, k_ref, v_ref, lq1_ref, lk1_ref, lq2_ref, lk2_ref, subg_ref, qg_ref, kg_ref, o_ref,
             m_ref, acc_ref, kt_ref, bound_ref, *, tq, n_kv, q_scale, lambda_init):
    i = pl.program_id(2)
    q = q_ref[...]
    lane = lax.broadcasted_iota(I32, (1, LANES), 1)
    lo = (lane & (DA_HEAD_DIM // 2)) == 0
    zero = jnp.zeros_like(q)
    qc = (jnp.where(lo, q, zero), jnp.where(lo, zero, q))

    @pl.when(i == 0)
    def _():
        def tbody(j, carry):
            start = pl.multiple_of(j * tq, tq)
            kt_ref[:, pl.ds(start, tq)] = k_ref[pl.ds(start, tq), :].astype(F32).T.astype(BF16)
            return carry
        lax.fori_loop(0, n_kv, tbody, 0)
        bound_ref[0] = (DA_HEAD_DIM * q_scale * DA_BOUND_SLACK
                        * jnp.max(jnp.abs(qg_ref[...])) * jnp.max(jnp.abs(kg_ref[...])))

    bound = bound_ref[0]
    bound_max = bound

    acc_ref[...] = jnp.zeros(acc_ref.shape, F32)
    ones = jnp.ones((tq, LANES), BF16)

    def scores(j):
        ktj = kt_ref[:, pl.ds(pl.multiple_of(j * tq, tq), tq)]
        return [jnp.dot(qc[c], ktj, preferred_element_type=F32) for c in range(2)]

    def diag_mask():
        r = lax.broadcasted_iota(I32, (tq, tq), 0)
        cidx = lax.broadcasted_iota(I32, (tq, tq), 1)
        return (cidx | (CHUNK - 1)) <= (r | (CHUNK - 1))

    def v_aug(j):
        return jnp.concatenate([v_ref[pl.ds(pl.multiple_of(j * tq, tq), tq), :], ones], axis=1)

    def consume_bounded(j, ss, masked):
        del masked
        vj = v_aug(j)
        ps = [jnp.exp2(ss[c] - bound).astype(BF16) for c in range(2)]
        pvs = [jnp.dot(ps[c], vj, preferred_element_type=F32) for c in range(2)]
        for c in range(2):
            acc_ref[c] = acc_ref[c] + pvs[c]

    def consume_online(j, ss, masked):
        vj = v_aug(j)
        ps, alphas = [], []
        for c in range(2):
            s = jnp.where(diag_mask(), ss[c], NEG_BIG) if masked else ss[c]
            m_old = m_ref[c]
            m_new = jnp.maximum(m_old, jnp.max(s, axis=-1, keepdims=True))
            ps.append(jnp.exp2(s - jnp.concatenate([m_new] * (tq // LANES), axis=1)).astype(BF16))
            alphas.append(jnp.exp2(m_old - m_new))
            m_ref[c] = m_new
        pvs = [jnp.dot(ps[c], vj, preferred_element_type=F32) for c in range(2)]
        for c in range(2):
            acc_ref[c] = jnp.concatenate([alphas[c], alphas[c]], axis=1) * acc_ref[c] + pvs[c]

    def diag_bounded():
        hq = tq // 2
        start = pl.multiple_of(i * tq, tq)
        kt_i = kt_ref[:, pl.ds(start, tq)]
        v_i = v_aug(i)
        r = lax.broadcasted_iota(I32, (hq, tq), 0)
        cidx = lax.broadcasted_iota(I32, (hq, tq), 1)
        mask_top = ((cidx | (CHUNK - 1)) <= (r | (CHUNK - 1)))[:, :hq]
        mask_bot = (cidx | (CHUNK - 1)) <= ((r + hq) | (CHUNK - 1))
        s_top = [jnp.dot(qc[c][:hq], kt_i[:, :hq], preferred_element_type=F32) for c in range(2)]
        s_bot = [jnp.dot(qc[c][hq:], kt_i, preferred_element_type=F32) for c in range(2)]
        p_top = [jnp.where(mask_top, jnp.exp2(s_top[c] - bound), 0.0).astype(BF16) for c in range(2)]
        p_bot = [jnp.where(mask_bot, jnp.exp2(s_bot[c] - bound), 0.0).astype(BF16) for c in range(2)]
        pv_top = [jnp.dot(p_top[c], v_i[:hq], preferred_element_type=F32) for c in range(2)]
        pv_bot = [jnp.dot(p_bot[c], v_i, preferred_element_type=F32) for c in range(2)]
        for c in range(2):
            acc_ref[c, pl.ds(0, hq)] = acc_ref[c, pl.ds(0, hq)] + pv_top[c]
            acc_ref[c, pl.ds(hq, hq)] = acc_ref[c, pl.ds(hq, hq)] + pv_bot[c]

    def sweep(consume, diag):
        def pair(jj, carry):
            sa = scores(2 * jj)
            sb = scores(2 * jj + 1)
            consume(2 * jj, sa, False)
            consume(2 * jj + 1, sb, False)
            return carry

        lax.fori_loop(0, lax.shift_right_logical(i, 1), pair, 0)

        @pl.when((i & 1) == 1)
        def _():
            consume(i - 1, scores(i - 1), False)
            diag()

        @pl.when((i & 1) == 0)
        def _():
            diag()

    @pl.when(bound_max < DA_BOUND_LIMIT)
    def _():
        sweep(consume_bounded, diag_bounded)

    @pl.when(jnp.logical_not(bound_max < DA_BOUND_LIMIT))
    def _():
        m_ref[...] = jnp.full(m_ref.shape, NEG_BIG, F32)
        sweep(consume_online, lambda: consume_online(i, scores(i), True))

    f32 = F32
    lam = (jnp.exp(jnp.sum(lq1_ref[...].astype(f32) * lk1_ref[...].astype(f32), keepdims=True))
           - jnp.exp(jnp.sum(lq2_ref[...].astype(f32) * lk2_ref[...].astype(f32), keepdims=True))
           + lambda_init)
    a1 = acc_ref[0]
    a2 = acc_ref[1]
    o = a1[:, :DA_V_DIM] / a1[:, DA_V_DIM:] - lam * (a2[:, :DA_V_DIM] / a2[:, DA_V_DIM:])
    ms = jnp.mean(o * o, axis=-1, keepdims=True)
    o = o * lax.rsqrt(ms + NORM_EPS) * subg_ref[...]
    o_ref[...] = (o * (1.0 - lambda_init)).astype(o_ref.dtype)


def _diff_attention(qk, proj, lq1, lk1, lq2, lk2, subg, qg, kg, *, B, S, tq, v_col0, q_scale, lambda_init):
    T = B * S
    H = DA_HEADS
    n_q = S // tq
    vec = lambda n: pl.BlockSpec((1, n), lambda b, h, i: (0, 0))
    return pl.pallas_call(
        functools.partial(_da_body, tq=tq, n_kv=n_q, q_scale=q_scale, lambda_init=lambda_init),
        grid=(B, H, n_q),
        in_specs=[
            pl.BlockSpec((tq, LANES), lambda b, h, i: (b * n_q + i, h)),
            pl.BlockSpec((S, LANES), lambda b, h, i: (b, H + h)),
            pl.BlockSpec((S, LANES), lambda b, h, i: (b, v_col0 + h)),
            vec(DA_HEAD_DIM), vec(DA_HEAD_DIM), vec(DA_HEAD_DIM), vec(DA_HEAD_DIM),
            vec(DA_V_DIM), vec(DA_HEAD_DIM), vec(DA_HEAD_DIM),
        ],
        out_specs=pl.BlockSpec((tq, LANES), lambda b, h, i: (b * n_q + i, h)),
        out_shape=jax.ShapeDtypeStruct((T, H * DA_V_DIM), BF16),
        scratch_shapes=[pltpu.VMEM((2, tq, LANES), F32), pltpu.VMEM((2, tq, 2 * DA_V_DIM), F32),
                        pltpu.VMEM((LANES, S), BF16), pltpu.SMEM((1,), F32)],
        compiler_params=_cparams(("arbitrary", "arbitrary", "arbitrary")),
        name="diff_attn",
    )(qk, qk, proj, lq1, lk1, lq2, lk2, subg, qg, kg)


def _mlpre_body(x_ref, cw_ref, cb_ref, bd_ref, wif_ref, bif_ref,
                xc_ref, q_ref, k_ref, v_ref, g_ref, prev_ref, *, tm, ncb, kscale):
    s_idx = pl.program_id(1)

    @pl.when(s_idx == 0)
    def _():
        prev_ref[...] = jnp.zeros(prev_ref.shape, F32)

    row8 = lax.broadcasted_iota(I32, (SUBLANES, MXU_DIM), 0)
    gacc = jnp.zeros((tm, LANES), F32) + bif_ref[...]
    for cb in range(ncb):
        sl = slice(cb * MXU_DIM, (cb + 1) * MXU_DIM)
        xb16 = x_ref[:, sl]
        x = xb16.astype(F32)
        prev = prev_ref[:, sl]
        conv = x * cw_ref[ML_CONV - 1:ML_CONV, sl] + cb_ref[:, sl]
        for d in range(1, ML_CONV):
            xs = pltpu.roll(x, d, 0)
            ps = pltpu.roll(prev, d, 0)
            head = jnp.where(row8 < d, ps, xs[:SUBLANES])
            shifted = jnp.concatenate([head, xs[SUBLANES:]], axis=0)
            conv = conv + shifted * cw_ref[ML_CONV - 1 - d:ML_CONV - d, sl]
        prev_ref[:, sl] = x[tm - SUBLANES:]
        xc = _silu(conv)
        xc16 = xc.astype(BF16)
        q = jnp.dot(xc16, bd_ref[0, cb], preferred_element_type=F32)
        k = jnp.dot(xc16, bd_ref[1, cb], preferred_element_type=F32)
        v = jnp.dot(xb16, bd_ref[2, cb], preferred_element_type=F32)
        q16, k16, v16 = q.astype(BF16), k.astype(BF16), v.astype(BF16)
        gacc = gacc + jnp.dot(q16, wif_ref[0, sl, :], preferred_element_type=F32)
        gacc = gacc + jnp.dot(k16, wif_ref[1, sl, :], preferred_element_type=F32)
        gacc = gacc + jnp.dot(v16, wif_ref[2, sl, :], preferred_element_type=F32)
        xc_ref[:, sl] = xc16
        q_ref[:, sl] = q16
        k_ref[:, sl] = (k * kscale).astype(BF16)
        v_ref[:, sl] = v16
    g_ref[...] = gacc


def _mlpre(proj, conv_w, conv_b, bd, wif, bif, *, B, S, C, tm, x_col):
    T = B * S
    n_s = S // tm
    row = lambda i: pl.BlockSpec((tm, C), lambda b, s: (b * n_s + s, i))
    outs = pl.pallas_call(
        functools.partial(_mlpre_body, tm=tm, ncb=C // MXU_DIM, kscale=(C // ML_HEADS) ** -0.5),
        grid=(B, n_s),
        in_specs=[
            row(x_col),
            pl.BlockSpec((ML_CONV, C), lambda b, s: (0, 0)),
            pl.BlockSpec((1, C), lambda b, s: (0, 0)),
            pl.BlockSpec(bd.shape, lambda b, s: (0, 0, 0, 0)),
            pl.BlockSpec(wif.shape, lambda b, s: (0, 0, 0)),
            pl.BlockSpec((1, LANES), lambda b, s: (0, 0)),
        ],
        out_specs=[row(0), row(0), row(0), row(0),
                   pl.BlockSpec((tm, LANES), lambda b, s: (b * n_s + s, 0))],
        out_shape=[jax.ShapeDtypeStruct((T, C), BF16)] * 4 + [jax.ShapeDtypeStruct((T, LANES), F32)],
        scratch_shapes=[pltpu.VMEM((SUBLANES, C), F32)],
        compiler_params=_cparams(("arbitrary", "arbitrary")),
        name="mlstm_pre",
    )(proj, conv_w, conv_b, bd, wif, bif)
    return outs


def _mlstm_body(q_ref, k_ref, v_ref, g_ref, xc_ref, z_ref, ng_ref, skip_ref, o_ref,
                c_ref, cb_ref, n_ref, m_ref, *, L, d, hpb):
    hg = pl.program_id(1)
    c_idx = pl.program_id(2)

    @pl.when(c_idx == 0)
    def _():
        c_ref[...] = jnp.zeros(c_ref.shape, F32)
        cb_ref[...] = jnp.zeros(cb_ref.shape, BF16)
        n_ref[...] = jnp.zeros(n_ref.shape, F32)
        m_ref[...] = jnp.zeros(m_ref.shape, F32)

    g = g_ref[...]
    gt = g.T
    lane = lax.broadcasted_iota(I32, (1, LANES), 1)
    sub = lax.broadcasted_iota(I32, (LANES, 1), 0)
    t_idx = lax.broadcasted_iota(I32, (L, L), 0)
    s_idx = lax.broadcasted_iota(I32, (L, L), 1)
    tri = s_idx <= t_idx

    for hh in range(hpb):
        h = hg * hpb + hh
        sl = slice(hh * d, (hh + 1) * d)
        i_col = jnp.sum(jnp.where(lane == h, g, 0.0), axis=1, keepdims=True)
        f_col = jnp.sum(jnp.where(lane == ML_HEADS + h, g, 0.0), axis=1, keepdims=True)
        i_row = jnp.sum(jnp.where(sub == h, gt, 0.0), axis=0, keepdims=True)
        f_row = jnp.sum(jnp.where(sub == ML_HEADS + h, gt, 0.0), axis=0, keepdims=True)
        lf_col = jax.nn.log_sigmoid(f_col)
        lf_row = jax.nn.log_sigmoid(f_row)

        b_col = jnp.sum(jnp.where(tri, lf_row, 0.0), axis=1, keepdims=True)
        b_row = jnp.sum(jnp.where(t_idx <= s_idx, lf_col, 0.0), axis=0, keepdims=True)
        b_last = jnp.sum(lf_row, axis=1, keepdims=True)

        m_prev = m_ref[hh]
        dlog = jnp.where(tri, b_col - b_row + i_row, NEG_BIG)
        inter_log = b_col + m_prev
        m_rowv = jnp.maximum(inter_log, jnp.max(dlog, axis=1, keepdims=True))
        dw = jnp.exp(dlog - m_rowv)
        inter_w = jnp.exp(inter_log - m_rowv)

        q = q_ref[:, sl]
        k = k_ref[:, sl]
        v = v_ref[:, sl]
        qk = lax.dot_general(q, k, (((1,), (1,)), ((), ())), preferred_element_type=F32)
        qc = jnp.dot(q, cb_ref[hh], preferred_element_type=F32)
        s = qk * dw

        state_row = b_last - b_row + i_row
        state_col = b_last - b_col + i_col
        m_new = jnp.maximum(b_last + m_prev, jnp.max(state_row, axis=1, keepdims=True))
        decay = jnp.exp(b_last + m_prev - m_new)
        ws_col = jnp.exp(state_col - m_new)
        kwf = k.astype(F32) * ws_col

        num = jnp.dot(s.astype(BF16), v, preferred_element_type=F32) + inter_w * qc
        dc = lax.dot_general(kwf.astype(BF16), v, (((0,), (0,)), ((), ())), preferred_element_type=F32)
        den = (jnp.sum(s, axis=1, keepdims=True)
               + inter_w * jnp.sum(q.astype(F32) * n_ref[hh], axis=1, keepdims=True))
        hv = num / jnp.maximum(jnp.abs(den), jnp.exp(-m_rowv))

        c_new = decay * c_ref[hh] + dc
        c_ref[hh] = c_new
        cb_ref[hh] = c_new.astype(BF16)
        n_ref[hh] = decay * n_ref[hh] + jnp.sum(kwf, axis=0, keepdims=True)
        m_ref[hh] = m_new

        mu = jnp.mean(hv, axis=-1, keepdims=True)
        cen = hv - mu
        var = jnp.mean(cen * cen, axis=-1, keepdims=True)
        hn = cen * lax.rsqrt(var + NORM_EPS) * ng_ref[:, sl]
        y = (hn + skip_ref[:, sl] * xc_ref[:, sl].astype(F32)) * _silu(z_ref[:, sl].astype(F32))
        o_ref[:, sl] = y.astype(o_ref.dtype)


def _mlstm(q, k, v, gates, xc, proj, norm_g, skip, *, B, S, C, L, z_col0, hpb):
    T = B * S
    H = ML_HEADS
    d = C // H
    w = hpb * d
    nc = S // L
    blk = lambda off: pl.BlockSpec((L, w), lambda b, h, c: (b * nc + c, off + h))
    par = pl.BlockSpec((1, w), lambda b, h, c: (0, h))
    return pl.pallas_call(
        functools.partial(_mlstm_body, L=L, d=d, hpb=hpb),
        grid=(B, H // hpb, nc),
        in_specs=[blk(0), blk(0), blk(0),
                  pl.BlockSpec((L, LANES), lambda b, h, c: (b * nc + c, 0)),
                  blk(0), blk(z_col0 // hpb), par, par],
        out_specs=blk(0),
        out_shape=jax.ShapeDtypeStruct((T, C), BF16),
        scratch_shapes=[pltpu.VMEM((hpb, d, d), F32), pltpu.VMEM((hpb, d, d), BF16),
                        pltpu.VMEM((hpb, 1, d), F32), pltpu.VMEM((hpb, 1, 1), F32)],
        compiler_params=_cparams(("arbitrary", "arbitrary", "arbitrary")),
        name="mlstm_scan",
    )(q, k, v, gates, xc, proj, norm_g, skip)


def _memkv_body(mem_ref, mg_ref, w_ref, kg_ref, k_ref, v_ref, *, W, dh):
    x = mem_ref[...]
    ms = jnp.mean(x * x, axis=-1, keepdims=True)
    xn = (x * lax.rsqrt(ms + NORM_EPS) * mg_ref[...]).astype(BF16)
    kv = jnp.dot(xn, w_ref[...], preferred_element_type=F32)
    for hd in range(W // dh):
        sl = slice(hd * dh, (hd + 1) * dh)
        kh = kv[:, sl]
        msk = jnp.mean(kh * kh, axis=-1, keepdims=True)
        k_ref[:, sl] = (kh * lax.rsqrt(msk + NORM_EPS) * kg_ref[...]).astype(BF16)
    v_ref[...] = kv[:, W:].astype(BF16)


def _memkv(mem2, mem_g, w_kv_bf, k_g, *, B, M, W):
    D = mem2.shape[1]
    dh = W // CA_HEADS
    return pl.pallas_call(
        functools.partial(_memkv_body, W=W, dh=dh),
        grid=(B,),
        in_specs=[pl.BlockSpec((M, D), lambda b: (b, 0)),
                  pl.BlockSpec((1, D), lambda b: (0, 0)),
                  pl.BlockSpec((D, 2 * W), lambda b: (0, 0)),
                  pl.BlockSpec((1, dh), lambda b: (0, 0))],
        out_specs=[pl.BlockSpec((M, W), lambda b: (b, 0))] * 2,
        out_shape=[jax.ShapeDtypeStruct((B * M, W), BF16)] * 2,
        compiler_params=_cparams(("arbitrary",)),
        name="mem_kv",
    )(mem2, mem_g, w_kv_bf, k_g)


def _xattn_tile(q_ref, k_ref, v_ref, qg_ref):
    W = q_ref.shape[1]
    dh = W // CA_HEADS
    scale = dh ** -0.5
    outs = []
    for hd in range(CA_HEADS):
        sl = slice(hd * dh, (hd + 1) * dh)
        qh = q_ref[:, sl].astype(F32)
        ms = jnp.mean(qh * qh, axis=-1, keepdims=True)
        qn = (qh * lax.rsqrt(ms + NORM_EPS) * (qg_ref[...] * scale)).astype(BF16)
        s = lax.dot_general(qn, k_ref[:, sl], (((1,), (1,)), ((), ())), preferred_element_type=F32)
        mx = jnp.max(s, axis=-1, keepdims=True)
        p = jnp.exp(s - mx)
        p = p / jnp.sum(p, axis=-1, keepdims=True)
        outs.append(jnp.dot(p.astype(BF16), v_ref[:, sl], preferred_element_type=F32).astype(BF16))
    return jnp.concatenate(outs, axis=1)


def _store_tile_rows(ref, x):
    n = x.shape[0]
    for c in range(SUBLANES):
        ref[pl.ds(c, n, stride=SUBLANES), :] = x[:, c * LANES:(c + 1) * LANES]


def _load_tile_rows(ref, n):
    return [ref[pl.ds(c, n, stride=SUBLANES), :] for c in range(SUBLANES)]


def _mix_body(x_ref, yda_ref, yml_ref, caq_ref, kmem_ref, vmem_ref, caqg_ref, gda_ref, gml_ref, gca_ref,
              bg_ref, wda_ref, wml_ref, wca_ref, wout_ref, fg_ref, rw_ref, rb_ref, tri_ref,
              h_ref, xp_ref, route_ref, cnt_ref, carry_ref, *, n_exp):
    i = pl.program_id(0)

    @pl.when(i == 0)
    def _():
        carry_ref[...] = jnp.zeros(carry_ref.shape, F32)

    def gate(g_ref, b):
        return _sigmoid(g_ref[...].astype(F32) + bg_ref[b])

    yca = _xattn_tile(caq_ref, kmem_ref, vmem_ref, caqg_ref)
    mix = (gate(gda_ref, 0) * jnp.dot(yda_ref[...], wda_ref[...], preferred_element_type=F32)
           + gate(gml_ref, 1) * jnp.dot(yml_ref[...], wml_ref[...], preferred_element_type=F32)
           + gate(gca_ref, 2) * jnp.dot(yca, wca_ref[...], preferred_element_type=F32))
    h1 = x_ref[...] + jnp.dot(mix.astype(BF16), wout_ref[...], preferred_element_type=F32)
    h_ref[...] = h1
    ms = jnp.mean(h1 * h1, axis=-1, keepdims=True)
    xn = h1 * lax.rsqrt(ms + NORM_EPS) * fg_ref[...]
    _store_tile_rows(xp_ref, xn)

    logits = jnp.dot(xn.astype(BF16), rw_ref[...], preferred_element_type=F32) + rb_ref[...]
    tm = logits.shape[0]
    lane = lax.broadcasted_iota(I32, (tm, LANES), 1)
    work = jnp.where(lane < n_exp, logits, NEG_BIG)
    sel = jnp.zeros((tm, LANES), F32)
    vals, idxs = [], []
    for _ in range(TOP_K):
        mx = jnp.max(work, axis=-1, keepdims=True)
        idx = jnp.min(jnp.where(work == mx, lane, LANES), axis=-1, keepdims=True)
        hit = lane == idx
        sel = jnp.where(hit, 1.0, sel)
        work = jnp.where(hit, NEG_BIG, work)
        vals.append(mx)
        idxs.append(idx)
    exps = [jnp.exp(v - vals[0]) for v in vals]
    tot = exps[0] + exps[1] + exps[2] + exps[3]

    cum = jnp.dot(tri_ref[...], sel.astype(BF16), preferred_element_type=F32) + carry_ref[...]
    route = jnp.zeros((tm, LANES), F32)
    for kk in range(TOP_K):
        rank = jnp.sum(jnp.where(lane == idxs[kk], cum, 0.0), axis=-1, keepdims=True)
        route = jnp.where(lane == kk, idxs[kk].astype(F32), route)
        route = jnp.where(lane == TOP_K + kk, exps[kk] / tot, route)
        route = jnp.where(lane == 2 * TOP_K + kk, rank, route)
    route_ref[...] = route
    carry_ref[...] = carry_ref[...] + jnp.sum(sel, axis=0, keepdims=True)
    cnt_ref[...] = carry_ref[...]


def _mix(x2, yda, yml, kmem, vmem, caqg, proj, bg, wda, wml, wca, wout, fg, rw, rb, tri,
         *, tm, S, caq_col, g_col, n_exp):
    T, D = x2.shape
    C = yml.shape[1]
    M = kmem.shape[0] // (T // S)
    W = kmem.shape[1]
    n_s = S // tm
    const = lambda shape: pl.BlockSpec(shape, lambda i: (0,) * len(shape))
    return pl.pallas_call(
        functools.partial(_mix_body, n_exp=n_exp),
        grid=(T // tm,),
        in_specs=[pl.BlockSpec((tm, D), lambda i: (i, 0)),
                  pl.BlockSpec((tm, D), lambda i: (i, 0)),
                  pl.BlockSpec((tm, C), lambda i: (i, 0)),
                  pl.BlockSpec((tm, W), lambda i: (i, caq_col)),
                  pl.BlockSpec((M, W), lambda i: (i // n_s, 0)),
                  pl.BlockSpec((M, W), lambda i: (i // n_s, 0)),
                  const((1, W // CA_HEADS)),
                  pl.BlockSpec((tm, D), lambda i: (i, g_col)),
                  pl.BlockSpec((tm, D), lambda i: (i, g_col + 1)),
                  pl.BlockSpec((tm, D), lambda i: (i, g_col + 2)),
                  const((N_BRANCH, 1, D)),
                  const((D, D)), const((C, D)), const((D, D)), const((D, D)),
                  const((1, D)), const((D, LANES)), const((1, LANES)), const((tm, tm))],
        out_specs=[pl.BlockSpec((tm, D), lambda i: (i, 0)),
                   pl.BlockSpec((tm * SUBLANES, LANES), lambda i: (i, 0)),
                   pl.BlockSpec((tm, LANES), lambda i: (i, 0)),
                   pl.BlockSpec((1, LANES), lambda i: (0, 0))],
        out_shape=[jax.ShapeDtypeStruct((T, D), F32),
                   jax.ShapeDtypeStruct((T * SUBLANES, LANES), F32),
                   jax.ShapeDtypeStruct((T, LANES), F32),
                   jax.ShapeDtypeStruct((1, LANES), F32)],
        scratch_shapes=[pltpu.VMEM((1, LANES), F32)],
        compiler_params=_cparams(("arbitrary",)),
        name="mix_route",
    )(x2, yda, yml, proj, kmem, vmem, caqg, proj, proj, proj, bg, wda, wml, wca, wout, fg, rw, rb, tri)


ROW_DMA_UNROLL = 8
PAD_CHUNK_LOG2 = 5
PAD_CHUNK = 1 << PAD_CHUNK_LOG2


def _issue_rows(row_copy, tm):
    def issue(t, carry):
        for kk in range(TOP_K):
            row_copy(t, kk).start(priority=kk % 2)
        return carry

    lax.fori_loop(0, tm, issue, 0, unroll=ROW_DMA_UNROLL)


def _drain_rows(row_copy, tm):
    def drain(t, carry):
        for kk in range(TOP_K):
            row_copy(t, kk).wait()
        return carry

    lax.fori_loop(0, tm, drain, 0, unroll=ROW_DMA_UNROLL)


def _issue_and_drain_rows(row_copy, tm):
    _issue_rows(row_copy, tm)
    _drain_rows(row_copy, tm)


def _group_starts(cnt, tg, n_exp):
    lane_r = lax.broadcasted_iota(I32, (LANES, LANES), 0)
    lane_c = lax.broadcasted_iota(I32, (LANES, LANES), 1)
    padded = jnp.ceil(cnt * (1.0 / tg)) * tg
    padded_col = jnp.sum(jnp.where(lane_r == lane_c, padded, 0.0), axis=1, keepdims=True)
    start = jnp.sum(jnp.where(lane_r < lane_c, padded_col, 0.0), axis=0, keepdims=True)
    return start, start + padded


def _scatter_body(route_ref, route_next_ref, cnt_ref, xp_ref, pos_ref, te_ref, nu_ref, xs_hbm,
                  posv_ref, pos_smem, padv_ref, pad_smem, zero_ref, sem_p, sem_d, sem_z,
                  *, tm, tg, n_exp, n_tiles):
    i = pl.program_id(0)
    n = pl.num_programs(0)
    slot = i & 1
    lane = lax.broadcasted_iota(I32, (tm, LANES), 1)
    start, end = _group_starts(cnt_ref[...], tg, n_exp)

    def stage_positions(route, s):
        posm = jnp.zeros((tm, LANES), F32)
        for kk in range(TOP_K):
            e = jnp.sum(jnp.where(lane == kk, route, 0.0), axis=-1, keepdims=True)
            rank = jnp.sum(jnp.where(lane == 2 * TOP_K + kk, route, 0.0), axis=-1, keepdims=True)
            st = jnp.sum(jnp.where(lane == e.astype(I32), start, 0.0), axis=-1, keepdims=True)
            posm = jnp.where(lane == kk, (st + rank) * SUBLANES, posm)
        posv_ref[s] = posm.astype(I32).T[:SUBLANES]
        cp = pltpu.make_async_copy(posv_ref.at[s], pos_smem.at[s], sem_p)
        cp.start()
        cp.wait()

    @pl.when(i == 0)
    def _():
        stage_positions(route_ref[...], 0)
        tstart = (lax.broadcasted_iota(I32, (n_tiles, LANES), 0) * tg).astype(F32)
        lane_t = lax.broadcasted_iota(I32, (n_tiles, LANES), 1)
        done = jnp.where((lane_t < n_exp) & (end <= tstart), 1.0, 0.0)
        te = jnp.sum(done, axis=-1, keepdims=True)
        te_ref[...] = jnp.broadcast_to(te, (n_tiles, LANES)).astype(I32)
        n_used = jnp.max(end, axis=-1, keepdims=True) * (1.0 / tg)
        nu_ref[...] = jnp.broadcast_to(n_used, (1, LANES)).astype(I32)

        cnt = cnt_ref[...]
        padv_ref[...] = jnp.zeros(padv_ref.shape, I32)
        padv_ref[0:1, :] = ((start + cnt) * SUBLANES).astype(I32)
        padv_ref[1:2, :] = (end - start - cnt).astype(I32)
        zero_ref[...] = jnp.zeros(zero_ref.shape, F32)
        cpz = pltpu.make_async_copy(padv_ref, pad_smem, sem_p)
        cpz.start()
        cpz.wait()

        def pad_copy(off, rows):
            n = rows * SUBLANES
            return pltpu.make_async_copy(zero_ref.at[pl.ds(0, n)],
                                         xs_hbm.at[pl.ds(pl.multiple_of(off, SUBLANES), n)], sem_z)

        def for_each_pad_copy(fn):
            def per_expert(e, carry):
                base = pad_smem[0, e]
                n_pad = pad_smem[1, e]
                n_chunks = lax.shift_right_logical(n_pad, PAD_CHUNK_LOG2)

                def per_chunk(c, c2):
                    fn(pad_copy(base + c * (PAD_CHUNK * SUBLANES), PAD_CHUNK))
                    return c2
                lax.fori_loop(0, n_chunks, per_chunk, 0)

                def per_row(r, c2):
                    fn(pad_copy(base + (n_chunks * PAD_CHUNK + r) * SUBLANES, 1))
                    return c2
                lax.fori_loop(0, n_pad - n_chunks * PAD_CHUNK, per_row, 0)
                return carry
            lax.fori_loop(0, n_exp, per_expert, 0)

        for_each_pad_copy(lambda cp: cp.start())
        for_each_pad_copy(lambda cp: cp.wait())

    def row_copy(t, kk):
        dst = pl.multiple_of(pos_smem[slot, kk, t], SUBLANES)
        return pltpu.make_async_copy(xp_ref.at[pl.ds(pl.multiple_of(t * SUBLANES, SUBLANES), SUBLANES)],
                                     xs_hbm.at[pl.ds(dst, SUBLANES)], sem_d)

    _issue_rows(row_copy, tm)

    @pl.when(i + 1 < n)
    def _():
        stage_positions(route_next_ref[...], 1 - slot)

    pos_ref[...] = posv_ref[slot]
    _drain_rows(row_copy, tm)


def _scatter(route, cnt, xp, *, tm, tg, n_exp, n_tiles):
    T = route.shape[0]
    W = xp.shape[1]
    n_rows = n_tiles * tg * SUBLANES
    n_steps = T // tm
    return pl.pallas_call(
        functools.partial(_scatter_body, tm=tm, tg=tg, n_exp=n_exp, n_tiles=n_tiles),
        grid=(n_steps,),
        in_specs=[pl.BlockSpec((tm, LANES), lambda i: (i, 0)),
                  pl.BlockSpec((tm, LANES), lambda i: (jnp.minimum(i + 1, n_steps - 1), 0)),
                  pl.BlockSpec((1, LANES), lambda i: (0, 0)),
                  pl.BlockSpec((tm * SUBLANES, W), lambda i: (i, 0))],
        out_specs=[pl.BlockSpec((SUBLANES, tm), lambda i: (0, i)),
                   pl.BlockSpec((n_tiles, LANES), lambda i: (0, 0)),
                   pl.BlockSpec((1, LANES), lambda i: (0, 0)),
                   pl.BlockSpec(memory_space=pl.ANY)],
        out_shape=[jax.ShapeDtypeStruct((SUBLANES, T), I32),
                   jax.ShapeDtypeStruct((n_tiles, LANES), I32),
                   jax.ShapeDtypeStruct((1, LANES), I32),
                   jax.ShapeDtypeStruct((n_rows, W), F32)],
        scratch_shapes=[pltpu.VMEM((2, SUBLANES, tm), I32), pltpu.SMEM((2, SUBLANES, tm), I32),
                        pltpu.VMEM((SUBLANES, LANES), I32), pltpu.SMEM((SUBLANES, LANES), I32),
                        pltpu.VMEM((PAD_CHUNK * SUBLANES, LANES), F32),
                        pltpu.SemaphoreType.DMA, pltpu.SemaphoreType.DMA, pltpu.SemaphoreType.DMA],
        compiler_params=_cparams(("arbitrary",)),
        name="moe_scatter",
    )(route, route, cnt, xp)


def _experts_body(te_ref, nu_ref, xs_ref, wgu_ref, bgu_ref, wd_ref, bd_ref, y_ref, wgu16_ref, wd16_ref,
                  *, F, tg):
    i = pl.program_id(0)
    live = i < nu_ref[0]
    new_expert = (i == 0) | (te_ref[i] != te_ref[jnp.maximum(i - 1, 0)])

    @pl.when(live & new_expert)
    def _():
        wgu16_ref[...] = wgu_ref[...].astype(BF16)
        wd16_ref[...] = wd_ref[...].astype(BF16)

    @pl.when(live)
    def _():
        x = jnp.concatenate([c.astype(BF16) for c in _load_tile_rows(xs_ref, tg)], axis=1)
        h = jnp.dot(x, wgu16_ref[...], preferred_element_type=F32) + bgu_ref[...]
        gate = jnp.minimum(h[:, :F], SWIGLU_LIMIT)
        up = jnp.clip(h[:, F:], -SWIGLU_LIMIT, SWIGLU_LIMIT)
        a = (up + 1.0) * (gate * _sigmoid(SWIGLU_ALPHA * gate))
        y = jnp.dot(a.astype(BF16), wd16_ref[...], preferred_element_type=F32) + bd_ref[...]
        _store_tile_rows(y_ref, y)


def _experts(te, n_used, xs, wgu, bgu, wd, bd, *, tg, n_tiles):
    n_rows, W = xs.shape
    E, D, F2 = wgu.shape
    F = F2 // 2
    row = lambda i, te, nu: (jnp.minimum(i, nu[0] - 1), 0)
    exp3 = lambda i, te, nu: (te[jnp.minimum(i, nu[0] - 1)], 0, 0)
    grid_spec = pltpu.PrefetchScalarGridSpec(
        num_scalar_prefetch=2,
        grid=(n_tiles,),
        in_specs=[pl.BlockSpec((tg * SUBLANES, W), row),
                  pl.BlockSpec((None, D, F2), exp3),
                  pl.BlockSpec((None, 1, F2), exp3),
                  pl.BlockSpec((None, F, D), exp3),
                  pl.BlockSpec((None, 1, D), exp3)],
        out_specs=pl.BlockSpec((tg * SUBLANES, W), row),
        scratch_shapes=[pltpu.VMEM((D, F2), BF16), pltpu.VMEM((F, D), BF16)],
    )
    return pl.pallas_call(
        functools.partial(_experts_body, F=F, tg=tg),
        grid_spec=grid_spec,
        out_shape=jax.ShapeDtypeStruct((n_rows, W), F32),
        compiler_params=_cparams(("arbitrary",)),
        name="moe_experts",
    )(te, n_used, xs, wgu, bgu, wd, bd)


def _combine_body(pos_ref, route_ref, h_ref, y_hbm, o_ref, pos_smem, buf_ref, sem_p, sem_d, *, tm):
    i = pl.program_id(0)
    n = pl.num_programs(0)
    slot = i & 1

    def gather_rows(step, s):
        cp = pltpu.make_async_copy(pos_ref.at[:, pl.ds(pl.multiple_of(step * tm, tm), tm)],
                                   pos_smem.at[s], sem_p)
        cp.start()
        cp.wait()
        return functools.partial(row_copy, s)

    def row_copy(s, t, kk):
        src = pl.multiple_of(pos_smem[s, kk, t], SUBLANES)
        return pltpu.make_async_copy(
            y_hbm.at[pl.ds(src, SUBLANES)],
            buf_ref.at[s, kk, pl.ds(pl.multiple_of(t * SUBLANES, SUBLANES), SUBLANES)], sem_d.at[s])

    @pl.when(i == 0)
    def _():
        _issue_rows(gather_rows(0, 0), tm)

    @pl.when(i + 1 < n)
    def _():
        _issue_rows(gather_rows(i + 1, 1 - slot), tm)

    _drain_rows(functools.partial(row_copy, slot), tm)

    route = route_ref[...]
    lane = lax.broadcasted_iota(I32, (tm, LANES), 1)
    ws = [jnp.sum(jnp.where(lane == TOP_K + kk, route, 0.0), axis=-1, keepdims=True)
          for kk in range(TOP_K)]
    for c in range(SUBLANES):
        sl = slice(c * LANES, (c + 1) * LANES)
        acc = h_ref[:, sl]
        for kk in range(TOP_K):
            acc = acc + ws[kk] * buf_ref[slot, kk, pl.ds(c, tm, stride=SUBLANES), :]
        o_ref[:, sl] = acc


def _combine(pos, route, h1, y, *, tm):
    T, D = h1.shape
    W = y.shape[1]
    return pl.pallas_call(
        functools.partial(_combine_body, tm=tm),
        grid=(T // tm,),
        in_specs=[pl.BlockSpec((SUBLANES, T), lambda i: (0, 0)),
                  pl.BlockSpec((tm, LANES), lambda i: (i, 0)),
                  pl.BlockSpec((tm, D), lambda i: (i, 0)),
                  pl.BlockSpec(memory_space=pl.ANY)],
        out_specs=pl.BlockSpec((tm, D), lambda i: (i, 0)),
        out_shape=jax.ShapeDtypeStruct((T, D), F32),
        scratch_shapes=[pltpu.SMEM((2, SUBLANES, tm), I32),
                        pltpu.VMEM((2, TOP_K, tm * SUBLANES, W), F32),
                        pltpu.SemaphoreType.DMA, pltpu.SemaphoreType.DMA((2,))],
        compiler_params=_cparams(("arbitrary",)),
        name="moe_combine",
    )(pos, route, h1, y)


def _blockdiag_dense(w, width):
    nb, bs, _ = w.shape
    per = width // bs
    wt = w.reshape(nb // per, per, bs, bs)
    eye = jnp.eye(per, dtype=w.dtype)
    dense = jnp.einsum('gpio,pq->gpiqo', wt, eye)
    return dense.reshape(nb // per, width, width)


def _rope_tables(S):
    half = DA_HEAD_DIM // 2
    inv = ROPE_THETA ** (-(jnp.arange(half, dtype=F32) * 2.0 / DA_HEAD_DIM))
    ang = jnp.arange(S, dtype=F32)[:, None] * inv[None, :]
    cos = jnp.tile(jnp.cos(ang), (1, LANES // half))
    sign = jnp.asarray(np.where(_HEAD_LANE_HALF == 0, -1.0, 1.0), F32)
    sin = jnp.tile(jnp.sin(ang), (1, LANES // half)) * sign[None, :]
    return cos, sin


_HEAD_LANE = np.arange(LANES)
_HEAD_LANE_HALF = _HEAD_LANE // (LANES // 2)
_HEAD_LANE_COMP = (_HEAD_LANE // (DA_HEAD_DIM // 2)) % 2
_HEAD_LANE_DIM = _HEAD_LANE_HALF * (DA_HEAD_DIM // 2) + _HEAD_LANE % (DA_HEAD_DIM // 2)
_HEAD_LANE_SRC = _HEAD_LANE_COMP * DA_HEAD_DIM + _HEAD_LANE_DIM


def _tile(n, pref):
    return pref if n % pref == 0 else n


def _layer(h2, mem2, B, S, lambda_init, attn_norm_g, w_in, b_gate, da_q_norm_g, da_k_norm_g,
           da_lambda_q1, da_lambda_k1, da_lambda_q2, da_lambda_k2, da_subln_g, ml_conv_w, ml_conv_b,
           ml_wq, ml_wk, ml_wv, ml_w_if, ml_b_if, ml_out_norm_g, ml_skip, mem_norm_g, ca_w_kv,
           ca_q_norm_g, ca_k_norm_g, w_branch_da, w_branch_ml, w_branch_ca, w_out, ffn_norm_g,
           router_w, router_b, w_gate_up, b_gate_up, w_down, b_down):
    T, D = h2.shape
    M = mem2.shape[0] // B
    QK = DA_HEADS * 2 * DA_HEAD_DIM
    VW = DA_HEADS * DA_V_DIM
    C = ml_conv_w.shape[1]
    CAW = ca_w_kv.shape[1] // 2
    E = router_w.shape[1]

    o = [0, QK, 2 * QK, 2 * QK + VW, 2 * QK + VW + C, 2 * QK + VW + 2 * C, 2 * QK + VW + 2 * C + CAW]
    head_perm = (np.arange(DA_HEADS)[:, None] * LANES + _HEAD_LANE_SRC[None, :]).reshape(-1)
    w_re = jnp.concatenate([w_in[:, o[0]:o[1]][:, head_perm], w_in[:, o[1]:o[2]][:, head_perm],
                            w_in[:, o[2]:o[3]], w_in[:, o[5]:o[6]], w_in[:, o[3]:o[5]], w_in[:, o[6]:]],
                           axis=1).astype(BF16)
    tn = 1024
    col_v, col_caq, col_mlx, col_mlz, col_gate = 2 * QK, 2 * QK + VW, 2 * QK + VW + CAW, \
        2 * QK + VW + CAW + C, 2 * QK + VW + CAW + 2 * C
    proj = _inproj(h2, attn_norm_g[None, :], w_re, tm=_tile(T, 2048), tn=tn)

    g2 = jnp.stack([da_q_norm_g[_HEAD_LANE_DIM], da_k_norm_g[_HEAD_LANE_DIM]])[:, None, :]
    cos_t, sin_t = _rope_tables(S)
    gm = jnp.asarray(_HEAD_LANE_COMP[:, None] == _HEAD_LANE_COMP[None, :], BF16)
    qk = _qkrope(proj, g2, cos_t, sin_t, gm, T=T, S=S, width=QK, tm=_tile(S, 1024))
    y_da = _diff_attention(qk, proj, da_lambda_q1[None, :], da_lambda_k1[None, :], da_lambda_q2[None, :],
                           da_lambda_k2[None, :], da_subln_g[None, :], da_q_norm_g[None, :],
                           da_k_norm_g[None, :], B=B, S=S, tq=_tile(S, 512),
                           v_col0=col_v // LANES, q_scale=DA_Q_SCALE, lambda_init=lambda_init)

    bd = jnp.stack([_blockdiag_dense(ml_wq, MXU_DIM), _blockdiag_dense(ml_wk, MXU_DIM),
                    _blockdiag_dense(ml_wv, MXU_DIM)]).astype(BF16)
    wif = jnp.pad(ml_w_if.reshape(3, C, 2 * ML_HEADS), ((0, 0), (0, 0), (0, LANES - 2 * ML_HEADS))).astype(BF16)
    bif = jnp.pad(ml_b_if, (0, LANES - 2 * ML_HEADS))[None, :]
    xc, mq, mk, mv, gates = _mlpre(proj, ml_conv_w, ml_conv_b[None, :], bd, wif, bif, B=B, S=S, C=C,
                                   tm=_tile(S, 512), x_col=col_mlx // C)
    dml = C // ML_HEADS
    y_ml = _mlstm(mq, mk, mv, gates, xc, proj, ml_out_norm_g[None, :], ml_skip[None, :], B=B, S=S, C=C,
                  L=_tile(S, 256), z_col0=col_mlz // dml, hpb=4)

    kmem, vmem = _memkv(mem2, mem_norm_g[None, :], ca_w_kv.astype(BF16), ca_k_norm_g[None, :], B=B, M=M, W=CAW)

    tmx = _tile(S, 512)
    tri = (jnp.arange(tmx)[None, :] < jnp.arange(tmx)[:, None]).astype(BF16)
    rw = jnp.pad(router_w, ((0, 0), (0, LANES - E))).astype(BF16)
    rb = jnp.pad(router_b, (0, LANES - E))[None, :]
    h1, xp, route, cnt = _mix(h2, y_da, y_ml, kmem, vmem, ca_q_norm_g[None, :], proj,
                              b_gate.reshape(N_BRANCH, 1, D),
                              w_branch_da.astype(BF16), w_branch_ml.astype(BF16),
                              w_branch_ca.astype(BF16), w_out.astype(BF16), ffn_norm_g[None, :], rw, rb, tri,
                              tm=tmx, S=S, caq_col=col_caq // CAW, g_col=col_gate // D, n_exp=E)

    tg = _tile(T, 512)
    n_tiles = (T * TOP_K) // tg + E
    assert D == SUBLANES * LANES, "MoE rows are moved as one (8,128) f32 tile each"
    pos, te, n_used, xs = _scatter(route, cnt, xp, tm=_tile(T, 512), tg=tg, n_exp=E, n_tiles=n_tiles)
    y = _experts(te[:, 0], n_used[0, :1], xs, w_gate_up, b_gate_up[:, None, :],
                 w_down, b_down[:, None, :], tg=tg, n_tiles=n_tiles)
    return _combine(pos, route, h1, y, tm=_tile(T, 256))


def kernel(x, mem, attn_norm_g, w_in, b_gate, da_q_norm_g, da_k_norm_g, da_lambda_q1, da_lambda_k1, da_lambda_q2, da_lambda_k2, da_subln_g, ml_conv_w, ml_conv_b, ml_wq, ml_wk, ml_wv, ml_w_if, ml_b_if, ml_out_norm_g, ml_skip, mem_norm_g, ca_w_kv, ca_q_norm_g, ca_k_norm_g, w_branch_da, w_branch_ml, w_branch_ca, w_out, ffn_norm_g, router_w, router_b, w_gate_up, b_gate_up, w_down, b_down):
    B, S, D = x.shape
    depth = w_in.shape[0]
    h2 = x.reshape(B * S, D)
    mem2 = mem.reshape(B * mem.shape[1], D)
    params = (attn_norm_g, w_in, b_gate, da_q_norm_g, da_k_norm_g, da_lambda_q1, da_lambda_k1,
              da_lambda_q2, da_lambda_k2, da_subln_g, ml_conv_w, ml_conv_b, ml_wq, ml_wk, ml_wv, ml_w_if,
              ml_b_if, ml_out_norm_g, ml_skip, mem_norm_g, ca_w_kv, ca_q_norm_g, ca_k_norm_g, w_branch_da,
              w_branch_ml, w_branch_ca, w_out, ffn_norm_g, router_w, router_b, w_gate_up, b_gate_up,
              w_down, b_down)
    for l in range(depth):
        lambda_init = 0.8 - 0.6 * math.exp(-0.3 * l)
        h2 = _layer(h2, mem2, B, S, lambda_init, *[p[l] for p in params])
    return h2.reshape(B, S, D)
```

```python
import functools
import math

import jax
import jax.numpy as jnp
import numpy as np
from jax import lax
from jax.experimental import pallas as pl
from jax.experimental.pallas import tpu as pltpu

F32 = jnp.float32
BF16 = jnp.bfloat16
I32 = jnp.int32
U32 = jnp.uint32

NORM_EPS = 1e-6
ROPE_THETA = 10000.0
CHUNK = 64

DA_HEADS = 8
DA_HEAD_DIM = 64
DA_V_DIM = 128
ML_HEADS = 4
ML_CONV = 4
ML_QKV_BLOCK = 4
CA_HEADS = 4
N_BRANCH = 3
TOP_K = 4
SWIGLU_LIMIT = 7.0
SWIGLU_ALPHA = 1.702

LANES = 128
SUBLANES = 8
MXU_DIM = 256
VMEM_LIMIT = 56 * 1024 * 1024
NEG_BIG = -1e30
DA_Q_SCALE = DA_HEAD_DIM ** -0.5 * math.log2(math.e)
DA_BOUND_SLACK = 1.01
DA_BOUND_LIMIT = 40.0


def _cparams(sem):
    return pltpu.CompilerParams(dimension_semantics=sem, vmem_limit_bytes=VMEM_LIMIT)


def _sigmoid(x):
    return 0.5 * jnp.tanh(0.5 * x) + 0.5


def _silu(x):
    return x * _sigmoid(x)


def _inproj_body(x_ref, g_ref, w_ref, o_ref, xn_ref):
    j = pl.program_id(1)

    @pl.when(j == 0)
    def _():
        x = x_ref[...]
        ms = jnp.mean(x * x, axis=-1, keepdims=True)
        xn_ref[...] = (x * lax.rsqrt(ms + NORM_EPS) * g_ref[...]).astype(BF16)

    o_ref[...] = jnp.dot(xn_ref[...], w_ref[...], preferred_element_type=F32).astype(o_ref.dtype)


def _inproj(x2, g, w_bf, *, tm, tn):
    T, D = x2.shape
    N = w_bf.shape[1]
    return pl.pallas_call(
        _inproj_body,
        grid=(T // tm, N // tn),
        in_specs=[
            pl.BlockSpec((tm, D), lambda i, j: (i, 0)),
            pl.BlockSpec((1, D), lambda i, j: (0, 0)),
            pl.BlockSpec((D, tn), lambda i, j: (0, j)),
        ],
        out_specs=pl.BlockSpec((tm, tn), lambda i, j: (i, j)),
        out_shape=jax.ShapeDtypeStruct((T, N), BF16),
        scratch_shapes=[pltpu.VMEM((tm, D), BF16)],
        compiler_params=_cparams(("arbitrary", "arbitrary")),
        name="inproj",
    )(x2, g, w_bf)


def _group_sumsq(xb, gm):
    sq = xb * xb
    hi = sq.astype(BF16)
    lo = (sq - hi.astype(F32)).astype(BF16)
    return (jnp.dot(hi, gm, preferred_element_type=F32)
            + jnp.dot(lo, gm, preferred_element_type=F32))


def _qkrope_body(x_ref, g_ref, cos_ref, sin_ref, gm_ref, o_ref, *, ncb, scale):
    c = pl.program_id(1)
    g = g_ref[...] * jnp.where(c == 0, scale, 1.0).astype(F32)
    cos = cos_ref[...]
    sin = sin_ref[...]
    gm = gm_ref[...]
    for cb in range(ncb):
        sl = slice(cb * LANES, (cb + 1) * LANES)
        xb = x_ref[:, sl].astype(F32)
        ss = _group_sumsq(xb, gm)
        y = xb * lax.rsqrt(ss * (1.0 / DA_HEAD_DIM) + NORM_EPS) * g
        sw = pltpu.roll(y, LANES // 2, 1)
        o_ref[:, sl] = (y * cos + sw * sin).astype(o_ref.dtype)


def _qkrope(proj, g2, cos_t, sin_t, gm, *, T, S, width, tm):
    n_s = S // tm
    return pl.pallas_call(
        functools.partial(_qkrope_body, ncb=width // LANES, scale=DA_Q_SCALE),
        grid=(T // tm, 2),
        in_specs=[
            pl.BlockSpec((tm, width), lambda i, c: (i, c)),
            pl.BlockSpec((None, 1, LANES), lambda i, c: (c, 0, 0)),
            pl.BlockSpec((tm, LANES), lambda i, c: (i % n_s, 0)),
            pl.BlockSpec((tm, LANES), lambda i, c: (i % n_s, 0)),
            pl.BlockSpec((LANES, LANES), lambda i, c: (0, 0)),
        ],
        out_specs=pl.BlockSpec((tm, width), lambda i, c: (i, c)),
        out_shape=jax.ShapeDtypeStruct((T, 2 * width), BF16),
        compiler_params=_cparams(("arbitrary", "arbitrary")),
        name="qkrope",
    )(proj, g2, cos_t, sin_t, gm)


def _da_body(q_ref, k_ref, v_ref, lq1_ref, lk1_ref, lq2_ref, lk2_ref, subg_ref, qg_ref, kg_ref, o_ref,
             m_ref, acc_ref, kt_ref, bound_ref, *, tq, tk, n_kv, q_scale, lambda_init):
    i = pl.program_id(2)
    q = q_ref[...]
    lane = lax.broadcasted_iota(I32, (1, LANES), 1)
    lo = (lane & (DA_HEAD_DIM // 2)) == 0
    zero = jnp.zeros_like(q)
    qc = (jnp.where(lo, q, zero), jnp.where(lo, zero, q))

    @pl.when(i == 0)
    def _():
        def tbody(j, carry):
            start = pl.multiple_of(j * tk, tk)
            kt_ref[:, pl.ds(start, tk)] = k_ref[pl.ds(start, tk), :].astype(F32).T.astype(BF16)
            return carry
        lax.fori_loop(0, n_kv, tbody, 0)
        bound_ref[0] = (DA_HEAD_DIM * q_scale * DA_BOUND_SLACK
                        * jnp.max(jnp.abs(qg_ref[...])) * jnp.max(jnp.abs(kg_ref[...])))

    bound = bound_ref[0]

    acc_ref[...] = jnp.zeros(acc_ref.shape, F32)
    ones = jnp.ones((tk, LANES), BF16)

    def kt_tile(j):
        return kt_ref[:, pl.ds(pl.multiple_of(j * tk, tk), tk)]

    def v_aug(j):
        return jnp.concatenate([v_ref[pl.ds(pl.multiple_of(j * tk, tk), tk), :], ones], axis=1)

    def scores(j):
        ktj = kt_tile(j)
        return [jnp.dot(qc[c], ktj, preferred_element_type=F32) for c in range(2)]

    def allowed(r0, rows, c0):
        r = lax.broadcasted_iota(I32, (rows, tk), 0) + r0
        cidx = lax.broadcasted_iota(I32, (rows, tk), 1) + c0
        return (cidx | (CHUNK - 1)) <= (r | (CHUNK - 1))

    def consume_bounded(j, ss):
        vj = v_aug(j)
        ps = [jnp.exp2(ss[c] - bound).astype(BF16) for c in range(2)]
        pvs = [jnp.dot(ps[c], vj, preferred_element_type=F32) for c in range(2)]
        for c in range(2):
            acc_ref[c] = acc_ref[c] + pvs[c]

    def consume_online(j, ss, mask=None):
        vj = v_aug(j)
        ps, alphas = [], []
        for c in range(2):
            s = ss[c] if mask is None else jnp.where(mask, ss[c], NEG_BIG)
            m_old = m_ref[c]
            m_new = jnp.maximum(m_old, jnp.max(s, axis=-1, keepdims=True))
            ps.append(jnp.exp2(s - jnp.concatenate([m_new] * (tk // LANES), axis=1)).astype(BF16))
            alphas.append(jnp.exp2(m_old - m_new))
            m_ref[c] = m_new
        pvs = [jnp.dot(ps[c], vj, preferred_element_type=F32) for c in range(2)]
        for c in range(2):
            acc_ref[c] = jnp.concatenate([alphas[c], alphas[c]], axis=1) * acc_ref[c] + pvs[c]

    def block_bounded(r0, j, on_diagonal):
        ktj = kt_tile(j)
        vj = v_aug(j)
        if not on_diagonal:
            s = [jnp.dot(qc[c][r0:r0 + tk], ktj, preferred_element_type=F32) for c in range(2)]
            p = [jnp.exp2(s[c] - bound).astype(BF16) for c in range(2)]
            pv = [jnp.dot(p[c], vj, preferred_element_type=F32) for c in range(2)]
            for c in range(2):
                acc_ref[c, pl.ds(r0, tk)] = acc_ref[c, pl.ds(r0, tk)] + pv[c]
            return
        hq = tk // 2
        mask_top = allowed(0, hq, 0)[:, :hq]
        mask_bot = allowed(hq, hq, 0)
        s_top = [jnp.dot(qc[c][r0:r0 + hq], ktj[:, :hq], preferred_element_type=F32) for c in range(2)]
        s_bot = [jnp.dot(qc[c][r0 + hq:r0 + tk], ktj, preferred_element_type=F32) for c in range(2)]
        p_top = [jnp.where(mask_top, jnp.exp2(s_top[c] - bound), 0.0).astype(BF16) for c in range(2)]
        p_bot = [jnp.where(mask_bot, jnp.exp2(s_bot[c] - bound), 0.0).astype(BF16) for c in range(2)]
        pv_top = [jnp.dot(p_top[c], vj[:hq], preferred_element_type=F32) for c in range(2)]
        pv_bot = [jnp.dot(p_bot[c], vj, preferred_element_type=F32) for c in range(2)]
        for c in range(2):
            acc_ref[c, pl.ds(r0, hq)] = acc_ref[c, pl.ds(r0, hq)] + pv_top[c]
            acc_ref[c, pl.ds(r0 + hq, hq)] = acc_ref[c, pl.ds(r0 + hq, hq)] + pv_bot[c]

    def diag_bounded():
        block_bounded(0, 2 * i, True)
        block_bounded(tk, 2 * i, False)
        block_bounded(tk, 2 * i + 1, True)

    def diag_online():
        consume_online(2 * i, scores(2 * i), allowed(0, tq, 0))
        consume_online(2 * i + 1, scores(2 * i + 1), allowed(0, tq, tk))

    def sweep(consume, diag):
        def pair(jj, carry):
            sa = scores(2 * jj)
            sb = scores(2 * jj + 1)
            consume(2 * jj, sa)
            consume(2 * jj + 1, sb)
            return carry

        lax.fori_loop(0, i, pair, 0)
        diag()

    @pl.when(bound < DA_BOUND_LIMIT)
    def _():
        sweep(consume_bounded, diag_bounded)

    @pl.when(jnp.logical_not(bound < DA_BOUND_LIMIT))
    def _():
        m_ref[...] = jnp.full(m_ref.shape, NEG_BIG, F32)
        sweep(consume_online, diag_online)

    f32 = F32
    lam = (jnp.exp(jnp.sum(lq1_ref[...].astype(f32) * lk1_ref[...].astype(f32), keepdims=True))
           - jnp.exp(jnp.sum(lq2_ref[...].astype(f32) * lk2_ref[...].astype(f32), keepdims=True))
           + lambda_init)
    a1 = acc_ref[0]
    a2 = acc_ref[1]
    o = a1[:, :DA_V_DIM] / a1[:, DA_V_DIM:] - lam * (a2[:, :DA_V_DIM] / a2[:, DA_V_DIM:])
    ms = jnp.mean(o * o, axis=-1, keepdims=True)
    o = o * lax.rsqrt(ms + NORM_EPS) * subg_ref[...]
    o_ref[...] = (o * (1.0 - lambda_init)).astype(o_ref.dtype)


def _diff_attention(qk, proj, lq1, lk1, lq2, lk2, subg, qg, kg, *, B, S, tq, v_col0, q_scale, lambda_init):
    T = B * S
    H = DA_HEADS
    n_q = S // tq
    tk = tq // 2
    assert tk % LANES == 0 and tk % (2 * CHUNK) == 0
    vec = lambda n: pl.BlockSpec((1, n), lambda b, h, i: (0, 0))
    return pl.pallas_call(
        functools.partial(_da_body, tq=tq, tk=tk, n_kv=S // tk, q_scale=q_scale, lambda_init=lambda_init),
        grid=(B, H, n_q),
        in_specs=[
            pl.BlockSpec((tq, LANES), lambda b, h, i: (b * n_q + i, h)),
            pl.BlockSpec((S, LANES), lambda b, h, i: (b, H + h)),
            pl.BlockSpec((S, LANES), lambda b, h, i: (b, v_col0 + h)),
            vec(DA_HEAD_DIM), vec(DA_HEAD_DIM), vec(DA_HEAD_DIM), vec(DA_HEAD_DIM),
            vec(DA_V_DIM), vec(DA_HEAD_DIM), vec(DA_HEAD_DIM),
        ],
        out_specs=pl.BlockSpec((tq, LANES), lambda b, h, i: (b * n_q + i, h)),
        out_shape=jax.ShapeDtypeStruct((T, H * DA_V_DIM), BF16),
        scratch_shapes=[pltpu.VMEM((2, tq, LANES), F32), pltpu.VMEM((2, tq, 2 * DA_V_DIM), F32),
                        pltpu.VMEM((LANES, S), BF16), pltpu.SMEM((1,), F32)],
        compiler_params=_cparams(("arbitrary", "arbitrary", "arbitrary")),
        name="diff_attn",
    )(qk, qk, proj, lq1, lk1, lq2, lk2, subg, qg, kg)


def _mlpre_body(x_ref, cw_ref, cb_ref, bd_ref, wif_ref, bif_ref,
                xc_ref, q_ref, k_ref, v_ref, g_ref, prev_ref, *, tm, ncb, kscale):
    s_idx = pl.program_id(1)

    @pl.when(s_idx == 0)
    def _():
        prev_ref[...] = jnp.zeros(prev_ref.shape, F32)

    row8 = lax.broadcasted_iota(I32, (SUBLANES, MXU_DIM), 0)
    gacc = jnp.zeros((tm, LANES), F32) + bif_ref[...]
    for cb in range(ncb):
        sl = slice(cb * MXU_DIM, (cb + 1) * MXU_DIM)
        xb16 = x_ref[:, sl]
        x = xb16.astype(F32)
        prev = prev_ref[:, sl]
        conv = x * cw_ref[ML_CONV - 1:ML_CONV, sl] + cb_ref[:, sl]
        for d in range(1, ML_CONV):
            xs = pltpu.roll(x, d, 0)
            ps = pltpu.roll(prev, d, 0)
            head = jnp.where(row8 < d, ps, xs[:SUBLANES])
            shifted = jnp.concatenate([head, xs[SUBLANES:]], axis=0)
            conv = conv + shifted * cw_ref[ML_CONV - 1 - d:ML_CONV - d, sl]
        prev_ref[:, sl] = x[tm - SUBLANES:]
        xc = _silu(conv)
        xc16 = xc.astype(BF16)
        q = jnp.dot(xc16, bd_ref[0, cb], preferred_element_type=F32)
        k = jnp.dot(xc16, bd_ref[1, cb], preferred_element_type=F32)
        v = jnp.dot(xb16, bd_ref[2, cb], preferred_element_type=F32)
        q16, k16, v16 = q.astype(BF16), k.astype(BF16), v.astype(BF16)
        gacc = gacc + jnp.dot(q16, wif_ref[0, sl, :], preferred_element_type=F32)
        gacc = gacc + jnp.dot(k16, wif_ref[1, sl, :], preferred_element_type=F32)
        gacc = gacc + jnp.dot(v16, wif_ref[2, sl, :], preferred_element_type=F32)
        xc_ref[:, sl] = xc16
        q_ref[:, sl] = q16
        k_ref[:, sl] = (k * kscale).astype(BF16)
        v_ref[:, sl] = v16
    g_ref[...] = gacc


def _mlpre(proj, conv_w, conv_b, bd, wif, bif, *, B, S, C, tm, x_col):
    T = B * S
    n_s = S // tm
    row = lambda i: pl.BlockSpec((tm, C), lambda b, s: (b * n_s + s, i))
    outs = pl.pallas_call(
        functools.partial(_mlpre_body, tm=tm, ncb=C // MXU_DIM, kscale=(C // ML_HEADS) ** -0.5),
        grid=(B, n_s),
        in_specs=[
            row(x_col),
            pl.BlockSpec((ML_CONV, C), lambda b, s: (0, 0)),
            pl.BlockSpec((1, C), lambda b, s: (0, 0)),
            pl.BlockSpec(bd.shape, lambda b, s: (0, 0, 0, 0)),
            pl.BlockSpec(wif.shape, lambda b, s: (0, 0, 0)),
            pl.BlockSpec((1, LANES), lambda b, s: (0, 0)),
        ],
        out_specs=[row(0), row(0), row(0), row(0),
                   pl.BlockSpec((tm, LANES), lambda b, s: (b * n_s + s, 0))],
        out_shape=[jax.ShapeDtypeStruct((T, C), BF16)] * 4 + [jax.ShapeDtypeStruct((T, LANES), F32)],
        scratch_shapes=[pltpu.VMEM((SUBLANES, C), F32)],
        compiler_params=_cparams(("arbitrary", "arbitrary")),
        name="mlstm_pre",
    )(proj, conv_w, conv_b, bd, wif, bif)
    return outs


def _mlstm_body(q_ref, k_ref, v_ref, g_ref, xc_ref, z_ref, ng_ref, skip_ref, o_ref,
                c_ref, cb_ref, n_ref, m_ref, *, L, d, hpb):
    hg = pl.program_id(1)
    c_idx = pl.program_id(2)

    @pl.when(c_idx == 0)
    def _():
        c_ref[...] = jnp.zeros(c_ref.shape, F32)
        cb_ref[...] = jnp.zeros(cb_ref.shape, BF16)
        n_ref[...] = jnp.zeros(n_ref.shape, F32)
        m_ref[...] = jnp.zeros(m_ref.shape, F32)

    g = g_ref[...]
    gt = g.T
    lane = lax.broadcasted_iota(I32, (1, LANES), 1)
    sub = lax.broadcasted_iota(I32, (LANES, 1), 0)
    t_idx = lax.broadcasted_iota(I32, (L, L), 0)
    s_idx = lax.broadcasted_iota(I32, (L, L), 1)
    tri = s_idx <= t_idx

    for hh in range(hpb):
        h = hg * hpb + hh
        sl = slice(hh * d, (hh + 1) * d)
        i_col = jnp.sum(jnp.where(lane == h, g, 0.0), axis=1, keepdims=True)
        f_col = jnp.sum(jnp.where(lane == ML_HEADS + h, g, 0.0), axis=1, keepdims=True)
        i_row = jnp.sum(jnp.where(sub == h, gt, 0.0), axis=0, keepdims=True)
        f_row = jnp.sum(jnp.where(sub == ML_HEADS + h, gt, 0.0), axis=0, keepdims=True)
        lf_col = jax.nn.log_sigmoid(f_col)
        lf_row = jax.nn.log_sigmoid(f_row)

        b_col = jnp.sum(jnp.where(tri, lf_row, 0.0), axis=1, keepdims=True)
        b_row = jnp.sum(jnp.where(t_idx <= s_idx, lf_col, 0.0), axis=0, keepdims=True)
        b_last = jnp.sum(lf_row, axis=1, keepdims=True)

        m_prev = m_ref[hh]
        dlog = jnp.where(tri, b_col - b_row + i_row, NEG_BIG)
        inter_log = b_col + m_prev
        m_rowv = jnp.maximum(inter_log, jnp.max(dlog, axis=1, keepdims=True))
        dw = jnp.exp(dlog - m_rowv)
        inter_w = jnp.exp(inter_log - m_rowv)

        q = q_ref[:, sl]
        k = k_ref[:, sl]
        v = v_ref[:, sl]
        qk = lax.dot_general(q, k, (((1,), (1,)), ((), ())), preferred_element_type=F32)
        qc = jnp.dot(q, cb_ref[hh], preferred_element_type=F32)
        s = qk * dw

        state_row = b_last - b_row + i_row
        state_col = b_last - b_col + i_col
        m_new = jnp.maximum(b_last + m_prev, jnp.max(state_row, axis=1, keepdims=True))
        decay = jnp.exp(b_last + m_prev - m_new)
        ws_col = jnp.exp(state_col - m_new)
        kwf = k.astype(F32) * ws_col

        num = jnp.dot(s.astype(BF16), v, preferred_element_type=F32) + inter_w * qc
        dc = lax.dot_general(kwf.astype(BF16), v, (((0,), (0,)), ((), ())), preferred_element_type=F32)
        den = (jnp.sum(s, axis=1, keepdims=True)
               + inter_w * jnp.sum(q.astype(F32) * n_ref[hh], axis=1, keepdims=True))
        hv = num / jnp.maximum(jnp.abs(den), jnp.exp(-m_rowv))

        c_new = decay * c_ref[hh] + dc
        c_ref[hh] = c_new
        cb_ref[hh] = c_new.astype(BF16)
        n_ref[hh] = decay * n_ref[hh] + jnp.sum(kwf, axis=0, keepdims=True)
        m_ref[hh] = m_new

        mu = jnp.mean(hv, axis=-1, keepdims=True)
        cen = hv - mu
        var = jnp.mean(cen * cen, axis=-1, keepdims=True)
        hn = cen * lax.rsqrt(var + NORM_EPS) * ng_ref[:, sl]
        y = (hn + skip_ref[:, sl] * xc_ref[:, sl].astype(F32)) * _silu(z_ref[:, sl].astype(F32))
        o_ref[:, sl] = y.astype(o_ref.dtype)


def _mlstm(q, k, v, gates, xc, proj, norm_g, skip, *, B, S, C, L, z_col0, hpb):
    T = B * S
    H = ML_HEADS
    d = C // H
    w = hpb * d
    nc = S // L
    blk = lambda off: pl.BlockSpec((L, w), lambda b, h, c: (b * nc + c, off + h))
    par = pl.BlockSpec((1, w), lambda b, h, c: (0, h))
    return pl.pallas_call(
        functools.partial(_mlstm_body, L=L, d=d, hpb=hpb),
        grid=(B, H // hpb, nc),
        in_specs=[blk(0), blk(0), blk(0),
                  pl.BlockSpec((L, LANES), lambda b, h, c: (b * nc + c, 0)),
                  blk(0), blk(z_col0 // hpb), par, par],
        out_specs=blk(0),
        out_shape=jax.ShapeDtypeStruct((T, C), BF16),
        scratch_shapes=[pltpu.VMEM((hpb, d, d), F32), pltpu.VMEM((hpb, d, d), BF16),
                        pltpu.VMEM((hpb, 1, d), F32), pltpu.VMEM((hpb, 1, 1), F32)],
        compiler_params=_cparams(("arbitrary", "arbitrary", "arbitrary")),
        name="mlstm_scan",
    )(q, k, v, gates, xc, proj, norm_g, skip)


def _memkv_body(mem_ref, mg_ref, w_ref, kg_ref, k_ref, v_ref, *, W, dh):
    x = mem_ref[...]
    ms = jnp.mean(x * x, axis=-1, keepdims=True)
    xn = (x * lax.rsqrt(ms + NORM_EPS) * mg_ref[...]).astype(BF16)
    kv = jnp.dot(xn, w_ref[...], preferred_element_type=F32)
    for hd in range(W // dh):
        sl = slice(hd * dh, (hd + 1) * dh)
        kh = kv[:, sl]
        msk = jnp.mean(kh * kh, axis=-1, keepdims=True)
        k_ref[:, sl] = (kh * lax.rsqrt(msk + NORM_EPS) * kg_ref[...]).astype(BF16)
    v_ref[...] = kv[:, W:].astype(BF16)


def _memkv(mem2, mem_g, w_kv_bf, k_g, *, B, M, W):
    D = mem2.shape[1]
    dh = W // CA_HEADS
    return pl.pallas_call(
        functools.partial(_memkv_body, W=W, dh=dh),
        grid=(B,),
        in_specs=[pl.BlockSpec((M, D), lambda b: (b, 0)),
                  pl.BlockSpec((1, D), lambda b: (0, 0)),
                  pl.BlockSpec((D, 2 * W), lambda b: (0, 0)),
                  pl.BlockSpec((1, dh), lambda b: (0, 0))],
        out_specs=[pl.BlockSpec((M, W), lambda b: (b, 0))] * 2,
        out_shape=[jax.ShapeDtypeStruct((B * M, W), BF16)] * 2,
        compiler_params=_cparams(("arbitrary",)),
        name="mem_kv",
    )(mem2, mem_g, w_kv_bf, k_g)


def _xattn_tile(q_ref, k_ref, v_ref, qg_ref):
    W = q_ref.shape[1]
    dh = W // CA_HEADS
    scale = dh ** -0.5
    outs = []
    for hd in range(CA_HEADS):
        sl = slice(hd * dh, (hd + 1) * dh)
        qh = q_ref[:, sl].astype(F32)
        ms = jnp.mean(qh * qh, axis=-1, keepdims=True)
        qn = (qh * lax.rsqrt(ms + NORM_EPS) * (qg_ref[...] * scale)).astype(BF16)
        s = lax.dot_general(qn, k_ref[:, sl], (((1,), (1,)), ((), ())), preferred_element_type=F32)
        mx = jnp.max(s, axis=-1, keepdims=True)
        p = jnp.exp(s - mx)
        p = p / jnp.sum(p, axis=-1, keepdims=True)
        outs.append(jnp.dot(p.astype(BF16), v_ref[:, sl], preferred_element_type=F32).astype(BF16))
    return jnp.concatenate(outs, axis=1)


def _store_tile_rows(ref, x):
    n = x.shape[0]
    for c in range(SUBLANES):
        ref[pl.ds(c, n, stride=SUBLANES), :] = x[:, c * LANES:(c + 1) * LANES]


def _load_tile_rows(ref, n):
    return [ref[pl.ds(c, n, stride=SUBLANES), :] for c in range(SUBLANES)]


def _mix_body(x_ref, yda_ref, yml_ref, caq_ref, kmem_ref, vmem_ref, caqg_ref, gda_ref, gml_ref, gca_ref,
              bg_ref, wda_ref, wml_ref, wca_ref, wout_ref, fg_ref, rw_ref, rb_ref, tri_ref,
              h_ref, xp_ref, route_ref, cnt_ref, carry_ref, *, n_exp):
    i = pl.program_id(0)

    @pl.when(i == 0)
    def _():
        carry_ref[...] = jnp.zeros(carry_ref.shape, F32)

    def gate(g_ref, b):
        return _sigmoid(g_ref[...].astype(F32) + bg_ref[b])

    yca = _xattn_tile(caq_ref, kmem_ref, vmem_ref, caqg_ref)
    mix = (gate(gda_ref, 0) * jnp.dot(yda_ref[...], wda_ref[...], preferred_element_type=F32)
           + gate(gml_ref, 1) * jnp.dot(yml_ref[...], wml_ref[...], preferred_element_type=F32)
           + gate(gca_ref, 2) * jnp.dot(yca, wca_ref[...], preferred_element_type=F32))
    h1 = x_ref[...] + jnp.dot(mix.astype(BF16), wout_ref[...], preferred_element_type=F32)
    h_ref[...] = h1
    ms = jnp.mean(h1 * h1, axis=-1, keepdims=True)
    xn = h1 * lax.rsqrt(ms + NORM_EPS) * fg_ref[...]
    _store_tile_rows(xp_ref, xn)

    logits = jnp.dot(xn.astype(BF16), rw_ref[...], preferred_element_type=F32) + rb_ref[...]
    tm = logits.shape[0]
    lane = lax.broadcasted_iota(I32, (tm, LANES), 1)
    work = jnp.where(lane < n_exp, logits, NEG_BIG)
    sel = jnp.zeros((tm, LANES), F32)
    vals, idxs = [], []
    for _ in range(TOP_K):
        mx = jnp.max(work, axis=-1, keepdims=True)
        idx = jnp.min(jnp.where(work == mx, lane, LANES), axis=-1, keepdims=True)
        hit = lane == idx
        sel = jnp.where(hit, 1.0, sel)
        work = jnp.where(hit, NEG_BIG, work)
        vals.append(mx)
        idxs.append(idx)
    exps = [jnp.exp(v - vals[0]) for v in vals]
    tot = exps[0] + exps[1] + exps[2] + exps[3]

    cum = jnp.dot(tri_ref[...], sel.astype(BF16), preferred_element_type=F32) + carry_ref[...]
    route = jnp.zeros((tm, LANES), F32)
    for kk in range(TOP_K):
        rank = jnp.sum(jnp.where(lane == idxs[kk], cum, 0.0), axis=-1, keepdims=True)
        route = jnp.where(lane == kk, idxs[kk].astype(F32), route)
        route = jnp.where(lane == TOP_K + kk, exps[kk] / tot, route)
        route = jnp.where(lane == 2 * TOP_K + kk, rank, route)
    route_ref[...] = route
    carry_ref[...] = carry_ref[...] + jnp.sum(sel, axis=0, keepdims=True)
    cnt_ref[...] = carry_ref[...]


def _mix(x2, yda, yml, kmem, vmem, caqg, proj, bg, wda, wml, wca, wout, fg, rw, rb, tri,
         *, tm, S, caq_col, g_col, n_exp):
    T, D = x2.shape
    C = yml.shape[1]
    M = kmem.shape[0] // (T // S)
    W = kmem.shape[1]
    n_s = S // tm
    const = lambda shape: pl.BlockSpec(shape, lambda i: (0,) * len(shape))
    return pl.pallas_call(
        functools.partial(_mix_body, n_exp=n_exp),
        grid=(T // tm,),
        in_specs=[pl.BlockSpec((tm, D), lambda i: (i, 0)),
                  pl.BlockSpec((tm, D), lambda i: (i, 0)),
                  pl.BlockSpec((tm, C), lambda i: (i, 0)),
                  pl.BlockSpec((tm, W), lambda i: (i, caq_col)),
                  pl.BlockSpec((M, W), lambda i: (i // n_s, 0)),
                  pl.BlockSpec((M, W), lambda i: (i // n_s, 0)),
                  const((1, W // CA_HEADS)),
                  pl.BlockSpec((tm, D), lambda i: (i, g_col)),
                  pl.BlockSpec((tm, D), lambda i: (i, g_col + 1)),
                  pl.BlockSpec((tm, D), lambda i: (i, g_col + 2)),
                  const((N_BRANCH, 1, D)),
                  const((D, D)), const((C, D)), const((D, D)), const((D, D)),
                  const((1, D)), const((D, LANES)), const((1, LANES)), const((tm, tm))],
        out_specs=[pl.BlockSpec((tm, D), lambda i: (i, 0)),
                   pl.BlockSpec((tm * SUBLANES, LANES), lambda i: (i, 0)),
                   pl.BlockSpec((tm, LANES), lambda i: (i, 0)),
                   pl.BlockSpec((1, LANES), lambda i: (0, 0))],
        out_shape=[jax.ShapeDtypeStruct((T, D), F32),
                   jax.ShapeDtypeStruct((T * SUBLANES, LANES), F32),
                   jax.ShapeDtypeStruct((T, LANES), F32),
                   jax.ShapeDtypeStruct((1, LANES), F32)],
        scratch_shapes=[pltpu.VMEM((1, LANES), F32)],
        compiler_params=_cparams(("arbitrary",)),
        name="mix_route",
    )(x2, yda, yml, proj, kmem, vmem, caqg, proj, proj, proj, bg, wda, wml, wca, wout, fg, rw, rb, tri)


ROW_DMA_UNROLL = 8
PAD_CHUNK_LOG2 = 5
PAD_CHUNK = 1 << PAD_CHUNK_LOG2


def _issue_rows(row_copy, tm):
    def issue(t, carry):
        for kk in range(TOP_K):
            row_copy(t, kk).start(priority=kk % 2)
        return carry

    lax.fori_loop(0, tm, issue, 0, unroll=ROW_DMA_UNROLL)


def _drain_rows(row_copy, tm):
    def drain(t, carry):
        for kk in range(TOP_K):
            row_copy(t, kk).wait()
        return carry

    lax.fori_loop(0, tm, drain, 0, unroll=ROW_DMA_UNROLL)


def _issue_and_drain_rows(row_copy, tm):
    _issue_rows(row_copy, tm)
    _drain_rows(row_copy, tm)


def _group_starts(cnt, tg, n_exp):
    lane_r = lax.broadcasted_iota(I32, (LANES, LANES), 0)
    lane_c = lax.broadcasted_iota(I32, (LANES, LANES), 1)
    padded = jnp.ceil(cnt * (1.0 / tg)) * tg
    padded_col = jnp.sum(jnp.where(lane_r == lane_c, padded, 0.0), axis=1, keepdims=True)
    start = jnp.sum(jnp.where(lane_r < lane_c, padded_col, 0.0), axis=0, keepdims=True)
    return start, start + padded


def _scatter_body(route_ref, route_next_ref, cnt_ref, xp_ref, pos_ref, te_ref, nu_ref, xs_hbm,
                  posv_ref, pos_smem, padv_ref, pad_smem, zero_ref, sem_p, sem_d, sem_z,
                  *, tm, tg, n_exp, n_tiles):
    i = pl.program_id(0)
    n = pl.num_programs(0)
    slot = i & 1
    lane = lax.broadcasted_iota(I32, (tm, LANES), 1)
    start, end = _group_starts(cnt_ref[...], tg, n_exp)

    def stage_positions(route, s):
        posm = jnp.zeros((tm, LANES), F32)
        for kk in range(TOP_K):
            e = jnp.sum(jnp.where(lane == kk, route, 0.0), axis=-1, keepdims=True)
            rank = jnp.sum(jnp.where(lane == 2 * TOP_K + kk, route, 0.0), axis=-1, keepdims=True)
            st = jnp.sum(jnp.where(lane == e.astype(I32), start, 0.0), axis=-1, keepdims=True)
            posm = jnp.where(lane == kk, (st + rank) * SUBLANES, posm)
        posv_ref[s] = posm.astype(I32).T[:SUBLANES]
        cp = pltpu.make_async_copy(posv_ref.at[s], pos_smem.at[s], sem_p)
        cp.start()
        cp.wait()

    @pl.when(i == 0)
    def _():
        stage_positions(route_ref[...], 0)
        tstart = (lax.broadcasted_iota(I32, (n_tiles, LANES), 0) * tg).astype(F32)
        lane_t = lax.broadcasted_iota(I32, (n_tiles, LANES), 1)
        done = jnp.where((lane_t < n_exp) & (end <= tstart), 1.0, 0.0)
        te = jnp.sum(done, axis=-1, keepdims=True)
        te_ref[...] = jnp.broadcast_to(te, (n_tiles, LANES)).astype(I32)
        n_used = jnp.max(end, axis=-1, keepdims=True) * (1.0 / tg)
        nu_ref[...] = jnp.broadcast_to(n_used, (1, LANES)).astype(I32)

        cnt = cnt_ref[...]
        padv_ref[...] = jnp.zeros(padv_ref.shape, I32)
        padv_ref[0:1, :] = ((start + cnt) * SUBLANES).astype(I32)
        padv_ref[1:2, :] = (end - start - cnt).astype(I32)
        zero_ref[...] = jnp.zeros(zero_ref.shape, F32)
        cpz = pltpu.make_async_copy(padv_ref, pad_smem, sem_p)
        cpz.start()
        cpz.wait()

        def pad_copy(off, rows):
            n = rows * SUBLANES
            return pltpu.make_async_copy(zero_ref.at[pl.ds(0, n)],
                                         xs_hbm.at[pl.ds(pl.multiple_of(off, SUBLANES), n)], sem_z)

        def for_each_pad_copy(fn):
            def per_expert(e, carry):
                base = pad_smem[0, e]
                n_pad = pad_smem[1, e]
                n_chunks = lax.shift_right_logical(n_pad, PAD_CHUNK_LOG2)

                def per_chunk(c, c2):
                    fn(pad_copy(base + c * (PAD_CHUNK * SUBLANES), PAD_CHUNK))
                    return c2
                lax.fori_loop(0, n_chunks, per_chunk, 0)

                def per_row(r, c2):
                    fn(pad_copy(base + (n_chunks * PAD_CHUNK + r) * SUBLANES, 1))
                    return c2
                lax.fori_loop(0, n_pad - n_chunks * PAD_CHUNK, per_row, 0)
                return carry
            lax.fori_loop(0, n_exp, per_expert, 0)

        for_each_pad_copy(lambda cp: cp.start())
        for_each_pad_copy(lambda cp: cp.wait())

    def row_copy(t, kk):
        dst = pl.multiple_of(pos_smem[slot, kk, t], SUBLANES)
        return pltpu.make_async_copy(xp_ref.at[pl.ds(pl.multiple_of(t * SUBLANES, SUBLANES), SUBLANES)],
                                     xs_hbm.at[pl.ds(dst, SUBLANES)], sem_d)

    _issue_rows(row_copy, tm)

    @pl.when(i + 1 < n)
    def _():
        stage_positions(route_next_ref[...], 1 - slot)

    pos_ref[...] = posv_ref[slot]
    _drain_rows(row_copy, tm)


def _scatter(route, cnt, xp, *, tm, tg, n_exp, n_tiles):
    T = route.shape[0]
    W = xp.shape[1]
    n_rows = n_tiles * tg * SUBLANES
    n_steps = T // tm
    return pl.pallas_call(
        functools.partial(_scatter_body, tm=tm, tg=tg, n_exp=n_exp, n_tiles=n_tiles),
        grid=(n_steps,),
        in_specs=[pl.BlockSpec((tm, LANES), lambda i: (i, 0)),
                  pl.BlockSpec((tm, LANES), lambda i: (jnp.minimum(i + 1, n_steps - 1), 0)),
                  pl.BlockSpec((1, LANES), lambda i: (0, 0)),
                  pl.BlockSpec((tm * SUBLANES, W), lambda i: (i, 0))],
        out_specs=[pl.BlockSpec((SUBLANES, tm), lambda i: (0, i)),
                   pl.BlockSpec((n_tiles, LANES), lambda i: (0, 0)),
                   pl.BlockSpec((1, LANES), lambda i: (0, 0)),
                   pl.BlockSpec(memory_space=pl.ANY)],
        out_shape=[jax.ShapeDtypeStruct((SUBLANES, T), I32),
                   jax.ShapeDtypeStruct((n_tiles, LANES), I32),
                   jax.ShapeDtypeStruct((1, LANES), I32),
                   jax.ShapeDtypeStruct((n_rows, W), F32)],
        scratch_shapes=[pltpu.VMEM((2, SUBLANES, tm), I32), pltpu.SMEM((2, SUBLANES, tm), I32),
                        pltpu.VMEM((SUBLANES, LANES), I32), pltpu.SMEM((SUBLANES, LANES), I32),
                        pltpu.VMEM((PAD_CHUNK * SUBLANES, LANES), F32),
                        pltpu.SemaphoreType.DMA, pltpu.SemaphoreType.DMA, pltpu.SemaphoreType.DMA],
        compiler_params=_cparams(("arbitrary",)),
        name="moe_scatter",
    )(route, route, cnt, xp)


def _experts_body(te_ref, nu_ref, xs_ref, wgu_ref, bgu_ref, wd_ref, bd_ref, y_ref, wgu16_ref, wd16_ref,
                  *, F, tg):
    i = pl.program_id(0)
    live = i < nu_ref[0]
    new_expert = (i == 0) | (te_ref[i] != te_ref[jnp.maximum(i - 1, 0)])

    @pl.when(live & new_expert)
    def _():
        wgu16_ref[...] = wgu_ref[...].astype(BF16)
        wd16_ref[...] = wd_ref[...].astype(BF16)

    @pl.when(live)
    def _():
        x = jnp.concatenate([c.astype(BF16) for c in _load_tile_rows(xs_ref, tg)], axis=1)
        h = jnp.dot(x, wgu16_ref[...], preferred_element_type=F32) + bgu_ref[...]
        gate = jnp.minimum(h[:, :F], SWIGLU_LIMIT)
        up = jnp.clip(h[:, F:], -SWIGLU_LIMIT, SWIGLU_LIMIT)
        a = (up + 1.0) * (gate * _sigmoid(SWIGLU_ALPHA * gate))
        y = jnp.dot(a.astype(BF16), wd16_ref[...], preferred_element_type=F32) + bd_ref[...]
        _store_tile_rows(y_ref, y)


def _experts(te, n_used, xs, wgu, bgu, wd, bd, *, tg, n_tiles):
    n_rows, W = xs.shape
    E, D, F2 = wgu.shape
    F = F2 // 2
    row = lambda i, te, nu: (jnp.minimum(i, nu[0] - 1), 0)
    exp3 = lambda i, te, nu: (te[jnp.minimum(i, nu[0] - 1)], 0, 0)
    grid_spec = pltpu.PrefetchScalarGridSpec(
        num_scalar_prefetch=2,
        grid=(n_tiles,),
        in_specs=[pl.BlockSpec((tg * SUBLANES, W), row),
                  pl.BlockSpec((None, D, F2), exp3),
                  pl.BlockSpec((None, 1, F2), exp3),
                  pl.BlockSpec((None, F, D), exp3),
                  pl.BlockSpec((None, 1, D), exp3)],
        out_specs=pl.BlockSpec((tg * SUBLANES, W), row),
        scratch_shapes=[pltpu.VMEM((D, F2), BF16), pltpu.VMEM((F, D), BF16)],
    )
    return pl.pallas_call(
        functools.partial(_experts_body, F=F, tg=tg),
        grid_spec=grid_spec,
        out_shape=jax.ShapeDtypeStruct((n_rows, W), F32),
        compiler_params=_cparams(("arbitrary",)),
        name="moe_experts",
    )(te, n_used, xs, wgu, bgu, wd, bd)


def _combine_body(pos_ref, route_ref, h_ref, y_hbm, o_ref, pos_smem, buf_ref, sem_p, sem_d, *, tm):
    i = pl.program_id(0)
    n = pl.num_programs(0)
    slot = i & 1

    def gather_rows(step, s):
        cp = pltpu.make_async_copy(pos_ref.at[:, pl.ds(pl.multiple_of(step * tm, tm), tm)],
                                   pos_smem.at[s], sem_p)
        cp.start()
        cp.wait()
        return functools.partial(row_copy, s)

    def row_copy(s, t, kk):
        src = pl.multiple_of(pos_smem[s, kk, t], SUBLANES)
        return pltpu.make_async_copy(
            y_hbm.at[pl.ds(src, SUBLANES)],
            buf_ref.at[s, kk, pl.ds(pl.multiple_of(t * SUBLANES, SUBLANES), SUBLANES)], sem_d.at[s])

    @pl.when(i == 0)
    def _():
        _issue_rows(gather_rows(0, 0), tm)

    @pl.when(i + 1 < n)
    def _():
        _issue_rows(gather_rows(i + 1, 1 - slot), tm)

    _drain_rows(functools.partial(row_copy, slot), tm)

    route = route_ref[...]
    lane = lax.broadcasted_iota(I32, (tm, LANES), 1)
    ws = [jnp.sum(jnp.where(lane == TOP_K + kk, route, 0.0), axis=-1, keepdims=True)
          for kk in range(TOP_K)]
    for c in range(SUBLANES):
        sl = slice(c * LANES, (c + 1) * LANES)
        acc = h_ref[:, sl]
        for kk in range(TOP_K):
            acc = acc + ws[kk] * buf_ref[slot, kk, pl.ds(c, tm, stride=SUBLANES), :]
        o_ref[:, sl] = acc


def _combine(pos, route, h1, y, *, tm):
    T, D = h1.shape
    W = y.shape[1]
    return pl.pallas_call(
        functools.partial(_combine_body, tm=tm),
        grid=(T // tm,),
        in_specs=[pl.BlockSpec((SUBLANES, T), lambda i: (0, 0)),
                  pl.BlockSpec((tm, LANES), lambda i: (i, 0)),
                  pl.BlockSpec((tm, D), lambda i: (i, 0)),
                  pl.BlockSpec(memory_space=pl.ANY)],
        out_specs=pl.BlockSpec((tm, D), lambda i: (i, 0)),
        out_shape=jax.ShapeDtypeStruct((T, D), F32),
        scratch_shapes=[pltpu.SMEM((2, SUBLANES, tm), I32),
                        pltpu.VMEM((2, TOP_K, tm * SUBLANES, W), F32),
                        pltpu.SemaphoreType.DMA, pltpu.SemaphoreType.DMA((2,))],
        compiler_params=_cparams(("arbitrary",)),
        name="moe_combine",
    )(pos, route, h1, y)


def _blockdiag_dense(w, width):
    nb, bs, _ = w.shape
    per = width // bs
    wt = w.reshape(nb // per, per, bs, bs)
    eye = jnp.eye(per, dtype=w.dtype)
    dense = jnp.einsum('gpio,pq->gpiqo', wt, eye)
    return dense.reshape(nb // per, width, width)


def _rope_tables(S):
    half = DA_HEAD_DIM // 2
    inv = ROPE_THETA ** (-(jnp.arange(half, dtype=F32) * 2.0 / DA_HEAD_DIM))
    ang = jnp.arange(S, dtype=F32)[:, None] * inv[None, :]
    cos = jnp.tile(jnp.cos(ang), (1, LANES // half))
    sign = jnp.asarray(np.where(_HEAD_LANE_HALF == 0, -1.0, 1.0), F32)
    sin = jnp.tile(jnp.sin(ang), (1, LANES // half)) * sign[None, :]
    return cos, sin


_HEAD_LANE = np.arange(LANES)
_HEAD_LANE_HALF = _HEAD_LANE // (LANES // 2)
_HEAD_LANE_COMP = (_HEAD_LANE // (DA_HEAD_DIM // 2)) % 2
_HEAD_LANE_DIM = _HEAD_LANE_HALF * (DA_HEAD_DIM // 2) + _HEAD_LANE % (DA_HEAD_DIM // 2)
_HEAD_LANE_SRC = _HEAD_LANE_COMP * DA_HEAD_DIM + _HEAD_LANE_DIM


def _tile(n, pref):
    return pref if n % pref == 0 else n


def _layer(h2, mem2, B, S, lambda_init, attn_norm_g, w_in, b_gate, da_q_norm_g, da_k_norm_g,
           da_lambda_q1, da_lambda_k1, da_lambda_q2, da_lambda_k2, da_subln_g, ml_conv_w, ml_conv_b,
           ml_wq, ml_wk, ml_wv, ml_w_if, ml_b_if, ml_out_norm_g, ml_skip, mem_norm_g, ca_w_kv,
           ca_q_norm_g, ca_k_norm_g, w_branch_da, w_branch_ml, w_branch_ca, w_out, ffn_norm_g,
           router_w, router_b, w_gate_up, b_gate_up, w_down, b_down):
    T, D = h2.shape
    M = mem2.shape[0] // B
    QK = DA_HEADS * 2 * DA_HEAD_DIM
    VW = DA_HEADS * DA_V_DIM
    C = ml_conv_w.shape[1]
    CAW = ca_w_kv.shape[1] // 2
    E = router_w.shape[1]

    o = [0, QK, 2 * QK, 2 * QK + VW, 2 * QK + VW + C, 2 * QK + VW + 2 * C, 2 * QK + VW + 2 * C + CAW]
    head_perm = (np.arange(DA_HEADS)[:, None] * LANES + _HEAD_LANE_SRC[None, :]).reshape(-1)
    w_re = jnp.concatenate([w_in[:, o[0]:o[1]][:, head_perm], w_in[:, o[1]:o[2]][:, head_perm],
                            w_in[:, o[2]:o[3]], w_in[:, o[5]:o[6]], w_in[:, o[3]:o[5]], w_in[:, o[6]:]],
                           axis=1).astype(BF16)
    tn = 1024
    col_v, col_caq, col_mlx, col_mlz, col_gate = 2 * QK, 2 * QK + VW, 2 * QK + VW + CAW, \
        2 * QK + VW + CAW + C, 2 * QK + VW + CAW + 2 * C
    proj = _inproj(h2, attn_norm_g[None, :], w_re, tm=_tile(T, 2048), tn=tn)

    g2 = jnp.stack([da_q_norm_g[_HEAD_LANE_DIM], da_k_norm_g[_HEAD_LANE_DIM]])[:, None, :]
    cos_t, sin_t = _rope_tables(S)
    gm = jnp.asarray(_HEAD_LANE_COMP[:, None] == _HEAD_LANE_COMP[None, :], BF16)
    qk = _qkrope(proj, g2, cos_t, sin_t, gm, T=T, S=S, width=QK, tm=_tile(S, 1024))
    y_da = _diff_attention(qk, proj, da_lambda_q1[None, :], da_lambda_k1[None, :], da_lambda_q2[None, :],
                           da_lambda_k2[None, :], da_subln_g[None, :], da_q_norm_g[None, :],
                           da_k_norm_g[None, :], B=B, S=S, tq=_tile(S, 1024),
                           v_col0=col_v // LANES, q_scale=DA_Q_SCALE, lambda_init=lambda_init)

    bd = jnp.stack([_blockdiag_dense(ml_wq, MXU_DIM), _blockdiag_dense(ml_wk, MXU_DIM),
                    _blockdiag_dense(ml_wv, MXU_DIM)]).astype(BF16)
    wif = jnp.pad(ml_w_if.reshape(3, C, 2 * ML_HEADS), ((0, 0), (0, 0), (0, LANES - 2 * ML_HEADS))).astype(BF16)
    bif = jnp.pad(ml_b_if, (0, LANES - 2 * ML_HEADS))[None, :]
    xc, mq, mk, mv, gates = _mlpre(proj, ml_conv_w, ml_conv_b[None, :], bd, wif, bif, B=B, S=S, C=C,
                                   tm=_tile(S, 512), x_col=col_mlx // C)
    dml = C // ML_HEADS
    y_ml = _mlstm(mq, mk, mv, gates, xc, proj, ml_out_norm_g[None, :], ml_skip[None, :], B=B, S=S, C=C,
                  L=_tile(S, 256), z_col0=col_mlz // dml, hpb=4)

    kmem, vmem = _memkv(mem2, mem_norm_g[None, :], ca_w_kv.astype(BF16), ca_k_norm_g[None, :], B=B, M=M, W=CAW)

    tmx = _tile(S, 512)
    tri = (jnp.arange(tmx)[None, :] < jnp.arange(tmx)[:, None]).astype(BF16)
    rw = jnp.pad(router_w, ((0, 0), (0, LANES - E))).astype(BF16)
    rb = jnp.pad(router_b, (0, LANES - E))[None, :]
    h1, xp, route, cnt = _mix(h2, y_da, y_ml, kmem, vmem, ca_q_norm_g[None, :], proj,
                              b_gate.reshape(N_BRANCH, 1, D),
                              w_branch_da.astype(BF16), w_branch_ml.astype(BF16),
                              w_branch_ca.astype(BF16), w_out.astype(BF16), ffn_norm_g[None, :], rw, rb, tri,
                              tm=tmx, S=S, caq_col=col_caq // CAW, g_col=col_gate // D, n_exp=E)

    tg = _tile(T, 512)
    n_tiles = (T * TOP_K) // tg + E
    assert D == SUBLANES * LANES, "MoE rows are moved as one (8,128) f32 tile each"
    pos, te, n_used, xs = _scatter(route, cnt, xp, tm=_tile(T, 512), tg=tg, n_exp=E, n_tiles=n_tiles)
    y = _experts(te[:, 0], n_used[0, :1], xs, w_gate_up, b_gate_up[:, None, :],
                 w_down, b_down[:, None, :], tg=tg, n_tiles=n_tiles)
    return _combine(pos, route, h1, y, tm=_tile(T, 256))


def kernel(x, mem, attn_norm_g, w_in, b_gate, da_q_norm_g, da_k_norm_g, da_lambda_q1, da_lambda_k1, da_lambda_q2, da_lambda_k2, da_subln_g, ml_conv_w, ml_conv_b, ml_wq, ml_wk, ml_wv, ml_w_if, ml_b_if, ml_out_norm_g, ml_skip, mem_norm_g, ca_w_kv, ca_q_norm_g, ca_k_norm_g, w_branch_da, w_branch_ml, w_branch_ca, w_out, ffn_norm_g, router_w, router_b, w_gate_up, b_gate_up, w_down, b_down):
    B, S, D = x.shape
    depth = w_in.shape[0]
    h2 = x.reshape(B * S, D)
    mem2 = mem.reshape(B * mem.shape[1], D)
    params = (attn_norm_g, w_in, b_gate, da_q_norm_g, da_k_norm_g, da_lambda_q1, da_lambda_k1,
              da_lambda_q2, da_lambda_k2, da_subln_g, ml_conv_w, ml_conv_b, ml_wq, ml_wk, ml_wv, ml_w_if,
              ml_b_if, ml_out_norm_g, ml_skip, mem_norm_g, ca_w_kv, ca_q_norm_g, ca_k_norm_g, w_branch_da,
              w_branch_ml, w_branch_ca, w_out, ffn_norm_g, router_w, router_b, w_gate_up, b_gate_up,
              w_down, b_down)
    for l in range(depth):
        lambda_init = 0.8 - 0.6 * math.exp(-0.3 * l)
        h2 = _layer(h2, mem2, B, S, lambda_init, *[p[l] for p in params])
    return h2.reshape(B, S, D)
```

```python
import functools
import math

import jax
import jax.numpy as jnp
import numpy as np
from jax import lax
from jax.experimental import pallas as pl
from jax.experimental.pallas import tpu as pltpu

F32 = jnp.float32
BF16 = jnp.bfloat16
I32 = jnp.int32
U32 = jnp.uint32

NORM_EPS = 1e-6
ROPE_THETA = 10000.0
CHUNK = 64

DA_HEADS = 8
DA_HEAD_DIM = 64
DA_V_DIM = 128
ML_HEADS = 4
ML_CONV = 4
ML_QKV_BLOCK = 4
CA_HEADS = 4
N_BRANCH = 3
TOP_K = 4
SWIGLU_LIMIT = 7.0
SWIGLU_ALPHA = 1.702

LANES = 128
SUBLANES = 8
MXU_DIM = 256
VMEM_LIMIT = 56 * 1024 * 1024
NEG_BIG = -1e30
DA_Q_SCALE = DA_HEAD_DIM ** -0.5 * math.log2(math.e)
DA_BOUND_SLACK = 1.01
DA_BOUND_LIMIT = 40.0


def _cparams(sem):
    return pltpu.CompilerParams(dimension_semantics=sem, vmem_limit_bytes=VMEM_LIMIT)


def _sigmoid(x):
    return 0.5 * jnp.tanh(0.5 * x) + 0.5


def _silu(x):
    return x * _sigmoid(x)


def _inproj_body(x_ref, g_ref, w_ref, o_ref, xn_ref):
    j = pl.program_id(1)

    @pl.when(j == 0)
    def _():
        x = x_ref[...]
        ms = jnp.mean(x * x, axis=-1, keepdims=True)
        xn_ref[...] = (x * lax.rsqrt(ms + NORM_EPS) * g_ref[...]).astype(BF16)

    o_ref[...] = jnp.dot(xn_ref[...], w_ref[...], preferred_element_type=F32).astype(o_ref.dtype)


def _inproj(x2, g, w_bf, *, tm, tn):
    T, D = x2.shape
    N = w_bf.shape[1]
    return pl.pallas_call(
        _inproj_body,
        grid=(T // tm, N // tn),
        in_specs=[
            pl.BlockSpec((tm, D), lambda i, j: (i, 0)),
            pl.BlockSpec((1, D), lambda i, j: (0, 0)),
            pl.BlockSpec((D, tn), lambda i, j: (0, j)),
        ],
        out_specs=pl.BlockSpec((tm, tn), lambda i, j: (i, j)),
        out_shape=jax.ShapeDtypeStruct((T, N), BF16),
        scratch_shapes=[pltpu.VMEM((tm, D), BF16)],
        compiler_params=_cparams(("arbitrary", "arbitrary")),
        name="inproj",
    )(x2, g, w_bf)


def _group_sumsq(xb, gm):
    sq = xb * xb
    hi = sq.astype(BF16)
    lo = (sq - hi.astype(F32)).astype(BF16)
    return (jnp.dot(hi, gm, preferred_element_type=F32)
            + jnp.dot(lo, gm, preferred_element_type=F32))


def _qkrope_body(x_ref, g_ref, cos_ref, sin_ref, gm_ref, o_ref, *, ncb, scale):
    c = pl.program_id(1)
    g = g_ref[...] * jnp.where(c == 0, scale, 1.0).astype(F32)
    cos = cos_ref[...]
    sin = sin_ref[...]
    gm = gm_ref[...]
    for cb in range(ncb):
        sl = slice(cb * LANES, (cb + 1) * LANES)
        xb = x_ref[:, sl].astype(F32)
        ss = _group_sumsq(xb, gm)
        y = xb * lax.rsqrt(ss * (1.0 / DA_HEAD_DIM) + NORM_EPS) * g
        sw = pltpu.roll(y, LANES // 2, 1)
        o_ref[:, sl] = (y * cos + sw * sin).astype(o_ref.dtype)


def _qkrope(proj, g2, cos_t, sin_t, gm, *, T, S, width, tm):
    n_s = S // tm
    return pl.pallas_call(
        functools.partial(_qkrope_body, ncb=width // LANES, scale=DA_Q_SCALE),
        grid=(T // tm, 2),
        in_specs=[
            pl.BlockSpec((tm, width), lambda i, c: (i, c)),
            pl.BlockSpec((None, 1, LANES), lambda i, c: (c, 0, 0)),
            pl.BlockSpec((tm, LANES), lambda i, c: (i % n_s, 0)),
            pl.BlockSpec((tm, LANES), lambda i, c: (i % n_s, 0)),
            pl.BlockSpec((LANES, LANES), lambda i, c: (0, 0)),
        ],
        out_specs=pl.BlockSpec((tm, width), lambda i, c: (i, c)),
        out_shape=jax.ShapeDtypeStruct((T, 2 * width), BF16),
        compiler_params=_cparams(("arbitrary", "arbitrary")),
        name="qkrope",
    )(proj, g2, cos_t, sin_t, gm)


def _da_body(q_ref, k_ref, v_ref, lq1_ref, lk1_ref, lq2_ref, lk2_ref, subg_ref, qg_ref, kg_ref, o_ref,
             m_ref, acc_ref, kt_ref, bound_ref, *, tq, tk, n_kv, q_scale, lambda_init):
    i = pl.program_id(2)
    q = q_ref[...]
    lane = lax.broadcasted_iota(I32, (1, LANES), 1)
    lo = (lane & (DA_HEAD_DIM // 2)) == 0
    zero = jnp.zeros_like(q)
    qc = (jnp.where(lo, q, zero), jnp.where(lo, zero, q))

    @pl.when(i == 0)
    def _():
        def tbody(j, carry):
            start = pl.multiple_of(j * tk, tk)
            kt_ref[:, pl.ds(start, tk)] = k_ref[pl.ds(start, tk), :].astype(F32).T.astype(BF16)
            return carry
        lax.fori_loop(0, n_kv, tbody, 0)
        bound_ref[0] = (DA_HEAD_DIM * q_scale * DA_BOUND_SLACK
                        * jnp.max(jnp.abs(qg_ref[...])) * jnp.max(jnp.abs(kg_ref[...])))

    bound = bound_ref[0]

    ones = jnp.ones((tk, LANES), BF16)

    def kt_tile(j):
        return kt_ref[:, pl.ds(pl.multiple_of(j * tk, tk), tk)]

    def v_aug(j):
        return jnp.concatenate([v_ref[pl.ds(pl.multiple_of(j * tk, tk), tk), :], ones], axis=1)

    def scores(j):
        ktj = kt_tile(j)
        return [jnp.dot(qc[c], ktj, preferred_element_type=F32) for c in range(2)]

    def allowed(r0, rows, c0):
        r = lax.broadcasted_iota(I32, (rows, tk), 0) + r0
        cidx = lax.broadcasted_iota(I32, (rows, tk), 1) + c0
        return (cidx | (CHUNK - 1)) <= (r | (CHUNK - 1))

    def consume_bounded(j, ss):
        vj = v_aug(j)
        ps = [jnp.exp2(ss[c] - bound).astype(BF16) for c in range(2)]
        pvs = [jnp.dot(ps[c], vj, preferred_element_type=F32) for c in range(2)]
        for c in range(2):
            acc_ref[c] = acc_ref[c] + pvs[c]

    def consume_online(j, ss, mask=None):
        vj = v_aug(j)
        ps, alphas = [], []
        for c in range(2):
            s = ss[c] if mask is None else jnp.where(mask, ss[c], NEG_BIG)
            m_old = m_ref[c]
            m_new = jnp.maximum(m_old, jnp.max(s, axis=-1, keepdims=True))
            ps.append(jnp.exp2(s - jnp.concatenate([m_new] * (tk // LANES), axis=1)).astype(BF16))
            alphas.append(jnp.exp2(m_old - m_new))
            m_ref[c] = m_new
        pvs = [jnp.dot(ps[c], vj, preferred_element_type=F32) for c in range(2)]
        for c in range(2):
            acc_ref[c] = jnp.concatenate([alphas[c], alphas[c]], axis=1) * acc_ref[c] + pvs[c]

    def block_bounded(r0, j, on_diagonal, first_touch):
        def accumulate(rows, n, val):
            for c in range(2):
                acc_ref[c, pl.ds(rows, n)] = val[c] if first_touch else acc_ref[c, pl.ds(rows, n)] + val[c]

        ktj = kt_tile(j)
        vj = v_aug(j)
        if not on_diagonal:
            s = [jnp.dot(qc[c][r0:r0 + tk], ktj, preferred_element_type=F32) for c in range(2)]
            p = [jnp.exp2(s[c] - bound).astype(BF16) for c in range(2)]
            accumulate(r0, tk, [jnp.dot(p[c], vj, preferred_element_type=F32) for c in range(2)])
            return
        hq = tk // 2
        mask_top = allowed(0, hq, 0)[:, :hq]
        mask_bot = allowed(hq, hq, 0)
        s_top = [jnp.dot(qc[c][r0:r0 + hq], ktj[:, :hq], preferred_element_type=F32) for c in range(2)]
        s_bot = [jnp.dot(qc[c][r0 + hq:r0 + tk], ktj, preferred_element_type=F32) for c in range(2)]
        p_top = [jnp.where(mask_top, jnp.exp2(s_top[c] - bound), 0.0).astype(BF16) for c in range(2)]
        p_bot = [jnp.where(mask_bot, jnp.exp2(s_bot[c] - bound), 0.0).astype(BF16) for c in range(2)]
        accumulate(r0, hq, [jnp.dot(p_top[c], vj[:hq], preferred_element_type=F32) for c in range(2)])
        accumulate(r0 + hq, hq, [jnp.dot(p_bot[c], vj, preferred_element_type=F32) for c in range(2)])

    def diag_bounded():
        block_bounded(0, 2 * i, True, True)
        block_bounded(tk, 2 * i, False, True)
        block_bounded(tk, 2 * i + 1, True, False)

    def diag_online():
        consume_online(2 * i, scores(2 * i), allowed(0, tq, 0))
        consume_online(2 * i + 1, scores(2 * i + 1), allowed(0, tq, tk))

    def sweep(consume, diag):
        diag()

        def pair(jj, carry):
            sa = scores(2 * jj)
            sb = scores(2 * jj + 1)
            consume(2 * jj, sa)
            consume(2 * jj + 1, sb)
            return carry

        lax.fori_loop(0, i, pair, 0)

    @pl.when(bound < DA_BOUND_LIMIT)
    def _():
        sweep(consume_bounded, diag_bounded)

    @pl.when(jnp.logical_not(bound < DA_BOUND_LIMIT))
    def _():
        m_ref[...] = jnp.full(m_ref.shape, NEG_BIG, F32)
        acc_ref[...] = jnp.zeros(acc_ref.shape, F32)
        sweep(consume_online, diag_online)

    f32 = F32
    lam = (jnp.exp(jnp.sum(lq1_ref[...].astype(f32) * lk1_ref[...].astype(f32), keepdims=True))
           - jnp.exp(jnp.sum(lq2_ref[...].astype(f32) * lk2_ref[...].astype(f32), keepdims=True))
           + lambda_init)
    a1 = acc_ref[0]
    a2 = acc_ref[1]
    o = a1[:, :DA_V_DIM] / a1[:, DA_V_DIM:] - lam * (a2[:, :DA_V_DIM] / a2[:, DA_V_DIM:])
    ms = jnp.mean(o * o, axis=-1, keepdims=True)
    o = o * lax.rsqrt(ms + NORM_EPS) * subg_ref[...]
    o_ref[...] = (o * (1.0 - lambda_init)).astype(o_ref.dtype)


def _diff_attention(qk, proj, lq1, lk1, lq2, lk2, subg, qg, kg, *, B, S, tq, v_col0, q_scale, lambda_init):
    T = B * S
    H = DA_HEADS
    n_q = S // tq
    tk = tq // 2
    assert tk % LANES == 0 and tk % (2 * CHUNK) == 0
    vec = lambda n: pl.BlockSpec((1, n), lambda b, h, i: (0, 0))
    return pl.pallas_call(
        functools.partial(_da_body, tq=tq, tk=tk, n_kv=S // tk, q_scale=q_scale, lambda_init=lambda_init),
        grid=(B, H, n_q),
        in_specs=[
            pl.BlockSpec((tq, LANES), lambda b, h, i: (b * n_q + i, h)),
            pl.BlockSpec((S, LANES), lambda b, h, i: (b, H + h)),
            pl.BlockSpec((S, LANES), lambda b, h, i: (b, v_col0 + h)),
            vec(DA_HEAD_DIM), vec(DA_HEAD_DIM), vec(DA_HEAD_DIM), vec(DA_HEAD_DIM),
            vec(DA_V_DIM), vec(DA_HEAD_DIM), vec(DA_HEAD_DIM),
        ],
        out_specs=pl.BlockSpec((tq, LANES), lambda b, h, i: (b * n_q + i, h)),
        out_shape=jax.ShapeDtypeStruct((T, H * DA_V_DIM), BF16),
        scratch_shapes=[pltpu.VMEM((2, tq, LANES), F32), pltpu.VMEM((2, tq, 2 * DA_V_DIM), F32),
                        pltpu.VMEM((LANES, S), BF16), pltpu.SMEM((1,), F32)],
        compiler_params=_cparams(("arbitrary", "arbitrary", "arbitrary")),
        name="diff_attn",
    )(qk, qk, proj, lq1, lk1, lq2, lk2, subg, qg, kg)


def _mlpre_body(x_ref, cw_ref, cb_ref, bd_ref, wif_ref, bif_ref,
                xc_ref, q_ref, k_ref, v_ref, g_ref, prev_ref, *, tm, ncb, kscale):
    s_idx = pl.program_id(1)

    @pl.when(s_idx == 0)
    def _():
        prev_ref[...] = jnp.zeros(prev_ref.shape, F32)

    row8 = lax.broadcasted_iota(I32, (SUBLANES, MXU_DIM), 0)
    gacc = jnp.zeros((tm, LANES), F32) + bif_ref[...]
    for cb in range(ncb):
        sl = slice(cb * MXU_DIM, (cb + 1) * MXU_DIM)
        xb16 = x_ref[:, sl]
        x = xb16.astype(F32)
        prev = prev_ref[:, sl]
        conv = x * cw_ref[ML_CONV - 1:ML_CONV, sl] + cb_ref[:, sl]
        for d in range(1, ML_CONV):
            xs = pltpu.roll(x, d, 0)
            ps = pltpu.roll(prev, d, 0)
            head = jnp.where(row8 < d, ps, xs[:SUBLANES])
            shifted = jnp.concatenate([head, xs[SUBLANES:]], axis=0)
            conv = conv + shifted * cw_ref[ML_CONV - 1 - d:ML_CONV - d, sl]
        prev_ref[:, sl] = x[tm - SUBLANES:]
        xc = _silu(conv)
        xc16 = xc.astype(BF16)
        q = jnp.dot(xc16, bd_ref[0, cb], preferred_element_type=F32)
        k = jnp.dot(xc16, bd_ref[1, cb], preferred_element_type=F32)
        v = jnp.dot(xb16, bd_ref[2, cb], preferred_element_type=F32)
        q16, k16, v16 = q.astype(BF16), k.astype(BF16), v.astype(BF16)
        gacc = gacc + jnp.dot(q16, wif_ref[0, sl, :], preferred_element_type=F32)
        gacc = gacc + jnp.dot(k16, wif_ref[1, sl, :], preferred_element_type=F32)
        gacc = gacc + jnp.dot(v16, wif_ref[2, sl, :], preferred_element_type=F32)
        xc_ref[:, sl] = xc16
        q_ref[:, sl] = q16
        k_ref[:, sl] = (k * kscale).astype(BF16)
        v_ref[:, sl] = v16
    g_ref[...] = gacc


def _mlpre(proj, conv_w, conv_b, bd, wif, bif, *, B, S, C, tm, x_col):
    T = B * S
    n_s = S // tm
    row = lambda i: pl.BlockSpec((tm, C), lambda b, s: (b * n_s + s, i))
    outs = pl.pallas_call(
        functools.partial(_mlpre_body, tm=tm, ncb=C // MXU_DIM, kscale=(C // ML_HEADS) ** -0.5),
        grid=(B, n_s),
        in_specs=[
            row(x_col),
            pl.BlockSpec((ML_CONV, C), lambda b, s: (0, 0)),
            pl.BlockSpec((1, C), lambda b, s: (0, 0)),
            pl.BlockSpec(bd.shape, lambda b, s: (0, 0, 0, 0)),
            pl.BlockSpec(wif.shape, lambda b, s: (0, 0, 0)),
            pl.BlockSpec((1, LANES), lambda b, s: (0, 0)),
        ],
        out_specs=[row(0), row(0), row(0), row(0),
                   pl.BlockSpec((tm, LANES), lambda b, s: (b * n_s + s, 0))],
        out_shape=[jax.ShapeDtypeStruct((T, C), BF16)] * 4 + [jax.ShapeDtypeStruct((T, LANES), F32)],
        scratch_shapes=[pltpu.VMEM((SUBLANES, C), F32)],
        compiler_params=_cparams(("arbitrary", "arbitrary")),
        name="mlstm_pre",
    )(proj, conv_w, conv_b, bd, wif, bif)
    return outs


def _mlstm_body(q_ref, k_ref, v_ref, g_ref, xc_ref, z_ref, ng_ref, skip_ref, o_ref,
                c_ref, cb_ref, n_ref, m_ref, *, L, d, hpb):
    hg = pl.program_id(1)
    c_idx = pl.program_id(2)

    @pl.when(c_idx == 0)
    def _():
        c_ref[...] = jnp.zeros(c_ref.shape, F32)
        cb_ref[...] = jnp.zeros(cb_ref.shape, BF16)
        n_ref[...] = jnp.zeros(n_ref.shape, F32)
        m_ref[...] = jnp.zeros(m_ref.shape, F32)

    g = g_ref[...]
    gt = g.T
    lane = lax.broadcasted_iota(I32, (1, LANES), 1)
    sub = lax.broadcasted_iota(I32, (LANES, 1), 0)
    t_idx = lax.broadcasted_iota(I32, (L, L), 0)
    s_idx = lax.broadcasted_iota(I32, (L, L), 1)
    tri = s_idx <= t_idx

    for hh in range(hpb):
        h = hg * hpb + hh
        sl = slice(hh * d, (hh + 1) * d)
        i_col = jnp.sum(jnp.where(lane == h, g, 0.0), axis=1, keepdims=True)
        f_col = jnp.sum(jnp.where(lane == ML_HEADS + h, g, 0.0), axis=1, keepdims=True)
        i_row = jnp.sum(jnp.where(sub == h, gt, 0.0), axis=0, keepdims=True)
        f_row = jnp.sum(jnp.where(sub == ML_HEADS + h, gt, 0.0), axis=0, keepdims=True)
        lf_col = jax.nn.log_sigmoid(f_col)
        lf_row = jax.nn.log_sigmoid(f_row)

        b_col = jnp.sum(jnp.where(tri, lf_row, 0.0), axis=1, keepdims=True)
        b_row = jnp.sum(jnp.where(t_idx <= s_idx, lf_col, 0.0), axis=0, keepdims=True)
        b_last = jnp.sum(lf_row, axis=1, keepdims=True)

        m_prev = m_ref[hh]
        dlog = jnp.where(tri, b_col - b_row + i_row, NEG_BIG)
        inter_log = b_col + m_prev
        m_rowv = jnp.maximum(inter_log, jnp.max(dlog, axis=1, keepdims=True))
        dw = jnp.exp(dlog - m_rowv)
        inter_w = jnp.exp(inter_log - m_rowv)

        q = q_ref[:, sl]
        k = k_ref[:, sl]
        v = v_ref[:, sl]
        qk = lax.dot_general(q, k, (((1,), (1,)), ((), ())), preferred_element_type=F32)
        qc = jnp.dot(q, cb_ref[hh], preferred_element_type=F32)
        s = qk * dw

        state_row = b_last - b_row + i_row
        state_col = b_last - b_col + i_col
        m_new = jnp.maximum(b_last + m_prev, jnp.max(state_row, axis=1, keepdims=True))
        decay = jnp.exp(b_last + m_prev - m_new)
        ws_col = jnp.exp(state_col - m_new)
        kwf = k.astype(F32) * ws_col

        num = jnp.dot(s.astype(BF16), v, preferred_element_type=F32) + inter_w * qc
        dc = lax.dot_general(kwf.astype(BF16), v, (((0,), (0,)), ((), ())), preferred_element_type=F32)
        den = (jnp.sum(s, axis=1, keepdims=True)
               + inter_w * jnp.sum(q.astype(F32) * n_ref[hh], axis=1, keepdims=True))
        hv = num / jnp.maximum(jnp.abs(den), jnp.exp(-m_rowv))

        c_new = decay * c_ref[hh] + dc
        c_ref[hh] = c_new
        cb_ref[hh] = c_new.astype(BF16)
        n_ref[hh] = decay * n_ref[hh] + jnp.sum(kwf, axis=0, keepdims=True)
        m_ref[hh] = m_new

        mu = jnp.mean(hv, axis=-1, keepdims=True)
        cen = hv - mu
        var = jnp.mean(cen * cen, axis=-1, keepdims=True)
        hn = cen * lax.rsqrt(var + NORM_EPS) * ng_ref[:, sl]
        y = (hn + skip_ref[:, sl] * xc_ref[:, sl].astype(F32)) * _silu(z_ref[:, sl].astype(F32))
        o_ref[:, sl] = y.astype(o_ref.dtype)


def _mlstm(q, k, v, gates, xc, proj, norm_g, skip, *, B, S, C, L, z_col0, hpb):
    T = B * S
    H = ML_HEADS
    d = C // H
    w = hpb * d
    nc = S // L
    blk = lambda off: pl.BlockSpec((L, w), lambda b, h, c: (b * nc + c, off + h))
    par = pl.BlockSpec((1, w), lambda b, h, c: (0, h))
    return pl.pallas_call(
        functools.partial(_mlstm_body, L=L, d=d, hpb=hpb),
        grid=(B, H // hpb, nc),
        in_specs=[blk(0), blk(0), blk(0),
                  pl.BlockSpec((L, LANES), lambda b, h, c: (b * nc + c, 0)),
                  blk(0), blk(z_col0 // hpb), par, par],
        out_specs=blk(0),
        out_shape=jax.ShapeDtypeStruct((T, C), BF16),
        scratch_shapes=[pltpu.VMEM((hpb, d, d), F32), pltpu.VMEM((hpb, d, d), BF16),
                        pltpu.VMEM((hpb, 1, d), F32), pltpu.VMEM((hpb, 1, 1), F32)],
        compiler_params=_cparams(("arbitrary", "arbitrary", "arbitrary")),
        name="mlstm_scan",
    )(q, k, v, gates, xc, proj, norm_g, skip)


def _memkv_body(mem_ref, mg_ref, w_ref, kg_ref, k_ref, v_ref, *, W, dh):
    x = mem_ref[...]
    ms = jnp.mean(x * x, axis=-1, keepdims=True)
    xn = (x * lax.rsqrt(ms + NORM_EPS) * mg_ref[...]).astype(BF16)
    kv = jnp.dot(xn, w_ref[...], preferred_element_type=F32)
    for hd in range(W // dh):
        sl = slice(hd * dh, (hd + 1) * dh)
        kh = kv[:, sl]
        msk = jnp.mean(kh * kh, axis=-1, keepdims=True)
        k_ref[:, sl] = (kh * lax.rsqrt(msk + NORM_EPS) * kg_ref[...]).astype(BF16)
    v_ref[...] = kv[:, W:].astype(BF16)


def _memkv(mem2, mem_g, w_kv_bf, k_g, *, B, M, W):
    D = mem2.shape[1]
    dh = W // CA_HEADS
    return pl.pallas_call(
        functools.partial(_memkv_body, W=W, dh=dh),
        grid=(B,),
        in_specs=[pl.BlockSpec((M, D), lambda b: (b, 0)),
                  pl.BlockSpec((1, D), lambda b: (0, 0)),
                  pl.BlockSpec((D, 2 * W), lambda b: (0, 0)),
                  pl.BlockSpec((1, dh), lambda b: (0, 0))],
        out_specs=[pl.BlockSpec((M, W), lambda b: (b, 0))] * 2,
        out_shape=[jax.ShapeDtypeStruct((B * M, W), BF16)] * 2,
        compiler_params=_cparams(("arbitrary",)),
        name="mem_kv",
    )(mem2, mem_g, w_kv_bf, k_g)


def _xattn_tile(q_ref, k_ref, v_ref, qg_ref):
    W = q_ref.shape[1]
    dh = W // CA_HEADS
    scale = dh ** -0.5
    outs = []
    for hd in range(CA_HEADS):
        sl = slice(hd * dh, (hd + 1) * dh)
        qh = q_ref[:, sl].astype(F32)
        ms = jnp.mean(qh * qh, axis=-1, keepdims=True)
        qn = (qh * lax.rsqrt(ms + NORM_EPS) * (qg_ref[...] * scale)).astype(BF16)
        s = lax.dot_general(qn, k_ref[:, sl], (((1,), (1,)), ((), ())), preferred_element_type=F32)
        mx = jnp.max(s, axis=-1, keepdims=True)
        p = jnp.exp(s - mx)
        p = p / jnp.sum(p, axis=-1, keepdims=True)
        outs.append(jnp.dot(p.astype(BF16), v_ref[:, sl], preferred_element_type=F32).astype(BF16))
    return jnp.concatenate(outs, axis=1)


def _store_tile_rows(ref, x):
    n = x.shape[0]
    for c in range(SUBLANES):
        ref[pl.ds(c, n, stride=SUBLANES), :] = x[:, c * LANES:(c + 1) * LANES]


def _load_tile_rows(ref, n):
    return [ref[pl.ds(c, n, stride=SUBLANES), :] for c in range(SUBLANES)]


def _mix_body(x_ref, yda_ref, yml_ref, caq_ref, kmem_ref, vmem_ref, caqg_ref, gda_ref, gml_ref, gca_ref,
              bg_ref, wda_ref, wml_ref, wca_ref, wout_ref, fg_ref, rw_ref, rb_ref, tri_ref,
              h_ref, xp_ref, route_ref, cnt_ref, carry_ref, *, n_exp):
    i = pl.program_id(0)

    @pl.when(i == 0)
    def _():
        carry_ref[...] = jnp.zeros(carry_ref.shape, F32)

    def gate(g_ref, b):
        return _sigmoid(g_ref[...].astype(F32) + bg_ref[b])

    yca = _xattn_tile(caq_ref, kmem_ref, vmem_ref, caqg_ref)
    mix = (gate(gda_ref, 0) * jnp.dot(yda_ref[...], wda_ref[...], preferred_element_type=F32)
           + gate(gml_ref, 1) * jnp.dot(yml_ref[...], wml_ref[...], preferred_element_type=F32)
           + gate(gca_ref, 2) * jnp.dot(yca, wca_ref[...], preferred_element_type=F32))
    h1 = x_ref[...] + jnp.dot(mix.astype(BF16), wout_ref[...], preferred_element_type=F32)
    h_ref[...] = h1
    ms = jnp.mean(h1 * h1, axis=-1, keepdims=True)
    xn = h1 * lax.rsqrt(ms + NORM_EPS) * fg_ref[...]
    _store_tile_rows(xp_ref, xn)

    logits = jnp.dot(xn.astype(BF16), rw_ref[...], preferred_element_type=F32) + rb_ref[...]
    tm = logits.shape[0]
    lane = lax.broadcasted_iota(I32, (tm, LANES), 1)
    work = jnp.where(lane < n_exp, logits, NEG_BIG)
    sel = jnp.zeros((tm, LANES), F32)
    vals, idxs = [], []
    for _ in range(TOP_K):
        mx = jnp.max(work, axis=-1, keepdims=True)
        idx = jnp.min(jnp.where(work == mx, lane, LANES), axis=-1, keepdims=True)
        hit = lane == idx
        sel = jnp.where(hit, 1.0, sel)
        work = jnp.where(hit, NEG_BIG, work)
        vals.append(mx)
        idxs.append(idx)
    exps = [jnp.exp(v - vals[0]) for v in vals]
    tot = exps[0] + exps[1] + exps[2] + exps[3]

    cum = jnp.dot(tri_ref[...], sel.astype(BF16), preferred_element_type=F32) + carry_ref[...]
    route = jnp.zeros((tm, LANES), F32)
    for kk in range(TOP_K):
        rank = jnp.sum(jnp.where(lane == idxs[kk], cum, 0.0), axis=-1, keepdims=True)
        route = jnp.where(lane == kk, idxs[kk].astype(F32), route)
        route = jnp.where(lane == TOP_K + kk, exps[kk] / tot, route)
        route = jnp.where(lane == 2 * TOP_K + kk, rank, route)
    route_ref[...] = route
    carry_ref[...] = carry_ref[...] + jnp.sum(sel, axis=0, keepdims=True)
    cnt_ref[...] = carry_ref[...]


def _mix(x2, yda, yml, kmem, vmem, caqg, proj, bg, wda, wml, wca, wout, fg, rw, rb, tri,
         *, tm, S, caq_col, g_col, n_exp):
    T, D = x2.shape
    C = yml.shape[1]
    M = kmem.shape[0] // (T // S)
    W = kmem.shape[1]
    n_s = S // tm
    const = lambda shape: pl.BlockSpec(shape, lambda i: (0,) * len(shape))
    return pl.pallas_call(
        functools.partial(_mix_body, n_exp=n_exp),
        grid=(T // tm,),
        in_specs=[pl.BlockSpec((tm, D), lambda i: (i, 0)),
                  pl.BlockSpec((tm, D), lambda i: (i, 0)),
                  pl.BlockSpec((tm, C), lambda i: (i, 0)),
                  pl.BlockSpec((tm, W), lambda i: (i, caq_col)),
                  pl.BlockSpec((M, W), lambda i: (i // n_s, 0)),
                  pl.BlockSpec((M, W), lambda i: (i // n_s, 0)),
                  const((1, W // CA_HEADS)),
                  pl.BlockSpec((tm, D), lambda i: (i, g_col)),
                  pl.BlockSpec((tm, D), lambda i: (i, g_col + 1)),
                  pl.BlockSpec((tm, D), lambda i: (i, g_col + 2)),
                  const((N_BRANCH, 1, D)),
                  const((D, D)), const((C, D)), const((D, D)), const((D, D)),
                  const((1, D)), const((D, LANES)), const((1, LANES)), const((tm, tm))],
        out_specs=[pl.BlockSpec((tm, D), lambda i: (i, 0)),
                   pl.BlockSpec((tm * SUBLANES, LANES), lambda i: (i, 0)),
                   pl.BlockSpec((tm, LANES), lambda i: (i, 0)),
                   pl.BlockSpec((1, LANES), lambda i: (0, 0))],
        out_shape=[jax.ShapeDtypeStruct((T, D), F32),
                   jax.ShapeDtypeStruct((T * SUBLANES, LANES), F32),
                   jax.ShapeDtypeStruct((T, LANES), F32),
                   jax.ShapeDtypeStruct((1, LANES), F32)],
        scratch_shapes=[pltpu.VMEM((1, LANES), F32)],
        compiler_params=_cparams(("arbitrary",)),
        name="mix_route",
    )(x2, yda, yml, proj, kmem, vmem, caqg, proj, proj, proj, bg, wda, wml, wca, wout, fg, rw, rb, tri)


ROW_DMA_UNROLL = 8
PAD_CHUNK_LOG2 = 5
PAD_CHUNK = 1 << PAD_CHUNK_LOG2


def _issue_rows(row_copy, tm):
    def issue(t, carry):
        for kk in range(TOP_K):
            row_copy(t, kk).start(priority=kk % 2)
        return carry

    lax.fori_loop(0, tm, issue, 0, unroll=ROW_DMA_UNROLL)


def _drain_rows(row_copy, tm):
    def drain(t, carry):
        for kk in range(TOP_K):
            row_copy(t, kk).wait()
        return carry

    lax.fori_loop(0, tm, drain, 0, unroll=ROW_DMA_UNROLL)


def _issue_and_drain_rows(row_copy, tm):
    _issue_rows(row_copy, tm)
    _drain_rows(row_copy, tm)


def _group_starts(cnt, tg, n_exp):
    lane_r = lax.broadcasted_iota(I32, (LANES, LANES), 0)
    lane_c = lax.broadcasted_iota(I32, (LANES, LANES), 1)
    padded = jnp.ceil(cnt * (1.0 / tg)) * tg
    padded_col = jnp.sum(jnp.where(lane_r == lane_c, padded, 0.0), axis=1, keepdims=True)
    start = jnp.sum(jnp.where(lane_r < lane_c, padded_col, 0.0), axis=0, keepdims=True)
    return start, start + padded


def _scatter_body(route_ref, route_next_ref, cnt_ref, xp_ref, pos_ref, te_ref, nu_ref, xs_hbm,
                  posv_ref, pos_smem, padv_ref, pad_smem, zero_ref, sem_p, sem_d, sem_z,
                  *, tm, tg, n_exp, n_tiles):
    i = pl.program_id(0)
    n = pl.num_programs(0)
    slot = i & 1
    lane = lax.broadcasted_iota(I32, (tm, LANES), 1)
    start, end = _group_starts(cnt_ref[...], tg, n_exp)

    def stage_positions(route, s):
        posm = jnp.zeros((tm, LANES), F32)
        for kk in range(TOP_K):
            e = jnp.sum(jnp.where(lane == kk, route, 0.0), axis=-1, keepdims=True)
            rank = jnp.sum(jnp.where(lane == 2 * TOP_K + kk, route, 0.0), axis=-1, keepdims=True)
            st = jnp.sum(jnp.where(lane == e.astype(I32), start, 0.0), axis=-1, keepdims=True)
            posm = jnp.where(lane == kk, (st + rank) * SUBLANES, posm)
        posv_ref[s] = posm.astype(I32).T[:SUBLANES]
        cp = pltpu.make_async_copy(posv_ref.at[s], pos_smem.at[s], sem_p)
        cp.start()
        cp.wait()

    @pl.when(i == 0)
    def _():
        stage_positions(route_ref[...], 0)
        tstart = (lax.broadcasted_iota(I32, (n_tiles, LANES), 0) * tg).astype(F32)
        lane_t = lax.broadcasted_iota(I32, (n_tiles, LANES), 1)
        done = jnp.where((lane_t < n_exp) & (end <= tstart), 1.0, 0.0)
        te = jnp.sum(done, axis=-1, keepdims=True)
        te_ref[...] = jnp.broadcast_to(te, (n_tiles, LANES)).astype(I32)
        n_used = jnp.max(end, axis=-1, keepdims=True) * (1.0 / tg)
        nu_ref[...] = jnp.broadcast_to(n_used, (1, LANES)).astype(I32)

        cnt = cnt_ref[...]
        padv_ref[...] = jnp.zeros(padv_ref.shape, I32)
        padv_ref[0:1, :] = ((start + cnt) * SUBLANES).astype(I32)
        padv_ref[1:2, :] = (end - start - cnt).astype(I32)
        zero_ref[...] = jnp.zeros(zero_ref.shape, F32)
        cpz = pltpu.make_async_copy(padv_ref, pad_smem, sem_p)
        cpz.start()
        cpz.wait()

        def pad_copy(off, rows):
            n = rows * SUBLANES
            return pltpu.make_async_copy(zero_ref.at[pl.ds(0, n)],
                                         xs_hbm.at[pl.ds(pl.multiple_of(off, SUBLANES), n)], sem_z)

        def for_each_pad_copy(fn):
            def per_expert(e, carry):
                base = pad_smem[0, e]
                n_pad = pad_smem[1, e]
                n_chunks = lax.shift_right_logical(n_pad, PAD_CHUNK_LOG2)

                def per_chunk(c, c2):
                    fn(pad_copy(base + c * (PAD_CHUNK * SUBLANES), PAD_CHUNK))
                    return c2
                lax.fori_loop(0, n_chunks, per_chunk, 0)

                def per_row(r, c2):
                    fn(pad_copy(base + (n_chunks * PAD_CHUNK + r) * SUBLANES, 1))
                    return c2
                lax.fori_loop(0, n_pad - n_chunks * PAD_CHUNK, per_row, 0)
                return carry
            lax.fori_loop(0, n_exp, per_expert, 0)

        for_each_pad_copy(lambda cp: cp.start())
        for_each_pad_copy(lambda cp: cp.wait())

    def row_copy(t, kk):
        dst = pl.multiple_of(pos_smem[slot, kk, t], SUBLANES)
        return pltpu.make_async_copy(xp_ref.at[pl.ds(pl.multiple_of(t * SUBLANES, SUBLANES), SUBLANES)],
                                     xs_hbm.at[pl.ds(dst, SUBLANES)], sem_d)

    _issue_rows(row_copy, tm)

    @pl.when(i + 1 < n)
    def _():
        stage_positions(route_next_ref[...], 1 - slot)

    pos_ref[...] = posv_ref[slot]
    _drain_rows(row_copy, tm)


def _scatter(route, cnt, xp, *, tm, tg, n_exp, n_tiles):
    T = route.shape[0]
    W = xp.shape[1]
    n_rows = n_tiles * tg * SUBLANES
    n_steps = T // tm
    return pl.pallas_call(
        functools.partial(_scatter_body, tm=tm, tg=tg, n_exp=n_exp, n_tiles=n_tiles),
        grid=(n_steps,),
        in_specs=[pl.BlockSpec((tm, LANES), lambda i: (i, 0)),
                  pl.BlockSpec((tm, LANES), lambda i: (jnp.minimum(i + 1, n_steps - 1), 0)),
                  pl.BlockSpec((1, LANES), lambda i: (0, 0)),
                  pl.BlockSpec((tm * SUBLANES, W), lambda i: (i, 0))],
        out_specs=[pl.BlockSpec((SUBLANES, tm), lambda i: (0, i)),
                   pl.BlockSpec((n_tiles, LANES), lambda i: (0, 0)),
                   pl.BlockSpec((1, LANES), lambda i: (0, 0)),
                   pl.BlockSpec(memory_space=pl.ANY)],
        out_shape=[jax.ShapeDtypeStruct((SUBLANES, T), I32),
                   jax.ShapeDtypeStruct((n_tiles, LANES), I32),
                   jax.ShapeDtypeStruct((1, LANES), I32),
                   jax.ShapeDtypeStruct((n_rows, W), F32)],
        scratch_shapes=[pltpu.VMEM((2, SUBLANES, tm), I32), pltpu.SMEM((2, SUBLANES, tm), I32),
                        pltpu.VMEM((SUBLANES, LANES), I32), pltpu.SMEM((SUBLANES, LANES), I32),
                        pltpu.VMEM((PAD_CHUNK * SUBLANES, LANES), F32),
                        pltpu.SemaphoreType.DMA, pltpu.SemaphoreType.DMA, pltpu.SemaphoreType.DMA],
        compiler_params=_cparams(("arbitrary",)),
        name="moe_scatter",
    )(route, route, cnt, xp)


def _experts_body(te_ref, nu_ref, xs_ref, wgu_ref, bgu_ref, wd_ref, bd_ref, y_ref, wgu16_ref, wd16_ref,
                  *, F, tg):
    i = pl.program_id(0)
    live = i < nu_ref[0]
    new_expert = (i == 0) | (te_ref[i] != te_ref[jnp.maximum(i - 1, 0)])

    @pl.when(live & new_expert)
    def _():
        wgu16_ref[...] = wgu_ref[...].astype(BF16)
        wd16_ref[...] = wd_ref[...].astype(BF16)

    @pl.when(live)
    def _():
        x = jnp.concatenate([c.astype(BF16) for c in _load_tile_rows(xs_ref, tg)], axis=1)
        h = jnp.dot(x, wgu16_ref[...], preferred_element_type=F32) + bgu_ref[...]
        gate = jnp.minimum(h[:, :F], SWIGLU_LIMIT)
        up = jnp.clip(h[:, F:], -SWIGLU_LIMIT, SWIGLU_LIMIT)
        a = (up + 1.0) * (gate * _sigmoid(SWIGLU_ALPHA * gate))
        y = jnp.dot(a.astype(BF16), wd16_ref[...], preferred_element_type=F32) + bd_ref[...]
        _store_tile_rows(y_ref, y)


def _experts(te, n_used, xs, wgu, bgu, wd, bd, *, tg, n_tiles):
    n_rows, W = xs.shape
    E, D, F2 = wgu.shape
    F = F2 // 2
    row = lambda i, te, nu: (jnp.minimum(i, nu[0] - 1), 0)
    exp3 = lambda i, te, nu: (te[jnp.minimum(i, nu[0] - 1)], 0, 0)
    grid_spec = pltpu.PrefetchScalarGridSpec(
        num_scalar_prefetch=2,
        grid=(n_tiles,),
        in_specs=[pl.BlockSpec((tg * SUBLANES, W), row),
                  pl.BlockSpec((None, D, F2), exp3),
                  pl.BlockSpec((None, 1, F2), exp3),
                  pl.BlockSpec((None, F, D), exp3),
                  pl.BlockSpec((None, 1, D), exp3)],
        out_specs=pl.BlockSpec((tg * SUBLANES, W), row),
        scratch_shapes=[pltpu.VMEM((D, F2), BF16), pltpu.VMEM((F, D), BF16)],
    )
    return pl.pallas_call(
        functools.partial(_experts_body, F=F, tg=tg),
        grid_spec=grid_spec,
        out_shape=jax.ShapeDtypeStruct((n_rows, W), F32),
        compiler_params=_cparams(("arbitrary",)),
        name="moe_experts",
    )(te, n_used, xs, wgu, bgu, wd, bd)


def _combine_body(pos_ref, route_ref, h_ref, y_hbm, o_ref, pos_smem, buf_ref, sem_p, sem_d, *, tm):
    i = pl.program_id(0)
    n = pl.num_programs(0)
    slot = i & 1

    def gather_rows(step, s):
        cp = pltpu.make_async_copy(pos_ref.at[:, pl.ds(pl.multiple_of(step * tm, tm), tm)],
                                   pos_smem.at[s], sem_p)
        cp.start()
        cp.wait()
        return functools.partial(row_copy, s)

    def row_copy(s, t, kk):
        src = pl.multiple_of(pos_smem[s, kk, t], SUBLANES)
        return pltpu.make_async_copy(
            y_hbm.at[pl.ds(src, SUBLANES)],
            buf_ref.at[s, kk, pl.ds(pl.multiple_of(t * SUBLANES, SUBLANES), SUBLANES)], sem_d.at[s])

    @pl.when(i == 0)
    def _():
        _issue_rows(gather_rows(0, 0), tm)

    @pl.when(i + 1 < n)
    def _():
        _issue_rows(gather_rows(i + 1, 1 - slot), tm)

    _drain_rows(functools.partial(row_copy, slot), tm)

    route = route_ref[...]
    lane = lax.broadcasted_iota(I32, (tm, LANES), 1)
    ws = [jnp.sum(jnp.where(lane == TOP_K + kk, route, 0.0), axis=-1, keepdims=True)
          for kk in range(TOP_K)]
    for c in range(SUBLANES):
        sl = slice(c * LANES, (c + 1) * LANES)
        acc = h_ref[:, sl]
        for kk in range(TOP_K):
            acc = acc + ws[kk] * buf_ref[slot, kk, pl.ds(c, tm, stride=SUBLANES), :]
        o_ref[:, sl] = acc


def _combine(pos, route, h1, y, *, tm):
    T, D = h1.shape
    W = y.shape[1]
    return pl.pallas_call(
        functools.partial(_combine_body, tm=tm),
        grid=(T // tm,),
        in_specs=[pl.BlockSpec((SUBLANES, T), lambda i: (0, 0)),
                  pl.BlockSpec((tm, LANES), lambda i: (i, 0)),
                  pl.BlockSpec((tm, D), lambda i: (i, 0)),
                  pl.BlockSpec(memory_space=pl.ANY)],
        out_specs=pl.BlockSpec((tm, D), lambda i: (i, 0)),
        out_shape=jax.ShapeDtypeStruct((T, D), F32),
        scratch_shapes=[pltpu.SMEM((2, SUBLANES, tm), I32),
                        pltpu.VMEM((2, TOP_K, tm * SUBLANES, W), F32),
                        pltpu.SemaphoreType.DMA, pltpu.SemaphoreType.DMA((2,))],
        compiler_params=_cparams(("arbitrary",)),
        name="moe_combine",
    )(pos, route, h1, y)


def _blockdiag_dense(w, width):
    nb, bs, _ = w.shape
    per = width // bs
    wt = w.reshape(nb // per, per, bs, bs)
    eye = jnp.eye(per, dtype=w.dtype)
    dense = jnp.einsum('gpio,pq->gpiqo', wt, eye)
    return dense.reshape(nb // per, width, width)


def _rope_tables(S):
    half = DA_HEAD_DIM // 2
    inv = ROPE_THETA ** (-(jnp.arange(half, dtype=F32) * 2.0 / DA_HEAD_DIM))
    ang = jnp.arange(S, dtype=F32)[:, None] * inv[None, :]
    cos = jnp.tile(jnp.cos(ang), (1, LANES // half))
    sign = jnp.asarray(np.where(_HEAD_LANE_HALF == 0, -1.0, 1.0), F32)
    sin = jnp.tile(jnp.sin(ang), (1, LANES // half)) * sign[None, :]
    return cos, sin


_HEAD_LANE = np.arange(LANES)
_HEAD_LANE_HALF = _HEAD_LANE // (LANES // 2)
_HEAD_LANE_COMP = (_HEAD_LANE // (DA_HEAD_DIM // 2)) % 2
_HEAD_LANE_DIM = _HEAD_LANE_HALF * (DA_HEAD_DIM // 2) + _HEAD_LANE % (DA_HEAD_DIM // 2)
_HEAD_LANE_SRC = _HEAD_LANE_COMP * DA_HEAD_DIM + _HEAD_LANE_DIM


def _tile(n, pref):
    return pref if n % pref == 0 else n


def _layer(h2, mem2, B, S, lambda_init, attn_norm_g, w_in, b_gate, da_q_norm_g, da_k_norm_g,
           da_lambda_q1, da_lambda_k1, da_lambda_q2, da_lambda_k2, da_subln_g, ml_conv_w, ml_conv_b,
           ml_wq, ml_wk, ml_wv, ml_w_if, ml_b_if, ml_out_norm_g, ml_skip, mem_norm_g, ca_w_kv,
           ca_q_norm_g, ca_k_norm_g, w_branch_da, w_branch_ml, w_branch_ca, w_out, ffn_norm_g,
           router_w, router_b, w_gate_up, b_gate_up, w_down, b_down):
    T, D = h2.shape
    M = mem2.shape[0] // B
    QK = DA_HEADS * 2 * DA_HEAD_DIM
    VW = DA_HEADS * DA_V_DIM
    C = ml_conv_w.shape[1]
    CAW = ca_w_kv.shape[1] // 2
    E = router_w.shape[1]

    o = [0, QK, 2 * QK, 2 * QK + VW, 2 * QK + VW + C, 2 * QK + VW + 2 * C, 2 * QK + VW + 2 * C + CAW]
    head_perm = (np.arange(DA_HEADS)[:, None] * LANES + _HEAD_LANE_SRC[None, :]).reshape(-1)
    w_re = jnp.concatenate([w_in[:, o[0]:o[1]][:, head_perm], w_in[:, o[1]:o[2]][:, head_perm],
                            w_in[:, o[2]:o[3]], w_in[:, o[5]:o[6]], w_in[:, o[3]:o[5]], w_in[:, o[6]:]],
                           axis=1).astype(BF16)
    tn = 1024
    col_v, col_caq, col_mlx, col_mlz, col_gate = 2 * QK, 2 * QK + VW, 2 * QK + VW + CAW, \
        2 * QK + VW + CAW + C, 2 * QK + VW + CAW + 2 * C
    proj = _inproj(h2, attn_norm_g[None, :], w_re, tm=_tile(T, 2048), tn=tn)

    g2 = jnp.stack([da_q_norm_g[_HEAD_LANE_DIM], da_k_norm_g[_HEAD_LANE_DIM]])[:, None, :]
    cos_t, sin_t = _rope_tables(S)
    gm = jnp.asarray(_HEAD_LANE_COMP[:, None] == _HEAD_LANE_COMP[None, :], BF16)
    qk = _qkrope(proj, g2, cos_t, sin_t, gm, T=T, S=S, width=QK, tm=_tile(S, 1024))
    y_da = _diff_attention(qk, proj, da_lambda_q1[None, :], da_lambda_k1[None, :], da_lambda_q2[None, :],
                           da_lambda_k2[None, :], da_subln_g[None, :], da_q_norm_g[None, :],
                           da_k_norm_g[None, :], B=B, S=S, tq=_tile(S, 1024),
                           v_col0=col_v // LANES, q_scale=DA_Q_SCALE, lambda_init=lambda_init)

    bd = jnp.stack([_blockdiag_dense(ml_wq, MXU_DIM), _blockdiag_dense(ml_wk, MXU_DIM),
                    _blockdiag_dense(ml_wv, MXU_DIM)]).astype(BF16)
    wif = jnp.pad(ml_w_if.reshape(3, C, 2 * ML_HEADS), ((0, 0), (0, 0), (0, LANES - 2 * ML_HEADS))).astype(BF16)
    bif = jnp.pad(ml_b_if, (0, LANES - 2 * ML_HEADS))[None, :]
    xc, mq, mk, mv, gates = _mlpre(proj, ml_conv_w, ml_conv_b[None, :], bd, wif, bif, B=B, S=S, C=C,
                                   tm=_tile(S, 512), x_col=col_mlx // C)
    dml = C // ML_HEADS
    y_ml = _mlstm(mq, mk, mv, gates, xc, proj, ml_out_norm_g[None, :], ml_skip[None, :], B=B, S=S, C=C,
                  L=_tile(S, 256), z_col0=col_mlz // dml, hpb=4)

    kmem, vmem = _memkv(mem2, mem_norm_g[None, :], ca_w_kv.astype(BF16), ca_k_norm_g[None, :], B=B, M=M, W=CAW)

    tmx = _tile(S, 512)
    tri = (jnp.arange(tmx)[None, :] < jnp.arange(tmx)[:, None]).astype(BF16)
    rw = jnp.pad(router_w, ((0, 0), (0, LANES - E))).astype(BF16)
    rb = jnp.pad(router_b, (0, LANES - E))[None, :]
    h1, xp, route, cnt = _mix(h2, y_da, y_ml, kmem, vmem, ca_q_norm_g[None, :], proj,
                              b_gate.reshape(N_BRANCH, 1, D),
                              w_branch_da.astype(BF16), w_branch_ml.astype(BF16),
                              w_branch_ca.astype(BF16), w_out.astype(BF16), ffn_norm_g[None, :], rw, rb, tri,
                              tm=tmx, S=S, caq_col=col_caq // CAW, g_col=col_gate // D, n_exp=E)

    tg = _tile(T, 512)
    n_tiles = (T * TOP_K) // tg + E
    assert D == SUBLANES * LANES, "MoE rows are moved as one (8,128) f32 tile each"
    pos, te, n_used, xs = _scatter(route, cnt, xp, tm=_tile(T, 512), tg=tg, n_exp=E, n_tiles=n_tiles)
    y = _experts(te[:, 0], n_used[0, :1], xs, w_gate_up, b_gate_up[:, None, :],
                 w_down, b_down[:, None, :], tg=tg, n_tiles=n_tiles)
    return _combine(pos, route, h1, y, tm=_tile(T, 512))


def kernel(x, mem, attn_norm_g, w_in, b_gate, da_q_norm_g, da_k_norm_g, da_lambda_q1, da_lambda_k1, da_lambda_q2, da_lambda_k2, da_subln_g, ml_conv_w, ml_conv_b, ml_wq, ml_wk, ml_wv, ml_w_if, ml_b_if, ml_out_norm_g, ml_skip, mem_norm_g, ca_w_kv, ca_q_norm_g, ca_k_norm_g, w_branch_da, w_branch_ml, w_branch_ca, w_out, ffn_norm_g, router_w, router_b, w_gate_up, b_gate_up, w_down, b_down):
    B, S, D = x.shape
    depth = w_in.shape[0]
    h2 = x.reshape(B * S, D)
    mem2 = mem.reshape(B * mem.shape[1], D)
    params = (attn_norm_g, w_in, b_gate, da_q_norm_g, da_k_norm_g, da_lambda_q1, da_lambda_k1,
              da_lambda_q2, da_lambda_k2, da_subln_g, ml_conv_w, ml_conv_b, ml_wq, ml_wk, ml_wv, ml_w_if,
              ml_b_if, ml_out_norm_g, ml_skip, mem_norm_g, ca_w_kv, ca_q_norm_g, ca_k_norm_g, w_branch_da,
              w_branch_ml, w_branch_ca, w_out, ffn_norm_g, router_w, router_b, w_gate_up, b_gate_up,
              w_down, b_down)
    for l in range(depth):
        lambda_init = 0.8 - 0.6 * math.exp(-0.3 * l)
        h2 = _layer(h2, mem2, B, S, lambda_init, *[p[l] for p in params])
    return h2.reshape(B, S, D)
```

```python
import functools
import math

import jax
import jax.numpy as jnp
import numpy as np
from jax import lax
from jax.experimental import pallas as pl
from jax.experimental.pallas import tpu as pltpu

F32 = jnp.float32
BF16 = jnp.bfloat16
I32 = jnp.int32
U32 = jnp.uint32

NORM_EPS = 1e-6
ROPE_THETA = 10000.0
CHUNK = 64

DA_HEADS = 8
DA_HEAD_DIM = 64
DA_V_DIM = 128
ML_HEADS = 4
ML_CONV = 4
ML_QKV_BLOCK = 4
CA_HEADS = 4
N_BRANCH = 3
TOP_K = 4
SWIGLU_LIMIT = 7.0
SWIGLU_ALPHA = 1.702

LANES = 128
SUBLANES = 8
MXU_DIM = 256
VMEM_LIMIT = 56 * 1024 * 1024
NEG_BIG = -1e30
DA_Q_SCALE = DA_HEAD_DIM ** -0.5 * math.log2(math.e)
DA_BOUND_SLACK = 1.01
DA_BOUND_LIMIT = 40.0


def _cparams(sem):
    return pltpu.CompilerParams(dimension_semantics=sem, vmem_limit_bytes=VMEM_LIMIT)


def _sigmoid(x):
    return 0.5 * jnp.tanh(0.5 * x) + 0.5


def _silu(x):
    return x * _sigmoid(x)


def _inproj_body(x_ref, g_ref, w_ref, o_ref, xn_ref):
    j = pl.program_id(1)

    @pl.when(j == 0)
    def _():
        x = x_ref[...]
        ms = jnp.mean(x * x, axis=-1, keepdims=True)
        xn_ref[...] = (x * lax.rsqrt(ms + NORM_EPS) * g_ref[...]).astype(BF16)

    o_ref[...] = jnp.dot(xn_ref[...], w_ref[...], preferred_element_type=F32).astype(o_ref.dtype)


def _inproj(x2, g, w_bf, *, tm, tn):
    T, D = x2.shape
    N = w_bf.shape[1]
    return pl.pallas_call(
        _inproj_body,
        grid=(T // tm, N // tn),
        in_specs=[
            pl.BlockSpec((tm, D), lambda i, j: (i, 0)),
            pl.BlockSpec((1, D), lambda i, j: (0, 0)),
            pl.BlockSpec((D, tn), lambda i, j: (0, j)),
        ],
        out_specs=pl.BlockSpec((tm, tn), lambda i, j: (i, j)),
        out_shape=jax.ShapeDtypeStruct((T, N), BF16),
        scratch_shapes=[pltpu.VMEM((tm, D), BF16)],
        compiler_params=_cparams(("arbitrary", "arbitrary")),
        name="inproj",
    )(x2, g, w_bf)


def _group_sumsq(xb, gm):
    sq = xb * xb
    hi = sq.astype(BF16)
    lo = (sq - hi.astype(F32)).astype(BF16)
    return (jnp.dot(hi, gm, preferred_element_type=F32)
            + jnp.dot(lo, gm, preferred_element_type=F32))


def _qkrope_body(x_ref, g_ref, cos_ref, sin_ref, gm_ref, o_ref, *, ncb, scale):
    c = pl.program_id(1)
    g = g_ref[...] * jnp.where(c == 0, scale, 1.0).astype(F32)
    cos = cos_ref[...]
    sin = sin_ref[...]
    gm = gm_ref[...]
    for cb in range(ncb):
        sl = slice(cb * LANES, (cb + 1) * LANES)
        xb = x_ref[:, sl].astype(F32)
        ss = _group_sumsq(xb, gm)
        y = xb * lax.rsqrt(ss * (1.0 / DA_HEAD_DIM) + NORM_EPS) * g
        sw = pltpu.roll(y, LANES // 2, 1)
        o_ref[:, sl] = (y * cos + sw * sin).astype(o_ref.dtype)


def _qkrope(proj, g2, cos_t, sin_t, gm, *, T, S, width, tm):
    n_s = S // tm
    return pl.pallas_call(
        functools.partial(_qkrope_body, ncb=width // LANES, scale=DA_Q_SCALE),
        grid=(T // tm, 2),
        in_specs=[
            pl.BlockSpec((tm, width), lambda i, c: (i, c)),
            pl.BlockSpec((None, 1, LANES), lambda i, c: (c, 0, 0)),
            pl.BlockSpec((tm, LANES), lambda i, c: (i % n_s, 0)),
            pl.BlockSpec((tm, LANES), lambda i, c: (i % n_s, 0)),
            pl.BlockSpec((LANES, LANES), lambda i, c: (0, 0)),
        ],
        out_specs=pl.BlockSpec((tm, width), lambda i, c: (i, c)),
        out_shape=jax.ShapeDtypeStruct((T, 2 * width), BF16),
        compiler_params=_cparams(("arbitrary", "arbitrary")),
        name="qkrope",
    )(proj, g2, cos_t, sin_t, gm)


def _da_body(q_ref, k_ref, v_ref, lq1_ref, lk1_ref, lq2_ref, lk2_ref, subg_ref, qg_ref, kg_ref, o_ref,
             m_ref, acc_ref, kt_ref, bound_ref, *, tq, tk, n_kv, q_scale, lambda_init):
    i = pl.program_id(2)
    q = q_ref[...]
    lane = lax.broadcasted_iota(I32, (1, LANES), 1)
    lo = (lane & (DA_HEAD_DIM // 2)) == 0
    zero = jnp.zeros_like(q)
    qc = (jnp.where(lo, q, zero), jnp.where(lo, zero, q))

    @pl.when(i == 0)
    def _():
        def tbody(j, carry):
            start = pl.multiple_of(j * tk, tk)
            kt_ref[:, pl.ds(start, tk)] = k_ref[pl.ds(start, tk), :].astype(F32).T.astype(BF16)
            return carry
        lax.fori_loop(0, n_kv, tbody, 0)
        bound_ref[0] = (DA_HEAD_DIM * q_scale * DA_BOUND_SLACK
                        * jnp.max(jnp.abs(qg_ref[...])) * jnp.max(jnp.abs(kg_ref[...])))

    bound = bound_ref[0]

    ones = jnp.ones((tk, LANES), BF16)

    def kt_tile(j):
        return kt_ref[:, pl.ds(pl.multiple_of(j * tk, tk), tk)]

    def v_aug(j):
        return jnp.concatenate([v_ref[pl.ds(pl.multiple_of(j * tk, tk), tk), :], ones], axis=1)

    def scores(j):
        ktj = kt_tile(j)
        return [jnp.dot(qc[c], ktj, preferred_element_type=F32) for c in range(2)]

    def allowed(r0, rows, c0):
        r = lax.broadcasted_iota(I32, (rows, tk), 0) + r0
        cidx = lax.broadcasted_iota(I32, (rows, tk), 1) + c0
        return (cidx | (CHUNK - 1)) <= (r | (CHUNK - 1))

    def consume_bounded(j, ss):
        vj = v_aug(j)
        ps = [jnp.exp2(ss[c] - bound).astype(BF16) for c in range(2)]
        pvs = [jnp.dot(ps[c], vj, preferred_element_type=F32) for c in range(2)]
        for c in range(2):
            acc_ref[c] = acc_ref[c] + pvs[c]

    def consume_online(j, ss, mask=None):
        vj = v_aug(j)
        ps, alphas = [], []
        for c in range(2):
            s = ss[c] if mask is None else jnp.where(mask, ss[c], NEG_BIG)
            m_old = m_ref[c]
            m_new = jnp.maximum(m_old, jnp.max(s, axis=-1, keepdims=True))
            ps.append(jnp.exp2(s - jnp.concatenate([m_new] * (tk // LANES), axis=1)).astype(BF16))
            alphas.append(jnp.exp2(m_old - m_new))
            m_ref[c] = m_new
        pvs = [jnp.dot(ps[c], vj, preferred_element_type=F32) for c in range(2)]
        for c in range(2):
            acc_ref[c] = jnp.concatenate([alphas[c], alphas[c]], axis=1) * acc_ref[c] + pvs[c]

    def block_bounded(r0, j, on_diagonal, first_touch):
        def accumulate(rows, n, val):
            for c in range(2):
                acc_ref[c, pl.ds(rows, n)] = val[c] if first_touch else acc_ref[c, pl.ds(rows, n)] + val[c]

        ktj = kt_tile(j)
        vj = v_aug(j)
        if not on_diagonal:
            s = [jnp.dot(qc[c][r0:r0 + tk], ktj, preferred_element_type=F32) for c in range(2)]
            p = [jnp.exp2(s[c] - bound).astype(BF16) for c in range(2)]
            accumulate(r0, tk, [jnp.dot(p[c], vj, preferred_element_type=F32) for c in range(2)])
            return
        hq = tk // 2
        mask_top = allowed(0, hq, 0)[:, :hq]
        mask_bot = allowed(hq, hq, 0)
        s_top = [jnp.dot(qc[c][r0:r0 + hq], ktj[:, :hq], preferred_element_type=F32) for c in range(2)]
        s_bot = [jnp.dot(qc[c][r0 + hq:r0 + tk], ktj, preferred_element_type=F32) for c in range(2)]
        p_top = [jnp.where(mask_top, jnp.exp2(s_top[c] - bound), 0.0).astype(BF16) for c in range(2)]
        p_bot = [jnp.where(mask_bot, jnp.exp2(s_bot[c] - bound), 0.0).astype(BF16) for c in range(2)]
        accumulate(r0, hq, [jnp.dot(p_top[c], vj[:hq], preferred_element_type=F32) for c in range(2)])
        accumulate(r0 + hq, hq, [jnp.dot(p_bot[c], vj, preferred_element_type=F32) for c in range(2)])

    def diag_bounded():
        block_bounded(0, 2 * i, True, True)
        block_bounded(tk, 2 * i, False, True)
        block_bounded(tk, 2 * i + 1, True, False)

    def diag_online():
        consume_online(2 * i, scores(2 * i), allowed(0, tq, 0))
        consume_online(2 * i + 1, scores(2 * i + 1), allowed(0, tq, tk))

    def sweep(consume, diag):
        diag()

        def pair(jj, carry):
            sa = scores(2 * jj)
            sb = scores(2 * jj + 1)
            consume(2 * jj, sa)
            consume(2 * jj + 1, sb)
            return carry

        lax.fori_loop(0, i, pair, 0)

    @pl.when(bound < DA_BOUND_LIMIT)
    def _():
        sweep(consume_bounded, diag_bounded)

    @pl.when(jnp.logical_not(bound < DA_BOUND_LIMIT))
    def _():
        m_ref[...] = jnp.full(m_ref.shape, NEG_BIG, F32)
        acc_ref[...] = jnp.zeros(acc_ref.shape, F32)
        sweep(consume_online, diag_online)

    f32 = F32
    lam = (jnp.exp(jnp.sum(lq1_ref[...].astype(f32) * lk1_ref[...].astype(f32), keepdims=True))
           - jnp.exp(jnp.sum(lq2_ref[...].astype(f32) * lk2_ref[...].astype(f32), keepdims=True))
           + lambda_init)
    a1 = acc_ref[0]
    a2 = acc_ref[1]
    o = a1[:, :DA_V_DIM] / a1[:, DA_V_DIM:] - lam * (a2[:, :DA_V_DIM] / a2[:, DA_V_DIM:])
    ms = jnp.mean(o * o, axis=-1, keepdims=True)
    o = o * lax.rsqrt(ms + NORM_EPS) * subg_ref[...]
    o_ref[...] = (o * (1.0 - lambda_init)).astype(o_ref.dtype)


def _diff_attention(qk, proj, lq1, lk1, lq2, lk2, subg, qg, kg, *, B, S, tq, v_col0, q_scale, lambda_init):
    T = B * S
    H = DA_HEADS
    n_q = S // tq
    tk = tq // 2
    assert tk % LANES == 0 and tk % (2 * CHUNK) == 0
    vec = lambda n: pl.BlockSpec((1, n), lambda b, h, i: (0, 0))
    return pl.pallas_call(
        functools.partial(_da_body, tq=tq, tk=tk, n_kv=S // tk, q_scale=q_scale, lambda_init=lambda_init),
        grid=(B, H, n_q),
        in_specs=[
            pl.BlockSpec((tq, LANES), lambda b, h, i: (b * n_q + i, h)),
            pl.BlockSpec((S, LANES), lambda b, h, i: (b, H + h)),
            pl.BlockSpec((S, LANES), lambda b, h, i: (b, v_col0 + h)),
            vec(DA_HEAD_DIM), vec(DA_HEAD_DIM), vec(DA_HEAD_DIM), vec(DA_HEAD_DIM),
            vec(DA_V_DIM), vec(DA_HEAD_DIM), vec(DA_HEAD_DIM),
        ],
        out_specs=pl.BlockSpec((tq, LANES), lambda b, h, i: (b * n_q + i, h)),
        out_shape=jax.ShapeDtypeStruct((T, H * DA_V_DIM), BF16),
        scratch_shapes=[pltpu.VMEM((2, tq, LANES), F32), pltpu.VMEM((2, tq, 2 * DA_V_DIM), F32),
                        pltpu.VMEM((LANES, S), BF16), pltpu.SMEM((1,), F32)],
        compiler_params=_cparams(("arbitrary", "arbitrary", "arbitrary")),
        name="diff_attn",
    )(qk, qk, proj, lq1, lk1, lq2, lk2, subg, qg, kg)


def _mlpre_body(x_ref, cw_ref, cb_ref, bd_ref, wif_ref, bif_ref,
                xc_ref, q_ref, k_ref, v_ref, g_ref, prev_ref, *, tm, ncb, kscale):
    s_idx = pl.program_id(1)

    @pl.when(s_idx == 0)
    def _():
        prev_ref[...] = jnp.zeros(prev_ref.shape, F32)

    row8 = lax.broadcasted_iota(I32, (SUBLANES, MXU_DIM), 0)
    gacc = jnp.zeros((tm, LANES), F32) + bif_ref[...]
    for cb in range(ncb):
        sl = slice(cb * MXU_DIM, (cb + 1) * MXU_DIM)
        xb16 = x_ref[:, sl]
        x = xb16.astype(F32)
        prev = prev_ref[:, sl]
        conv = x * cw_ref[ML_CONV - 1:ML_CONV, sl] + cb_ref[:, sl]
        for d in range(1, ML_CONV):
            xs = pltpu.roll(x, d, 0)
            ps = pltpu.roll(prev, d, 0)
            head = jnp.where(row8 < d, ps, xs[:SUBLANES])
            shifted = jnp.concatenate([head, xs[SUBLANES:]], axis=0)
            conv = conv + shifted * cw_ref[ML_CONV - 1 - d:ML_CONV - d, sl]
        prev_ref[:, sl] = x[tm - SUBLANES:]
        xc = _silu(conv)
        xc16 = xc.astype(BF16)
        q = jnp.dot(xc16, bd_ref[0, cb], preferred_element_type=F32)
        k = jnp.dot(xc16, bd_ref[1, cb], preferred_element_type=F32)
        v = jnp.dot(xb16, bd_ref[2, cb], preferred_element_type=F32)
        q16, k16, v16 = q.astype(BF16), k.astype(BF16), v.astype(BF16)
        gacc = gacc + jnp.dot(q16, wif_ref[0, sl, :], preferred_element_type=F32)
        gacc = gacc + jnp.dot(k16, wif_ref[1, sl, :], preferred_element_type=F32)
        gacc = gacc + jnp.dot(v16, wif_ref[2, sl, :], preferred_element_type=F32)
        xc_ref[:, sl] = xc16
        q_ref[:, sl] = q16
        k_ref[:, sl] = (k * kscale).astype(BF16)
        v_ref[:, sl] = v16
    g_ref[...] = gacc


def _mlpre(proj, conv_w, conv_b, bd, wif, bif, *, B, S, C, tm, x_col):
    T = B * S
    n_s = S // tm
    row = lambda i: pl.BlockSpec((tm, C), lambda b, s: (b * n_s + s, i))
    outs = pl.pallas_call(
        functools.partial(_mlpre_body, tm=tm, ncb=C // MXU_DIM, kscale=(C // ML_HEADS) ** -0.5),
        grid=(B, n_s),
        in_specs=[
            row(x_col),
            pl.BlockSpec((ML_CONV, C), lambda b, s: (0, 0)),
            pl.BlockSpec((1, C), lambda b, s: (0, 0)),
            pl.BlockSpec(bd.shape, lambda b, s: (0, 0, 0, 0)),
            pl.BlockSpec(wif.shape, lambda b, s: (0, 0, 0)),
            pl.BlockSpec((1, LANES), lambda b, s: (0, 0)),
        ],
        out_specs=[row(0), row(0), row(0), row(0),
                   pl.BlockSpec((tm, LANES), lambda b, s: (b * n_s + s, 0))],
        out_shape=[jax.ShapeDtypeStruct((T, C), BF16)] * 4 + [jax.ShapeDtypeStruct((T, LANES), F32)],
        scratch_shapes=[pltpu.VMEM((SUBLANES, C), F32)],
        compiler_params=_cparams(("arbitrary", "arbitrary")),
        name="mlstm_pre",
    )(proj, conv_w, conv_b, bd, wif, bif)
    return outs


def _mlstm_body(q_ref, k_ref, v_ref, g_ref, xc_ref, z_ref, ng_ref, skip_ref, o_ref,
                c_ref, cb_ref, n_ref, m_ref, *, L, d, hpb):
    hg = pl.program_id(1)
    c_idx = pl.program_id(2)

    @pl.when(c_idx == 0)
    def _():
        c_ref[...] = jnp.zeros(c_ref.shape, F32)
        cb_ref[...] = jnp.zeros(cb_ref.shape, BF16)
        n_ref[...] = jnp.zeros(n_ref.shape, F32)
        m_ref[...] = jnp.zeros(m_ref.shape, F32)

    g = g_ref[...]
    gt = g.T
    lane = lax.broadcasted_iota(I32, (1, LANES), 1)
    sub = lax.broadcasted_iota(I32, (LANES, 1), 0)
    t_idx = lax.broadcasted_iota(I32, (L, L), 0)
    s_idx = lax.broadcasted_iota(I32, (L, L), 1)
    tri = s_idx <= t_idx

    for hh in range(hpb):
        h = hg * hpb + hh
        sl = slice(hh * d, (hh + 1) * d)
        i_col = jnp.sum(jnp.where(lane == h, g, 0.0), axis=1, keepdims=True)
        f_col = jnp.sum(jnp.where(lane == ML_HEADS + h, g, 0.0), axis=1, keepdims=True)
        i_row = jnp.sum(jnp.where(sub == h, gt, 0.0), axis=0, keepdims=True)
        f_row = jnp.sum(jnp.where(sub == ML_HEADS + h, gt, 0.0), axis=0, keepdims=True)
        lf_col = jax.nn.log_sigmoid(f_col)
        lf_row = jax.nn.log_sigmoid(f_row)

        b_col = jnp.sum(jnp.where(tri, lf_row, 0.0), axis=1, keepdims=True)
        b_row = jnp.sum(jnp.where(t_idx <= s_idx, lf_col, 0.0), axis=0, keepdims=True)
        b_last = jnp.sum(lf_row, axis=1, keepdims=True)

        m_prev = m_ref[hh]
        dlog = jnp.where(tri, b_col - b_row + i_row, NEG_BIG)
        inter_log = b_col + m_prev
        m_rowv = jnp.maximum(inter_log, jnp.max(dlog, axis=1, keepdims=True))
        dw = jnp.exp(dlog - m_rowv)
        inter_w = jnp.exp(inter_log - m_rowv)

        q = q_ref[:, sl]
        k = k_ref[:, sl]
        v = v_ref[:, sl]
        qk = lax.dot_general(q, k, (((1,), (1,)), ((), ())), preferred_element_type=F32)
        qc = jnp.dot(q, cb_ref[hh], preferred_element_type=F32)
        s = qk * dw

        state_row = b_last - b_row + i_row
        state_col = b_last - b_col + i_col
        m_new = jnp.maximum(b_last + m_prev, jnp.max(state_row, axis=1, keepdims=True))
        decay = jnp.exp(b_last + m_prev - m_new)
        ws_col = jnp.exp(state_col - m_new)
        kwf = k.astype(F32) * ws_col

        num = jnp.dot(s.astype(BF16), v, preferred_element_type=F32) + inter_w * qc
        dc = lax.dot_general(kwf.astype(BF16), v, (((0,), (0,)), ((), ())), preferred_element_type=F32)
        den = (jnp.sum(s, axis=1, keepdims=True)
               + inter_w * jnp.sum(q.astype(F32) * n_ref[hh], axis=1, keepdims=True))
        hv = num / jnp.maximum(jnp.abs(den), jnp.exp(-m_rowv))

        c_new = decay * c_ref[hh] + dc
        c_ref[hh] = c_new
        cb_ref[hh] = c_new.astype(BF16)
        n_ref[hh] = decay * n_ref[hh] + jnp.sum(kwf, axis=0, keepdims=True)
        m_ref[hh] = m_new

        mu = jnp.mean(hv, axis=-1, keepdims=True)
        cen = hv - mu
        var = jnp.mean(cen * cen, axis=-1, keepdims=True)
        hn = cen * lax.rsqrt(var + NORM_EPS) * ng_ref[:, sl]
        y = (hn + skip_ref[:, sl] * xc_ref[:, sl].astype(F32)) * _silu(z_ref[:, sl].astype(F32))
        o_ref[:, sl] = y.astype(o_ref.dtype)


def _mlstm(q, k, v, gates, xc, proj, norm_g, skip, *, B, S, C, L, z_col0, hpb):
    T = B * S
    H = ML_HEADS
    d = C // H
    w = hpb * d
    nc = S // L
    blk = lambda off: pl.BlockSpec((L, w), lambda b, h, c: (b * nc + c, off + h))
    par = pl.BlockSpec((1, w), lambda b, h, c: (0, h))
    return pl.pallas_call(
        functools.partial(_mlstm_body, L=L, d=d, hpb=hpb),
        grid=(B, H // hpb, nc),
        in_specs=[blk(0), blk(0), blk(0),
                  pl.BlockSpec((L, LANES), lambda b, h, c: (b * nc + c, 0)),
                  blk(0), blk(z_col0 // hpb), par, par],
        out_specs=blk(0),
        out_shape=jax.ShapeDtypeStruct((T, C), BF16),
        scratch_shapes=[pltpu.VMEM((hpb, d, d), F32), pltpu.VMEM((hpb, d, d), BF16),
                        pltpu.VMEM((hpb, 1, d), F32), pltpu.VMEM((hpb, 1, 1), F32)],
        compiler_params=_cparams(("arbitrary", "arbitrary", "arbitrary")),
        name="mlstm_scan",
    )(q, k, v, gates, xc, proj, norm_g, skip)


def _memkv_body(mem_ref, mg_ref, w_ref, kg_ref, k_ref, v_ref, *, W, dh):
    x = mem_ref[...]
    ms = jnp.mean(x * x, axis=-1, keepdims=True)
    xn = (x * lax.rsqrt(ms + NORM_EPS) * mg_ref[...]).astype(BF16)
    kv = jnp.dot(xn, w_ref[...], preferred_element_type=F32)
    for hd in range(W // dh):
        sl = slice(hd * dh, (hd + 1) * dh)
        kh = kv[:, sl]
        msk = jnp.mean(kh * kh, axis=-1, keepdims=True)
        k_ref[:, sl] = (kh * lax.rsqrt(msk + NORM_EPS) * kg_ref[...]).astype(BF16)
    v_ref[...] = kv[:, W:].astype(BF16)


def _memkv(mem2, mem_g, w_kv_bf, k_g, *, B, M, W):
    D = mem2.shape[1]
    dh = W // CA_HEADS
    return pl.pallas_call(
        functools.partial(_memkv_body, W=W, dh=dh),
        grid=(B,),
        in_specs=[pl.BlockSpec((M, D), lambda b: (b, 0)),
                  pl.BlockSpec((1, D), lambda b: (0, 0)),
                  pl.BlockSpec((D, 2 * W), lambda b: (0, 0)),
                  pl.BlockSpec((1, dh), lambda b: (0, 0))],
        out_specs=[pl.BlockSpec((M, W), lambda b: (b, 0))] * 2,
        out_shape=[jax.ShapeDtypeStruct((B * M, W), BF16)] * 2,
        compiler_params=_cparams(("arbitrary",)),
        name="mem_kv",
    )(mem2, mem_g, w_kv_bf, k_g)


def _xattn_tile(q_ref, k_ref, v_ref, qg_ref):
    W = q_ref.shape[1]
    dh = W // CA_HEADS
    scale = dh ** -0.5
    outs = []
    for hd in range(CA_HEADS):
        sl = slice(hd * dh, (hd + 1) * dh)
        qh = q_ref[:, sl].astype(F32)
        ms = jnp.mean(qh * qh, axis=-1, keepdims=True)
        qn = (qh * lax.rsqrt(ms + NORM_EPS) * (qg_ref[...] * scale)).astype(BF16)
        s = lax.dot_general(qn, k_ref[:, sl], (((1,), (1,)), ((), ())), preferred_element_type=F32)
        mx = jnp.max(s, axis=-1, keepdims=True)
        p = jnp.exp(s - mx)
        p = p / jnp.sum(p, axis=-1, keepdims=True)
        outs.append(jnp.dot(p.astype(BF16), v_ref[:, sl], preferred_element_type=F32).astype(BF16))
    return jnp.concatenate(outs, axis=1)


def _store_tile_rows(ref, x):
    n = x.shape[0]
    for c in range(SUBLANES):
        ref[pl.ds(c, n, stride=SUBLANES), :] = x[:, c * LANES:(c + 1) * LANES]


def _load_tile_rows(ref, n):
    return [ref[pl.ds(c, n, stride=SUBLANES), :] for c in range(SUBLANES)]


def _mix_body(x_ref, yda_ref, yml_ref, caq_ref, kmem_ref, vmem_ref, caqg_ref, gda_ref, gml_ref, gca_ref,
              bg_ref, wda_ref, wml_ref, wca_ref, wout_ref, fg_ref, rw_ref, rb_ref, tri_ref,
              h_ref, xp_ref, route_ref, cnt_ref, carry_ref, *, n_exp):
    i = pl.program_id(0)

    @pl.when(i == 0)
    def _():
        carry_ref[...] = jnp.zeros(carry_ref.shape, F32)

    def gate(g_ref, b):
        return _sigmoid(g_ref[...].astype(F32) + bg_ref[b])

    yca = _xattn_tile(caq_ref, kmem_ref, vmem_ref, caqg_ref)
    mix = (gate(gda_ref, 0) * jnp.dot(yda_ref[...], wda_ref[...], preferred_element_type=F32)
           + gate(gml_ref, 1) * jnp.dot(yml_ref[...], wml_ref[...], preferred_element_type=F32)
           + gate(gca_ref, 2) * jnp.dot(yca, wca_ref[...], preferred_element_type=F32))
    h1 = x_ref[...] + jnp.dot(mix.astype(BF16), wout_ref[...], preferred_element_type=F32)
    h_ref[...] = h1
    ms = jnp.mean(h1 * h1, axis=-1, keepdims=True)
    xn = h1 * lax.rsqrt(ms + NORM_EPS) * fg_ref[...]
    _store_tile_rows(xp_ref, xn)

    logits = jnp.dot(xn.astype(BF16), rw_ref[...], preferred_element_type=F32) + rb_ref[...]
    tm = logits.shape[0]
    lane = lax.broadcasted_iota(I32, (tm, LANES), 1)
    work = jnp.where(lane < n_exp, logits, NEG_BIG)
    sel = jnp.zeros((tm, LANES), F32)
    vals, idxs = [], []
    for _ in range(TOP_K):
        mx = jnp.max(work, axis=-1, keepdims=True)
        idx = jnp.min(jnp.where(work == mx, lane, LANES), axis=-1, keepdims=True)
        hit = lane == idx
        sel = jnp.where(hit, 1.0, sel)
        work = jnp.where(hit, NEG_BIG, work)
        vals.append(mx)
        idxs.append(idx)
    exps = [jnp.exp(v - vals[0]) for v in vals]
    tot = exps[0] + exps[1] + exps[2] + exps[3]

    cum = jnp.dot(tri_ref[...], sel.astype(BF16), preferred_element_type=F32) + carry_ref[...]
    route = jnp.zeros((tm, LANES), F32)
    for kk in range(TOP_K):
        rank = jnp.sum(jnp.where(lane == idxs[kk], cum, 0.0), axis=-1, keepdims=True)
        route = jnp.where(lane == kk, idxs[kk].astype(F32), route)
        route = jnp.where(lane == TOP_K + kk, exps[kk] / tot, route)
        route = jnp.where(lane == 2 * TOP_K + kk, rank, route)
    route_ref[...] = route
    carry_ref[...] = carry_ref[...] + jnp.sum(sel, axis=0, keepdims=True)
    cnt_ref[...] = carry_ref[...]


def _mix(x2, yda, yml, kmem, vmem, caqg, proj, bg, wda, wml, wca, wout, fg, rw, rb, tri,
         *, tm, S, caq_col, g_col, n_exp):
    T, D = x2.shape
    C = yml.shape[1]
    M = kmem.shape[0] // (T // S)
    W = kmem.shape[1]
    n_s = S // tm
    const = lambda shape: pl.BlockSpec(shape, lambda i: (0,) * len(shape))
    return pl.pallas_call(
        functools.partial(_mix_body, n_exp=n_exp),
        grid=(T // tm,),
        in_specs=[pl.BlockSpec((tm, D), lambda i: (i, 0)),
                  pl.BlockSpec((tm, D), lambda i: (i, 0)),
                  pl.BlockSpec((tm, C), lambda i: (i, 0)),
                  pl.BlockSpec((tm, W), lambda i: (i, caq_col)),
                  pl.BlockSpec((M, W), lambda i: (i // n_s, 0)),
                  pl.BlockSpec((M, W), lambda i: (i // n_s, 0)),
                  const((1, W // CA_HEADS)),
                  pl.BlockSpec((tm, D), lambda i: (i, g_col)),
                  pl.BlockSpec((tm, D), lambda i: (i, g_col + 1)),
                  pl.BlockSpec((tm, D), lambda i: (i, g_col + 2)),
                  const((N_BRANCH, 1, D)),
                  const((D, D)), const((C, D)), const((D, D)), const((D, D)),
                  const((1, D)), const((D, LANES)), const((1, LANES)), const((tm, tm))],
        out_specs=[pl.BlockSpec((tm, D), lambda i: (i, 0)),
                   pl.BlockSpec((tm * SUBLANES, LANES), lambda i: (i, 0)),
                   pl.BlockSpec((tm, LANES), lambda i: (i, 0)),
                   pl.BlockSpec((1, LANES), lambda i: (0, 0))],
        out_shape=[jax.ShapeDtypeStruct((T, D), F32),
                   jax.ShapeDtypeStruct((T * SUBLANES, LANES), F32),
                   jax.ShapeDtypeStruct((T, LANES), F32),
                   jax.ShapeDtypeStruct((1, LANES), F32)],
        scratch_shapes=[pltpu.VMEM((1, LANES), F32)],
        compiler_params=_cparams(("arbitrary",)),
        name="mix_route",
    )(x2, yda, yml, proj, kmem, vmem, caqg, proj, proj, proj, bg, wda, wml, wca, wout, fg, rw, rb, tri)


ROW_DMA_UNROLL = 8
PAD_CHUNK_LOG2 = 5
PAD_CHUNK = 1 << PAD_CHUNK_LOG2


def _issue_rows(row_copy, tm):
    def issue(t, carry):
        for kk in range(TOP_K):
            row_copy(t, kk).start(priority=kk % 2)
        return carry

    lax.fori_loop(0, tm, issue, 0, unroll=ROW_DMA_UNROLL)


def _drain_rows(row_copy, tm):
    def drain(t, carry):
        for kk in range(TOP_K):
            row_copy(t, kk).wait()
        return carry

    lax.fori_loop(0, tm, drain, 0, unroll=ROW_DMA_UNROLL)


def _issue_and_drain_rows(row_copy, tm):
    _issue_rows(row_copy, tm)
    _drain_rows(row_copy, tm)


def _group_starts(cnt, tg, n_exp):
    lane_r = lax.broadcasted_iota(I32, (LANES, LANES), 0)
    lane_c = lax.broadcasted_iota(I32, (LANES, LANES), 1)
    padded = jnp.ceil(cnt * (1.0 / tg)) * tg
    padded_col = jnp.sum(jnp.where(lane_r == lane_c, padded, 0.0), axis=1, keepdims=True)
    start = jnp.sum(jnp.where(lane_r < lane_c, padded_col, 0.0), axis=0, keepdims=True)
    return start, start + padded


def _scatter_body(route_ref, route_next_ref, cnt_ref, xp_ref, pos_ref, te_ref, nu_ref, xs_hbm,
                  posv_ref, pos_smem, padv_ref, pad_smem, zero_ref, sem_p, sem_d, sem_z,
                  *, tm, tg, n_exp, n_tiles):
    i = pl.program_id(0)
    n = pl.num_programs(0)
    slot = i & 1
    lane = lax.broadcasted_iota(I32, (tm, LANES), 1)
    start, end = _group_starts(cnt_ref[...], tg, n_exp)

    def stage_positions(route, s):
        posm = jnp.zeros((tm, LANES), F32)
        for kk in range(TOP_K):
            e = jnp.sum(jnp.where(lane == kk, route, 0.0), axis=-1, keepdims=True)
            rank = jnp.sum(jnp.where(lane == 2 * TOP_K + kk, route, 0.0), axis=-1, keepdims=True)
            st = jnp.sum(jnp.where(lane == e.astype(I32), start, 0.0), axis=-1, keepdims=True)
            posm = jnp.where(lane == kk, (st + rank) * SUBLANES, posm)
        posv_ref[s] = posm.astype(I32).T[:SUBLANES]
        cp = pltpu.make_async_copy(posv_ref.at[s], pos_smem.at[s], sem_p)
        cp.start()
        cp.wait()

    @pl.when(i == 0)
    def _():
        stage_positions(route_ref[...], 0)
        tstart = (lax.broadcasted_iota(I32, (n_tiles, LANES), 0) * tg).astype(F32)
        lane_t = lax.broadcasted_iota(I32, (n_tiles, LANES), 1)
        done = jnp.where((lane_t < n_exp) & (end <= tstart), 1.0, 0.0)
        te = jnp.sum(done, axis=-1, keepdims=True)
        te_ref[...] = jnp.broadcast_to(te, (n_tiles, LANES)).astype(I32)
        n_used = jnp.max(end, axis=-1, keepdims=True) * (1.0 / tg)
        nu_ref[...] = jnp.broadcast_to(n_used, (1, LANES)).astype(I32)

        cnt = cnt_ref[...]
        padv_ref[...] = jnp.zeros(padv_ref.shape, I32)
        padv_ref[0:1, :] = ((start + cnt) * SUBLANES).astype(I32)
        padv_ref[1:2, :] = (end - start - cnt).astype(I32)
        zero_ref[...] = jnp.zeros(zero_ref.shape, F32)
        cpz = pltpu.make_async_copy(padv_ref, pad_smem, sem_p)
        cpz.start()
        cpz.wait()

        def pad_copy(off, rows):
            n = rows * SUBLANES
            return pltpu.make_async_copy(zero_ref.at[pl.ds(0, n)],
                                         xs_hbm.at[pl.ds(pl.multiple_of(off, SUBLANES), n)], sem_z)

        def for_each_pad_copy(fn):
            def per_expert(e, carry):
                base = pad_smem[0, e]
                n_pad = pad_smem[1, e]
                n_chunks = lax.shift_right_logical(n_pad, PAD_CHUNK_LOG2)

                def per_chunk(c, c2):
                    fn(pad_copy(base + c * (PAD_CHUNK * SUBLANES), PAD_CHUNK))
                    return c2
                lax.fori_loop(0, n_chunks, per_chunk, 0)

                def per_row(r, c2):
                    fn(pad_copy(base + (n_chunks * PAD_CHUNK + r) * SUBLANES, 1))
                    return c2
                lax.fori_loop(0, n_pad - n_chunks * PAD_CHUNK, per_row, 0)
                return carry
            lax.fori_loop(0, n_exp, per_expert, 0)

        for_each_pad_copy(lambda cp: cp.start())
        for_each_pad_copy(lambda cp: cp.wait())

    def row_copy(t, kk):
        dst = pl.multiple_of(pos_smem[slot, kk, t], SUBLANES)
        return pltpu.make_async_copy(xp_ref.at[pl.ds(pl.multiple_of(t * SUBLANES, SUBLANES), SUBLANES)],
                                     xs_hbm.at[pl.ds(dst, SUBLANES)], sem_d)

    _issue_rows(row_copy, tm)

    @pl.when(i + 1 < n)
    def _():
        stage_positions(route_next_ref[...], 1 - slot)

    pos_ref[...] = posv_ref[slot]
    _drain_rows(row_copy, tm)


def _scatter(route, cnt, xp, *, tm, tg, n_exp, n_tiles):
    T = route.shape[0]
    W = xp.shape[1]
    n_rows = n_tiles * tg * SUBLANES
    n_steps = T // tm
    return pl.pallas_call(
        functools.partial(_scatter_body, tm=tm, tg=tg, n_exp=n_exp, n_tiles=n_tiles),
        grid=(n_steps,),
        in_specs=[pl.BlockSpec((tm, LANES), lambda i: (i, 0)),
                  pl.BlockSpec((tm, LANES), lambda i: (jnp.minimum(i + 1, n_steps - 1), 0)),
                  pl.BlockSpec((1, LANES), lambda i: (0, 0)),
                  pl.BlockSpec((tm * SUBLANES, W), lambda i: (i, 0))],
        out_specs=[pl.BlockSpec((SUBLANES, tm), lambda i: (0, i)),
                   pl.BlockSpec((n_tiles, LANES), lambda i: (0, 0)),
                   pl.BlockSpec((1, LANES), lambda i: (0, 0)),
                   pl.BlockSpec(memory_space=pl.ANY)],
        out_shape=[jax.ShapeDtypeStruct((SUBLANES, T), I32),
                   jax.ShapeDtypeStruct((n_tiles, LANES), I32),
                   jax.ShapeDtypeStruct((1, LANES), I32),
                   jax.ShapeDtypeStruct((n_rows, W), F32)],
        scratch_shapes=[pltpu.VMEM((2, SUBLANES, tm), I32), pltpu.SMEM((2, SUBLANES, tm), I32),
                        pltpu.VMEM((SUBLANES, LANES), I32), pltpu.SMEM((SUBLANES, LANES), I32),
                        pltpu.VMEM((PAD_CHUNK * SUBLANES, LANES), F32),
                        pltpu.SemaphoreType.DMA, pltpu.SemaphoreType.DMA, pltpu.SemaphoreType.DMA],
        compiler_params=_cparams(("arbitrary",)),
        name="moe_scatter",
    )(route, route, cnt, xp)


def _experts_body(te_ref, nu_ref, xs_ref, wgu_ref, bgu_ref, wd_ref, bd_ref, y_ref, wgu16_ref, wd16_ref,
                  *, F, tg):
    i = pl.program_id(0)
    live = i < nu_ref[0]
    new_expert = (i == 0) | (te_ref[i] != te_ref[jnp.maximum(i - 1, 0)])

    @pl.when(live & new_expert)
    def _():
        wgu16_ref[...] = wgu_ref[...].astype(BF16)
        wd16_ref[...] = wd_ref[...].astype(BF16)

    @pl.when(live)
    def _():
        x = jnp.concatenate([c.astype(BF16) for c in _load_tile_rows(xs_ref, tg)], axis=1)
        h = jnp.dot(x, wgu16_ref[...], preferred_element_type=F32) + bgu_ref[...]
        gate = jnp.minimum(h[:, :F], SWIGLU_LIMIT)
        up = jnp.clip(h[:, F:], -SWIGLU_LIMIT, SWIGLU_LIMIT)
        a = (up + 1.0) * (gate * _sigmoid(SWIGLU_ALPHA * gate))
        y = jnp.dot(a.astype(BF16), wd16_ref[...], preferred_element_type=F32) + bd_ref[...]
        _store_tile_rows(y_ref, y)


def _experts(te, n_used, xs, wgu, bgu, wd, bd, *, tg, n_tiles):
    n_rows, W = xs.shape
    E, D, F2 = wgu.shape
    F = F2 // 2
    row = lambda i, te, nu: (jnp.minimum(i, nu[0] - 1), 0)
    exp3 = lambda i, te, nu: (te[jnp.minimum(i, nu[0] - 1)], 0, 0)
    grid_spec = pltpu.PrefetchScalarGridSpec(
        num_scalar_prefetch=2,
        grid=(n_tiles,),
        in_specs=[pl.BlockSpec((tg * SUBLANES, W), row),
                  pl.BlockSpec((None, D, F2), exp3),
                  pl.BlockSpec((None, 1, F2), exp3),
                  pl.BlockSpec((None, F, D), exp3),
                  pl.BlockSpec((None, 1, D), exp3)],
        out_specs=pl.BlockSpec((tg * SUBLANES, W), row),
        scratch_shapes=[pltpu.VMEM((D, F2), BF16), pltpu.VMEM((F, D), BF16)],
    )
    return pl.pallas_call(
        functools.partial(_experts_body, F=F, tg=tg),
        grid_spec=grid_spec,
        out_shape=jax.ShapeDtypeStruct((n_rows, W), F32),
        compiler_params=_cparams(("arbitrary",)),
        name="moe_experts",
    )(te, n_used, xs, wgu, bgu, wd, bd)


def _combine_body(pos_ref, route_ref, h_ref, y_hbm, o_ref, pos_smem, buf_ref, sem_p, sem_d, *, tm):
    i = pl.program_id(0)
    n = pl.num_programs(0)
    slot = i & 1

    def gather_rows(step, s):
        cp = pltpu.make_async_copy(pos_ref.at[:, pl.ds(pl.multiple_of(step * tm, tm), tm)],
                                   pos_smem.at[s], sem_p)
        cp.start()
        cp.wait()
        return functools.partial(row_copy, s)

    def row_copy(s, t, kk):
        src = pl.multiple_of(pos_smem[s, kk, t], SUBLANES)
        return pltpu.make_async_copy(
            y_hbm.at[pl.ds(src, SUBLANES)],
            buf_ref.at[s, kk, pl.ds(pl.multiple_of(t * SUBLANES, SUBLANES), SUBLANES)], sem_d.at[s])

    @pl.when(i == 0)
    def _():
        _issue_rows(gather_rows(0, 0), tm)

    @pl.when(i + 1 < n)
    def _():
        _issue_rows(gather_rows(i + 1, 1 - slot), tm)

    _drain_rows(functools.partial(row_copy, slot), tm)

    route = route_ref[...]
    lane = lax.broadcasted_iota(I32, (tm, LANES), 1)
    ws = [jnp.sum(jnp.where(lane == TOP_K + kk, route, 0.0), axis=-1, keepdims=True)
          for kk in range(TOP_K)]
    for c in range(SUBLANES):
        sl = slice(c * LANES, (c + 1) * LANES)
        acc = h_ref[:, sl]
        for kk in range(TOP_K):
            acc = acc + ws[kk] * buf_ref[slot, kk, pl.ds(c, tm, stride=SUBLANES), :]
        o_ref[:, sl] = acc


def _combine(pos, route, h1, y, *, tm):
    T, D = h1.shape
    W = y.shape[1]
    return pl.pallas_call(
        functools.partial(_combine_body, tm=tm),
        grid=(T // tm,),
        in_specs=[pl.BlockSpec((SUBLANES, T), lambda i: (0, 0)),
                  pl.BlockSpec((tm, LANES), lambda i: (i, 0)),
                  pl.BlockSpec((tm, D), lambda i: (i, 0)),
                  pl.BlockSpec(memory_space=pl.ANY)],
        out_specs=pl.BlockSpec((tm, D), lambda i: (i, 0)),
        out_shape=jax.ShapeDtypeStruct((T, D), F32),
        scratch_shapes=[pltpu.SMEM((2, SUBLANES, tm), I32),
                        pltpu.VMEM((2, TOP_K, tm * SUBLANES, W), F32),
                        pltpu.SemaphoreType.DMA, pltpu.SemaphoreType.DMA((2,))],
        compiler_params=_cparams(("arbitrary",)),
        name="moe_combine",
    )(pos, route, h1, y)


def _blockdiag_dense(w, width):
    nb, bs, _ = w.shape
    per = width // bs
    wt = w.reshape(nb // per, per, bs, bs)
    eye = jnp.eye(per, dtype=w.dtype)
    dense = jnp.einsum('gpio,pq->gpiqo', wt, eye)
    return dense.reshape(nb // per, width, width)


def _rope_tables(S):
    half = DA_HEAD_DIM // 2
    inv = ROPE_THETA ** (-(jnp.arange(half, dtype=F32) * 2.0 / DA_HEAD_DIM))
    ang = jnp.arange(S, dtype=F32)[:, None] * inv[None, :]
    cos = jnp.tile(jnp.cos(ang), (1, LANES // half))
    sign = jnp.asarray(np.where(_HEAD_LANE_HALF == 0, -1.0, 1.0), F32)
    sin = jnp.tile(jnp.sin(ang), (1, LANES // half)) * sign[None, :]
    return cos, sin


_HEAD_LANE = np.arange(LANES)
_HEAD_LANE_HALF = _HEAD_LANE // (LANES // 2)
_HEAD_LANE_COMP = (_HEAD_LANE // (DA_HEAD_DIM // 2)) % 2
_HEAD_LANE_DIM = _HEAD_LANE_HALF * (DA_HEAD_DIM // 2) + _HEAD_LANE % (DA_HEAD_DIM // 2)
_HEAD_LANE_SRC = _HEAD_LANE_COMP * DA_HEAD_DIM + _HEAD_LANE_DIM


def _tile(n, pref):
    return pref if n % pref == 0 else n


def _layer(h2, mem2, B, S, lambda_init, attn_norm_g, w_in, b_gate, da_q_norm_g, da_k_norm_g,
           da_lambda_q1, da_lambda_k1, da_lambda_q2, da_lambda_k2, da_subln_g, ml_conv_w, ml_conv_b,
           ml_wq, ml_wk, ml_wv, ml_w_if, ml_b_if, ml_out_norm_g, ml_skip, mem_norm_g, ca_w_kv,
           ca_q_norm_g, ca_k_norm_g, w_branch_da, w_branch_ml, w_branch_ca, w_out, ffn_norm_g,
           router_w, router_b, w_gate_up, b_gate_up, w_down, b_down):
    T, D = h2.shape
    M = mem2.shape[0] // B
    QK = DA_HEADS * 2 * DA_HEAD_DIM
    VW = DA_HEADS * DA_V_DIM
    C = ml_conv_w.shape[1]
    CAW = ca_w_kv.shape[1] // 2
    E = router_w.shape[1]

    o = [0, QK, 2 * QK, 2 * QK + VW, 2 * QK + VW + C, 2 * QK + VW + 2 * C, 2 * QK + VW + 2 * C + CAW]
    head_perm = (np.arange(DA_HEADS)[:, None] * LANES + _HEAD_LANE_SRC[None, :]).reshape(-1)
    w_re = jnp.concatenate([w_in[:, o[0]:o[1]][:, head_perm], w_in[:, o[1]:o[2]][:, head_perm],
                            w_in[:, o[2]:o[3]], w_in[:, o[5]:o[6]], w_in[:, o[3]:o[5]], w_in[:, o[6]:]],
                           axis=1).astype(BF16)
    tn = 1024
    col_v, col_caq, col_mlx, col_mlz, col_gate = 2 * QK, 2 * QK + VW, 2 * QK + VW + CAW, \
        2 * QK + VW + CAW + C, 2 * QK + VW + CAW + 2 * C
    proj = _inproj(h2, attn_norm_g[None, :], w_re, tm=_tile(T, 2048), tn=tn)

    g2 = jnp.stack([da_q_norm_g[_HEAD_LANE_DIM], da_k_norm_g[_HEAD_LANE_DIM]])[:, None, :]
    cos_t, sin_t = _rope_tables(S)
    gm = jnp.asarray(_HEAD_LANE_COMP[:, None] == _HEAD_LANE_COMP[None, :], BF16)
    qk = _qkrope(proj, g2, cos_t, sin_t, gm, T=T, S=S, width=QK, tm=_tile(S, 1024))
    y_da = _diff_attention(qk, proj, da_lambda_q1[None, :], da_lambda_k1[None, :], da_lambda_q2[None, :],
                           da_lambda_k2[None, :], da_subln_g[None, :], da_q_norm_g[None, :],
                           da_k_norm_g[None, :], B=B, S=S, tq=_tile(S, 1024),
                           v_col0=col_v // LANES, q_scale=DA_Q_SCALE, lambda_init=lambda_init)

    bd = jnp.stack([_blockdiag_dense(ml_wq, MXU_DIM), _blockdiag_dense(ml_wk, MXU_DIM),
                    _blockdiag_dense(ml_wv, MXU_DIM)]).astype(BF16)
    wif = jnp.pad(ml_w_if.reshape(3, C, 2 * ML_HEADS), ((0, 0), (0, 0), (0, LANES - 2 * ML_HEADS))).astype(BF16)
    bif = jnp.pad(ml_b_if, (0, LANES - 2 * ML_HEADS))[None, :]
    xc, mq, mk, mv, gates = _mlpre(proj, ml_conv_w, ml_conv_b[None, :], bd, wif, bif, B=B, S=S, C=C,
                                   tm=_tile(S, 512), x_col=col_mlx // C)
    dml = C // ML_HEADS
    y_ml = _mlstm(mq, mk, mv, gates, xc, proj, ml_out_norm_g[None, :], ml_skip[None, :], B=B, S=S, C=C,
                  L=_tile(S, 256), z_col0=col_mlz // dml, hpb=4)

    kmem, vmem = _memkv(mem2, mem_norm_g[None, :], ca_w_kv.astype(BF16), ca_k_norm_g[None, :], B=B, M=M, W=CAW)

    tmx = _tile(S, 512)
    tri = (jnp.arange(tmx)[None, :] < jnp.arange(tmx)[:, None]).astype(BF16)
    rw = jnp.pad(router_w, ((0, 0), (0, LANES - E))).astype(BF16)
    rb = jnp.pad(router_b, (0, LANES - E))[None, :]
    h1, xp, route, cnt = _mix(h2, y_da, y_ml, kmem, vmem, ca_q_norm_g[None, :], proj,
                              b_gate.reshape(N_BRANCH, 1, D),
                              w_branch_da.astype(BF16), w_branch_ml.astype(BF16),
                              w_branch_ca.astype(BF16), w_out.astype(BF16), ffn_norm_g[None, :], rw, rb, tri,
                              tm=tmx, S=S, caq_col=col_caq // CAW, g_col=col_gate // D, n_exp=E)

    tg = _tile(T, 512)
    n_tiles = (T * TOP_K) // tg + E
    assert D == SUBLANES * LANES, "MoE rows are moved as one (8,128) f32 tile each"
    pos, te, n_used, xs = _scatter(route, cnt, xp, tm=_tile(T, 1024), tg=tg, n_exp=E, n_tiles=n_tiles)
    y = _experts(te[:, 0], n_used[0, :1], xs, w_gate_up, b_gate_up[:, None, :],
                 w_down, b_down[:, None, :], tg=tg, n_tiles=n_tiles)
    return _combine(pos, route, h1, y, tm=_tile(T, 512))


def kernel(x, mem, attn_norm_g, w_in, b_gate, da_q_norm_g, da_k_norm_g, da_lambda_q1, da_lambda_k1, da_lambda_q2, da_lambda_k2, da_subln_g, ml_conv_w, ml_conv_b, ml_wq, ml_wk, ml_wv, ml_w_if, ml_b_if, ml_out_norm_g, ml_skip, mem_norm_g, ca_w_kv, ca_q_norm_g, ca_k_norm_g, w_branch_da, w_branch_ml, w_branch_ca, w_out, ffn_norm_g, router_w, router_b, w_gate_up, b_gate_up, w_down, b_down):
    B, S, D = x.shape
    depth = w_in.shape[0]
    h2 = x.reshape(B * S, D)
    mem2 = mem.reshape(B * mem.shape[1], D)
    params = (attn_norm_g, w_in, b_gate, da_q_norm_g, da_k_norm_g, da_lambda_q1, da_lambda_k1,
              da_lambda_q2, da_lambda_k2, da_subln_g, ml_conv_w, ml_conv_b, ml_wq, ml_wk, ml_wv, ml_w_if,
              ml_b_if, ml_out_norm_g, ml_skip, mem_norm_g, ca_w_kv, ca_q_norm_g, ca_k_norm_g, w_branch_da,
              w_branch_ml, w_branch_ca, w_out, ffn_norm_g, router_w, router_b, w_gate_up, b_gate_up,
              w_down, b_down)
    for l in range(depth):
        lambda_init = 0.8 - 0.6 * math.exp(-0.3 * l)
        h2 = _layer(h2, mem2, B, S, lambda_init, *[p[l] for p in params])
    return h2.reshape(B, S, D)
```
